```python
import jax, jax.numpy as jnp
from jax import lax
import numpy as np

D_MODEL = 1024
BATCH = 16
SEQ = 2048
DEPTH = 4

MEM_LEN = 256
CONV_DIM = 512
CONV_WIDTH = 3
MLA_HEADS = 8
QK_NOPE_DIM = 64
QK_ROPE_DIM = 32
V_HEAD_DIM = 64
Q_RANK = 384
KV_RANK = 256
ROPE_BASE = 10000.0
Q_BLOCK = 128
XATTN_HEADS = 4
XATTN_HEAD_DIM = D_MODEL // XATTN_HEADS
N_EXPERTS = 64
TOP_K = 8
N_GROUPS = 8
TOPK_GROUPS = 4
EXPERT_DIM = 256
SHARED_DIM = 256
ROUTE_SCALE = 2.5
LN_EPS = 1e-5
RMS_EPS = 1e-6
DEEPNORM_ALPHA = (2 * DEPTH) ** 0.25
DEEPNORM_BETA = (8 * DEPTH) ** -0.25
IN_SPLIT_SIZES = (CONV_DIM, CONV_DIM, CONV_DIM, Q_RANK, KV_RANK, QK_ROPE_DIM, D_MODEL, D_MODEL)
IN_DIM = 3 * CONV_DIM + Q_RANK + KV_RANK + QK_ROPE_DIM + 2 * D_MODEL

kernel_name = 'hybrid_conv_mla_memxattn_moe_deepnorm'


def _split_points(sizes):
    pts, acc = [], 0
    for s in sizes[:-1]:
        acc += s
        pts.append(acc)
    return pts


def layer_norm(x, g, b):
    xf = x.astype(jnp.float32)
    mu = jnp.mean(xf, axis=-1, keepdims=True)
    var = jnp.mean(jnp.square(xf - mu), axis=-1, keepdims=True)
    return ((xf - mu) * lax.rsqrt(var + LN_EPS) * g.astype(jnp.float32) + b.astype(jnp.float32)).astype(x.dtype)


def rms_norm(x, g):
    xf = x.astype(jnp.float32)
    ms = jnp.mean(jnp.square(xf), axis=-1, keepdims=True)
    return (xf * lax.rsqrt(ms + RMS_EPS) * g.astype(jnp.float32)).astype(x.dtype)


def rope_tables(positions):
    half = QK_ROPE_DIM // 2
    inv_freq = ROPE_BASE ** (-jnp.arange(half, dtype=jnp.float32) / half)
    ang = positions.astype(jnp.float32)[..., None] * inv_freq
    return jnp.cos(ang), jnp.sin(ang)


def apply_rope(x, cos, sin):
    x1, x2 = jnp.split(x.astype(jnp.float32), 2, axis=-1)
    return jnp.concatenate([x1 * cos - x2 * sin, x2 * cos + x1 * sin], axis=-1).astype(x.dtype)


def short_conv_branch(b_gate, c_gate, h, w_conv, w_conv_out):
    u = c_gate * h
    z = lax.conv_general_dilated(u, w_conv.astype(u.dtype), window_strides=(1,),
                                 padding=[(CONV_WIDTH - 1, 0)],
                                 dimension_numbers=('NWC', 'WIO', 'NWC'),
                                 feature_group_count=CONV_DIM)
    return (b_gate * z) @ w_conv_out


def mla_branch(c_q, c_kv, k_rope, cos, sin, q_norm, w_uq, kv_norm, w_uk, w_uv, w_attn_out):
    bn, s_len, _ = c_q.shape
    q = (rms_norm(c_q, q_norm) @ w_uq).reshape(bn, s_len, MLA_HEADS, QK_NOPE_DIM + QK_ROPE_DIM)
    q_nope = q[..., :QK_NOPE_DIM]
    q_rope = apply_rope(q[..., QK_NOPE_DIM:], cos[:, :, None, :], sin[:, :, None, :])
    ckv = rms_norm(c_kv, kv_norm)
    k_nope = (ckv @ w_uk).reshape(bn, s_len, MLA_HEADS, QK_NOPE_DIM)
    v = (ckv @ w_uv).reshape(bn, s_len, MLA_HEADS, V_HEAD_DIM)
    k_rope = apply_rope(k_rope, cos, sin)
    scale = (QK_NOPE_DIM + QK_ROPE_DIM) ** -0.5
    n_blk = s_len // Q_BLOCK
    qn_blk = q_nope.reshape(bn, n_blk, Q_BLOCK, MLA_HEADS, QK_NOPE_DIM).transpose(1, 0, 2, 3, 4)
    qr_blk = q_rope.reshape(bn, n_blk, Q_BLOCK, MLA_HEADS, QK_ROPE_DIM).transpose(1, 0, 2, 3, 4)
    k_pos = jnp.arange(s_len)

    def attend(args):
        qn, qr, blk = args
        sc = (jnp.einsum('bqhd,bkhd->bhqk', qn, k_nope)
              + jnp.einsum('bqhr,bkr->bhqk', qr, k_rope)).astype(jnp.float32) * scale
        q_pos = blk * Q_BLOCK + jnp.arange(Q_BLOCK)
        sc = jnp.where(k_pos[None, :] <= q_pos[:, None], sc, -jnp.inf)
        p = jax.nn.softmax(sc, axis=-1).astype(v.dtype)
        return jnp.einsum('bhqk,bkhd->bqhd', p, v)

    o = lax.map(attend, (qn_blk, qr_blk, jnp.arange(n_blk)))
    o = o.transpose(1, 0, 2, 3, 4).reshape(bn, s_len, MLA_HEADS * V_HEAD_DIM)
    return o @ w_attn_out


def parallel_mixer(x, cos, sin, w_in, w_conv, w_conv_out, q_norm, w_uq, kv_norm, w_uk, w_uv,
                   w_attn_out, w_mix_out):
    proj = x @ w_in
    b_gate, c_gate, h, c_q, c_kv, k_rope, g_a, g_b = jnp.split(proj, _split_points(IN_SPLIT_SIZES), axis=-1)
    y_a = short_conv_branch(b_gate, c_gate, h, w_conv, w_conv_out)
    y_b = mla_branch(c_q, c_kv, k_rope, cos, sin, q_norm, w_uq, kv_norm, w_uk, w_uv, w_attn_out)
    y = jax.nn.sigmoid(g_a) * y_a + jax.nn.sigmoid(g_b) * y_b
    return y @ w_mix_out


def memory_cross_attention(x, mem, w_xq, w_xk, w_xv, w_xo):
    bn, s_len, _ = x.shape
    m_len = mem.shape[1]
    q = (x @ w_xq).reshape(bn, s_len, XATTN_HEADS, XATTN_HEAD_DIM)
    k = (mem @ w_xk).reshape(bn, m_len, XATTN_HEADS, XATTN_HEAD_DIM)
    v = (mem @ w_xv).reshape(bn, m_len, XATTN_HEADS, XATTN_HEAD_DIM)
    sc = jnp.einsum('bshd,bmhd->bhsm', q, k).astype(jnp.float32) * (XATTN_HEAD_DIM ** -0.5)
    p = jax.nn.softmax(sc, axis=-1).astype(v.dtype)
    o = jnp.einsum('bhsm,bmhd->bshd', p, v).reshape(bn, s_len, D_MODEL)
    return o @ w_xo


def route(x2, w_router, router_bias):
    t = x2.shape[0]
    scores = jax.nn.sigmoid((x2 @ w_router).astype(jnp.float32))
    sel = scores + router_bias.astype(jnp.float32)
    grouped = sel.reshape(t, N_GROUPS, N_EXPERTS // N_GROUPS)
    group_score = jnp.sum(lax.top_k(grouped, 2)[0], axis=-1)
    _, top_groups = lax.top_k(group_score, TOPK_GROUPS)
    group_mask = jnp.sum(jax.nn.one_hot(top_groups, N_GROUPS, dtype=jnp.float32), axis=-2) > 0
    expert_mask = jnp.repeat(group_mask, N_EXPERTS // N_GROUPS, axis=-1)
    _, idx = lax.top_k(jnp.where(expert_mask, sel, -jnp.inf), TOP_K)
    w = jnp.take_along_axis(scores, idx, axis=-1)
    w = w / jnp.sum(w, axis=-1, keepdims=True) * ROUTE_SCALE
    return jnp.sum(jax.nn.one_hot(idx, N_EXPERTS, dtype=jnp.float32) * w[..., None], axis=-2)


def moe_ffn(x, w_router, router_bias, w_gate, w_up, w_down, ws_gate, ws_up, ws_down):
    bn, s_len, d = x.shape
    gates = route(x.reshape(-1, d), w_router, router_bias).reshape(bn, s_len, N_EXPERTS).astype(x.dtype)
    n_blk = s_len // Q_BLOCK
    xb = x.reshape(bn, n_blk, Q_BLOCK, d).transpose(1, 0, 2, 3)
    gb = gates.reshape(bn, n_blk, Q_BLOCK, N_EXPERTS).transpose(1, 0, 2, 3)

    def experts(args):
        xt, gt = args
        hid = jax.nn.silu(jnp.einsum('btd,edf->btef', xt, w_gate)) * jnp.einsum('btd,edf->btef', xt, w_up)
        return jnp.einsum('btef,efd->btd', hid * gt[..., None], w_down)

    routed = lax.map(experts, (xb, gb)).transpose(1, 0, 2, 3).reshape(bn, s_len, d)
    shared = (jax.nn.silu(x @ ws_gate) * (x @ ws_up)) @ ws_down
    return routed + shared


def _normal(key, shape, fan_in, scale=1.0):
    return jax.random.normal(key, shape, jnp.float32) * (scale * fan_in ** -0.5)


def setup_inputs(seed: int = 0) -> dict:
    key = jax.random.key(seed)
    ks = iter(jax.random.split(key, 48))
    L, D = DEPTH, D_MODEL
    beta = DEEPNORM_BETA
    gain = lambda k, n: 1.0 + 0.05 * jax.random.normal(k, (L, n), jnp.float32)
    bias = lambda k, n: 0.02 * jax.random.normal(k, (L, n), jnp.float32)
    x = jax.random.normal(next(ks), (BATCH, SEQ, D), jnp.float32)
    mem = jax.random.normal(next(ks), (BATCH, MEM_LEN, D), jnp.float32)
    offsets = jax.random.randint(next(ks), (BATCH, 1), 0, 4096, dtype=jnp.int32)
    positions = offsets + jnp.arange(SEQ, dtype=jnp.int32)[None, :]
    return {
        'x': x,
        'mem': mem,
        'positions': positions,
        'w_in': _normal(next(ks), (L, D, IN_DIM), D),
        'w_conv': _normal(next(ks), (L, CONV_WIDTH, 1, CONV_DIM), CONV_WIDTH),
        'w_conv_out': _normal(next(ks), (L, CONV_DIM, D), CONV_DIM, beta),
        'q_norm': gain(next(ks), Q_RANK),
        'w_uq': _normal(next(ks), (L, Q_RANK, MLA_HEADS * (QK_NOPE_DIM + QK_ROPE_DIM)), Q_RANK),
        'kv_norm': gain(next(ks), KV_RANK),
        'w_uk': _normal(next(ks), (L, KV_RANK, MLA_HEADS * QK_NOPE_DIM), KV_RANK),
        'w_uv': _normal(next(ks), (L, KV_RANK, MLA_HEADS * V_HEAD_DIM), KV_RANK, beta),
        'w_attn_out': _normal(next(ks), (L, MLA_HEADS * V_HEAD_DIM, D), MLA_HEADS * V_HEAD_DIM, beta),
        'w_mix_out': _normal(next(ks), (L, D, D), D, beta),
        'ln1_g': gain(next(ks), D),
        'ln1_b': bias(next(ks), D),
        'w_xq': _normal(next(ks), (L, D, D), D),
        'w_xk': _normal(next(ks), (L, D, D), D),
        'w_xv': _normal(next(ks), (L, D, D), D, beta),
        'w_xo': _normal(next(ks), (L, D, D), D, beta),
        'ln2_g': gain(next(ks), D),
        'ln2_b': bias(next(ks), D),
        'w_router': _normal(next(ks), (L, D, N_EXPERTS), D),
        'router_bias': 0.01 * jax.random.normal(next(ks), (L, N_EXPERTS), jnp.float32),
        'w_gate': _normal(next(ks), (L, N_EXPERTS, D, EXPERT_DIM), D),
        'w_up': _normal(next(ks), (L, N_EXPERTS, D, EXPERT_DIM), D),
        'w_down': _normal(next(ks), (L, N_EXPERTS, EXPERT_DIM, D), EXPERT_DIM, beta),
        'ws_gate': _normal(next(ks), (L, D, SHARED_DIM), D),
        'ws_up': _normal(next(ks), (L, D, SHARED_DIM), D),
        'ws_down': _normal(next(ks), (L, SHARED_DIM, D), SHARED_DIM, beta),
        'ln3_g': gain(next(ks), D),
        'ln3_b': bias(next(ks), D),
    }


def reference(x, mem, positions, w_in, w_conv, w_conv_out, q_norm, w_uq, kv_norm, w_uk, w_uv,
              w_attn_out, w_mix_out, ln1_g, ln1_b, w_xq, w_xk, w_xv, w_xo, ln2_g, ln2_b,
              w_router, router_bias, w_gate, w_up, w_down, ws_gate, ws_up, ws_down, ln3_g, ln3_b):
    cos, sin = rope_tables(positions)
    for l in range(DEPTH):
        mix = parallel_mixer(x, cos, sin, w_in[l], w_conv[l], w_conv_out[l], q_norm[l], w_uq[l],
                             kv_norm[l], w_uk[l], w_uv[l], w_attn_out[l], w_mix_out[l])
        x = layer_norm(DEEPNORM_ALPHA * x + mix, ln1_g[l], ln1_b[l])
        xat = memory_cross_attention(x, mem, w_xq[l], w_xk[l], w_xv[l], w_xo[l])
        x = layer_norm(DEEPNORM_ALPHA * x + xat, ln2_g[l], ln2_b[l])
        ffn = moe_ffn(x, w_router[l], router_bias[l], w_gate[l], w_up[l], w_down[l],
                      ws_gate[l], ws_up[l], ws_down[l])
        x = layer_norm(DEEPNORM_ALPHA * x + ffn, ln3_g[l], ln3_b[l])
    return x
```

```python
import functools

import jax
import jax.numpy as jnp
from jax import lax
from jax.experimental import pallas as pl
from jax.experimental.pallas import tpu as pltpu

CONV_DIM = 512
CONV_WIDTH = 3
MLA_HEADS = 8
QK_NOPE_DIM = 64
QK_ROPE_DIM = 32
V_HEAD_DIM = 64
Q_RANK = 384
KV_RANK = 256
ROPE_BASE = 10000.0
XATTN_HEADS = 4
N_EXPERTS = 64
TOP_K = 8
N_GROUPS = 8
TOPK_GROUPS = 4
GROUP_SIZE = N_EXPERTS // N_GROUPS
ROUTE_SCALE = 2.5
LN_EPS = 1e-5
RMS_EPS = 1e-6

LANES = 128
HALO_ROWS = 8
VMEM_LIMIT = 56 * 1024 * 1024

BF16 = jnp.bfloat16
F32 = jnp.float32


def _dot(a, b):
    return jnp.dot(a, b, preferred_element_type=F32)


def _dot_nt(a, b):
    return lax.dot_general(a, b, (((1,), (1,)), ((), ())), preferred_element_type=F32)


def _layer_norm(x, g, b):
    mu = jnp.mean(x, axis=-1, keepdims=True)
    xc = x - mu
    var = jnp.mean(xc * xc, axis=-1, keepdims=True)
    return xc * lax.rsqrt(var + LN_EPS) * g + b


def _rms_norm(x, g):
    ms = jnp.mean(x * x, axis=-1, keepdims=True)
    return x * lax.rsqrt(ms + RMS_EPS) * g


def _silu(x):
    return x * jax.nn.sigmoid(x)


_C_BCH = 0
_C_CQ = _C_BCH + 3 * CONV_DIM
_C_CKV = _C_CQ + Q_RANK
_C_KR = _C_CKV + KV_RANK
_C_GA = _C_KR + 2 * LANES
_C_GB = _C_GA + 1024


def _mixer_in_kernel(x_ref, xh_ref, tab_ref, wx_ref, wconv_ref, wco_ref, qn_ref, wq_ref,
                     kvn_ref, wkv_ref, eplace_ref,
                     yag_ref, sgb_ref, q_ref, k_ref, v_ref, *, d_model):
    i = pl.program_id(1)
    d = d_model
    ts = x_ref.shape[1]
    xb = x_ref[0].astype(BF16)

    bch = _dot(xb, wx_ref[:, _C_BCH:_C_BCH + 3 * CONV_DIM])
    b_gate = bch[:, :CONV_DIM]
    u = bch[:, CONV_DIM:2 * CONV_DIM] * bch[:, 2 * CONV_DIM:]
    xh = xh_ref[0].astype(BF16)
    hp = _dot(xh, wx_ref[:, _C_BCH + CONV_DIM:_C_BCH + 3 * CONV_DIM])
    uh = hp[:, :CONV_DIM] * hp[:, CONV_DIM:]
    uh = jnp.where(i == 0, 0.0, uh)
    row = lax.broadcasted_iota(jnp.int32, (ts, CONV_DIM), 0)
    u1 = jnp.where(row == 0, uh[HALO_ROWS - 1:HALO_ROWS], pltpu.roll(u, 1, 0))
    u2 = jnp.where(row == 0, uh[HALO_ROWS - 2:HALO_ROWS - 1],
                   jnp.where(row == 1, uh[HALO_ROWS - 1:HALO_ROWS], pltpu.roll(u, 2, 0)))
    wc = wconv_ref[0]
    z = wc[0:1] * u2 + wc[1:2] * u1 + wc[2:3] * u
    y_a = _dot((b_gate * z).astype(BF16), wco_ref[0])
    g_a = _dot(xb, wx_ref[:, _C_GA:_C_GA + d])
    yag_ref[0] = (jax.nn.sigmoid(g_a) * y_a).astype(BF16)
    g_b = _dot(xb, wx_ref[:, _C_GB:_C_GB + d])
    sgb_ref[0] = jax.nn.sigmoid(g_b).astype(BF16)

    tab = tab_ref[0]
    t_a, t_b = tab[:, 0:LANES], tab[:, LANES:2 * LANES]
    t_ck, t_sk = tab[:, 2 * LANES:3 * LANES], tab[:, 3 * LANES:4 * LANES]
    c_q = _dot(xb, wx_ref[:, _C_CQ:_C_CQ + Q_RANK])
    cqn = _rms_norm(c_q, qn_ref[0]).astype(BF16)
    hw = MLA_HEADS * LANES
    q = _dot(cqn, wq_ref[0, :, :hw])
    qp = _dot(cqn, wq_ref[0, :, hw:])
    for h in range(MLA_HEADS):
        sl = slice(h * LANES, (h + 1) * LANES)
        q_ref[0, :, sl] = (q[:, sl] * t_a + qp[:, sl] * t_b).astype(BF16)

    c_kv = _dot(xb, wx_ref[:, _C_CKV:_C_CKV + KV_RANK])
    ckvn = _rms_norm(c_kv, kvn_ref[0]).astype(BF16)
    kr2 = _dot(xb, wx_ref[:, _C_KR:_C_KR + 2 * LANES])
    kr_rot = (kr2[:, :LANES] * t_ck + kr2[:, LANES:] * t_sk).astype(BF16)
    k = _dot(ckvn, wkv_ref[0, :, :hw]) + _dot(kr_rot, eplace_ref[...])
    k_ref[0] = k.astype(BF16)
    v_ref[0] = _dot(ckvn, wkv_ref[0, :, hw:]).astype(BF16)


def _mixer_in(l, x, tabs, wx, wconv, wco, qn, wq, kvn, wkv, eplace, ts):
    bn, s_len, d = x.shape
    n_t = s_len // ts
    hw = MLA_HEADS * LANES
    vw = MLA_HEADS * V_HEAD_DIM
    tile = lambda w: pl.BlockSpec((1, ts, w), lambda b, i: (b, i, 0))
    lay = lambda a: pl.BlockSpec((1,) + a.shape[1:], lambda b, i: (l,) + (0,) * (a.ndim - 1))
    halo = pl.BlockSpec((1, HALO_ROWS, d),
                        lambda b, i: (b, jnp.maximum(i * (ts // HALO_ROWS) - 1, 0), 0))
    return pl.pallas_call(
        functools.partial(_mixer_in_kernel, d_model=d),
        grid=(bn, n_t),
        in_specs=[tile(d), halo, tile(4 * LANES),
                  pl.BlockSpec((None,) + wx.shape[1:], lambda b, i: (l, 0, 0)),
                  lay(wconv), lay(wco), lay(qn), lay(wq), lay(kvn), lay(wkv),
                  pl.BlockSpec(eplace.shape, lambda b, i: (0, 0))],
        out_specs=[tile(d), tile(d), tile(hw), tile(hw), tile(vw)],
        out_shape=[jax.ShapeDtypeStruct((bn, s_len, d), BF16),
                   jax.ShapeDtypeStruct((bn, s_len, d), BF16),
                   jax.ShapeDtypeStruct((bn, s_len, hw), BF16),
                   jax.ShapeDtypeStruct((bn, s_len, hw), BF16),
                   jax.ShapeDtypeStruct((bn, s_len, vw), BF16)],
        compiler_params=pltpu.CompilerParams(
            dimension_semantics=("parallel", "arbitrary"), vmem_limit_bytes=VMEM_LIMIT),
        name="mixer_in",
    )(x, x, tabs, wx, wconv, wco, qn, wq, kvn, wkv, eplace)


def _attn_kernel(q_ref, k_ref, v_ref, o_ref, *, tq):
    s_len = q_ref.shape[1]
    n_q = s_len // tq
    lane = lax.broadcasted_iota(jnp.int32, (tq, 2 * V_HEAD_DIM), 1)
    for qi in range(n_q):
        kv_len = (qi + 1) * tq
        rows = slice(qi * tq, (qi + 1) * tq)
        v2 = v_ref[0, :kv_len, :]
        r = lax.broadcasted_iota(jnp.int32, (tq, kv_len), 0) + qi * tq
        c = lax.broadcasted_iota(jnp.int32, (tq, kv_len), 1)
        outs = []
        for h in range(2):
            sl = slice(h * LANES, (h + 1) * LANES)
            s = _dot_nt(q_ref[0, rows, sl], k_ref[0, :kv_len, sl])
            s = jnp.where(c <= r, s, -jnp.inf)
            m = jnp.max(s, axis=-1, keepdims=True)
            p = jnp.exp(s - m)
            den = jnp.sum(p, axis=-1, keepdims=True)
            outs.append(_dot(p.astype(BF16), v2) / den)
        o_ref[0, rows, :] = jnp.where(lane < V_HEAD_DIM, outs[0], outs[1]).astype(BF16)


def _attention(q, k, v, tq):
    bn, s_len, _ = q.shape
    n_pairs = MLA_HEADS // 2
    return pl.pallas_call(
        functools.partial(_attn_kernel, tq=tq),
        grid=(bn, n_pairs),
        in_specs=[pl.BlockSpec((1, s_len, 2 * LANES), lambda b, h: (b, 0, h)),
                  pl.BlockSpec((1, s_len, 2 * LANES), lambda b, h: (b, 0, h)),
                  pl.BlockSpec((1, s_len, 2 * V_HEAD_DIM), lambda b, h: (b, 0, h))],
        out_specs=pl.BlockSpec((1, s_len, 2 * V_HEAD_DIM), lambda b, h: (b, 0, h)),
        out_shape=jax.ShapeDtypeStruct((bn, s_len, MLA_HEADS * V_HEAD_DIM), BF16),
        compiler_params=pltpu.CompilerParams(
            dimension_semantics=("parallel", "parallel"), vmem_limit_bytes=VMEM_LIMIT),
        name="mla_attention",
    )(q, k, v)


def _mem_kv_kernel(mem_ref, wk_ref, wv_ref, k_ref, v_ref):
    mb = mem_ref[0].astype(BF16)
    k_ref[0, 0] = _dot(mb, wk_ref[0]).astype(BF16)
    v_ref[0, 0] = _dot(mb, wv_ref[0]).astype(BF16)


def _mem_kv(mem, w_xk, w_xv):
    bn, m_len, d = mem.shape
    n_l = w_xk.shape[0]
    wspec = pl.BlockSpec((1, d, d), lambda l, b: (l, 0, 0))
    ospec = pl.BlockSpec((1, 1, m_len, d), lambda l, b: (l, b, 0, 0))
    return pl.pallas_call(
        _mem_kv_kernel,
        grid=(n_l, bn),
        in_specs=[pl.BlockSpec((1, m_len, d), lambda l, b: (b, 0, 0)), wspec, wspec],
        out_specs=[ospec, ospec],
        out_shape=[jax.ShapeDtypeStruct((n_l, bn, m_len, d), BF16)] * 2,
        compiler_params=pltpu.CompilerParams(
            dimension_semantics=("arbitrary", "arbitrary"), vmem_limit_bytes=VMEM_LIMIT),
        name="mem_kv",
    )(mem, w_xk, w_xv)


def _first_argmax(vals, idx):
    m = functools.reduce(jnp.maximum, [jnp.max(v, axis=0, keepdims=True) for v in vals])
    big = jnp.int32(1 << 20)
    cand = [jnp.min(jnp.where(v == m, ix, big), axis=0, keepdims=True) for v, ix in zip(vals, idx)]
    first = functools.reduce(jnp.minimum, cand)
    return [ix == first for ix in idx], m


def _route(logits_t, bias):
    n = logits_t.shape[1]
    scores = jax.nn.sigmoid(logits_t)
    sel = scores + bias
    sub = lax.broadcasted_iota(jnp.int32, (GROUP_SIZE, n), 0)
    neg = -jnp.inf
    sel_g = [sel[g * GROUP_SIZE:(g + 1) * GROUP_SIZE] for g in range(N_GROUPS)]
    gs = []
    for g in range(N_GROUPS):
        (hot,), m1 = _first_argmax([sel_g[g]], [sub])
        m2 = jnp.max(jnp.where(hot, neg, sel_g[g]), axis=0, keepdims=True)
        gs.append(m1 + m2)
    gsv = jnp.concatenate(gs, axis=0)
    gmask = jnp.zeros(gsv.shape, F32)
    for _ in range(TOPK_GROUPS):
        (hot,), _m = _first_argmax([gsv], [sub])
        gmask = jnp.where(hot, 1.0, gmask)
        gsv = jnp.where(hot, neg, gsv)
    msel = [jnp.where(gmask[g:g + 1] > 0.0, sel_g[g], neg) for g in range(N_GROUPS)]
    eidx = [sub + g * GROUP_SIZE for g in range(N_GROUPS)]
    chosen = [jnp.zeros((GROUP_SIZE, n), jnp.bool_) for _ in range(N_GROUPS)]
    for _ in range(TOP_K):
        hots, _m = _first_argmax(msel, eidx)
        chosen = [jnp.logical_or(c, h) for c, h in zip(chosen, hots)]
        msel = [jnp.where(h, neg, v) for h, v in zip(hots, msel)]
    sc_g = [scores[g * GROUP_SIZE:(g + 1) * GROUP_SIZE] for g in range(N_GROUPS)]
    picked = [jnp.where(c, s, 0.0) for c, s in zip(chosen, sc_g)]
    wsum = functools.reduce(
        lambda a, b: a + b, [jnp.sum(p, axis=0, keepdims=True) for p in picked])
    return jnp.concatenate([p / wsum * ROUTE_SCALE for p in picked], axis=0)


def _post_kernel(x_ref, yag_ref, sgb_ref, o_ref, km_ref, vm_ref, wao_ref, wmo_ref, ln1_ref,
                 wxq_ref, wxo_ref, ln2_ref, wr_ref, rb_ref, wsgu_ref, wsd_ref,
                 x2b_ref, base_ref, gates_ref, *, alpha):
    d = x_ref.shape[2]
    x = x_ref[0]
    y_b = _dot(o_ref[0], wao_ref[0])
    y = yag_ref[0].astype(F32) + sgb_ref[0].astype(F32) * y_b
    mix = _dot(y.astype(BF16), wmo_ref[0])
    x1 = _layer_norm(alpha * x + mix, ln1_ref[0, 0:1], ln1_ref[0, 1:2])

    hd = d // XATTN_HEADS
    xq = (_dot(x1.astype(BF16), wxq_ref[0]) * (hd ** -0.5)).astype(BF16)
    heads = []
    for h in range(XATTN_HEADS):
        sl = slice(h * hd, (h + 1) * hd)
        s = _dot_nt(xq[:, sl], km_ref[0, 0, :, sl])
        m = jnp.max(s, axis=-1, keepdims=True)
        p = jnp.exp(s - m)
        den = jnp.sum(p, axis=-1, keepdims=True)
        heads.append((_dot(p.astype(BF16), vm_ref[0, 0, :, sl]) / den).astype(BF16))
    xat = _dot(jnp.concatenate(heads, axis=-1), wxo_ref[0])
    x2 = _layer_norm(alpha * x1 + xat, ln2_ref[0, 0:1], ln2_ref[0, 1:2])

    logits_t = lax.dot_general(wr_ref[0], x2, (((1,), (1,)), ((), ())),
                               preferred_element_type=F32, precision=lax.Precision.HIGHEST)
    gates_t = _route(logits_t, rb_ref[0])
    gates_ref[0] = gates_t.T

    x2b = x2.astype(BF16)
    x2b_ref[0] = x2b
    sh = _dot(x2b, wsgu_ref[0])
    f = sh.shape[1] // 2
    hid = _silu(sh[:, :f]) * sh[:, f:]
    base_ref[0] = alpha * x2 + _dot(hid.astype(BF16), wsd_ref[0])


def _post(l, x, yag, sgb, o, kmem, vmem, wao, wmo, ln1, wxq, wxo, ln2, wr, rb, wsgu, wsd,
          ts, alpha):
    bn, s_len, d = x.shape
    m_len = kmem.shape[2]
    tile = lambda w: pl.BlockSpec((1, ts, w), lambda b, i: (b, i, 0))
    lay = lambda a: pl.BlockSpec((1,) + a.shape[1:], lambda b, i: (l,) + (0,) * (a.ndim - 1))
    memspec = pl.BlockSpec((1, 1, m_len, d), lambda b, i: (l, b, 0, 0))
    return pl.pallas_call(
        functools.partial(_post_kernel, alpha=alpha),
        grid=(bn, s_len // ts),
        in_specs=[tile(d), tile(d), tile(d), tile(o.shape[2]), memspec, memspec,
                  lay(wao), lay(wmo), lay(ln1), lay(wxq), lay(wxo), lay(ln2), lay(wr), lay(rb),
                  lay(wsgu), lay(wsd)],
        out_specs=[tile(d), tile(d), tile(N_EXPERTS)],
        out_shape=[jax.ShapeDtypeStruct((bn, s_len, d), BF16),
                   jax.ShapeDtypeStruct((bn, s_len, d), F32),
                   jax.ShapeDtypeStruct((bn, s_len, N_EXPERTS), F32)],
        compiler_params=pltpu.CompilerParams(
            dimension_semantics=("parallel", "arbitrary"), vmem_limit_bytes=VMEM_LIMIT),
        name="post_mixer",
    )(x, yag, sgb, o, kmem, vmem, wao, wmo, ln1, wxq, wxo, ln2, wr, rb, wsgu, wsd)


def _moe_kernel(x_ref, g_ref, base_ref, wg_ref, wu_ref, wd_ref, ln_ref, out_ref, acc_ref):
    e = pl.program_id(1)

    @pl.when(e == 0)
    def _():
        acc_ref[...] = jnp.zeros_like(acc_ref)

    xb = x_ref[...]
    gt = _dot(xb, wg_ref[0, 0].astype(BF16))
    up = _dot(xb, wu_ref[0, 0].astype(BF16))
    lane = lax.broadcasted_iota(jnp.int32, g_ref.shape, 1)
    gate = jnp.sum(jnp.where(lane == e, g_ref[...], 0.0), axis=1, keepdims=True)
    hid = (_silu(gt) * up * gate).astype(BF16)
    acc_ref[...] += _dot(hid, wd_ref[0, 0].astype(BF16))

    @pl.when(e == pl.num_programs(1) - 1)
    def _():
        out_ref[...] = _layer_norm(base_ref[...] + acc_ref[...], ln_ref[0, 0:1], ln_ref[0, 1:2])


def _moe(l, x2b, gates, base, w_gate, w_up, w_down, ln3, tm):
    t, d = x2b.shape
    n_e, _, f = w_gate.shape[1:]
    tile = lambda w: pl.BlockSpec((tm, w), lambda i, e: (i, 0))
    return pl.pallas_call(
        _moe_kernel,
        grid=(t // tm, n_e),
        in_specs=[tile(d), tile(N_EXPERTS), tile(d),
                  pl.BlockSpec((1, 1, d, f), lambda i, e: (l, e, 0, 0)),
                  pl.BlockSpec((1, 1, d, f), lambda i, e: (l, e, 0, 0)),
                  pl.BlockSpec((1, 1, f, d), lambda i, e: (l, e, 0, 0)),
                  pl.BlockSpec((1,) + ln3.shape[1:], lambda i, e: (l, 0, 0))],
        out_specs=tile(d),
        out_shape=jax.ShapeDtypeStruct((t, d), F32),
        scratch_shapes=[pltpu.VMEM((tm, d), F32)],
        compiler_params=pltpu.CompilerParams(
            dimension_semantics=("parallel", "arbitrary"), vmem_limit_bytes=VMEM_LIMIT),
        name="moe_experts",
    )(x2b, gates, base, w_gate, w_up, w_down, ln3)


def _head_chunks(w, widths, n_heads, per_head):
    lead = w.shape[:-1]
    wh = w.reshape(lead + (n_heads, per_head))
    parts = [sign * wh[..., a:b] for a, b, sign in widths]
    used = sum(b - a for a, b, _ in widths)
    parts.append(jnp.zeros(lead + (n_heads, LANES - used), w.dtype))
    return jnp.concatenate(parts, axis=-1).reshape(lead + (n_heads * LANES,))


def kernel(x, mem, positions, w_in, w_conv, w_conv_out, q_norm, w_uq, kv_norm, w_uk, w_uv,
           w_attn_out, w_mix_out, ln1_g, ln1_b, w_xq, w_xk, w_xv, w_xo, ln2_g, ln2_b,
           w_router, router_bias, w_gate, w_up, w_down, ws_gate, ws_up, ws_down, ln3_g, ln3_b):
    bn, s_len, d = x.shape
    depth = w_in.shape[0]
    alpha = (2 * depth) ** 0.25
    ts = min(512, s_len)
    tq = min(512, s_len)
    tm = min(1024, bn * s_len)
    half = QK_ROPE_DIM // 2
    nope, rope = QK_NOPE_DIM, QK_ROPE_DIM

    inv_freq = ROPE_BASE ** (-jnp.arange(half, dtype=F32) / half)
    ang = positions.astype(F32)[..., None] * inv_freq
    cos, sin = jnp.cos(ang), jnp.sin(ang)
    scale = (nope + rope) ** -0.5
    zeros = lambda w: jnp.zeros((bn, s_len, w), F32)
    tabs = jnp.concatenate([
        scale * jnp.ones((bn, s_len, nope), F32), scale * cos, scale * cos, zeros(LANES - nope - rope),
        zeros(nope), scale * sin, scale * sin, zeros(LANES - nope - rope),
        cos, cos, zeros(LANES - rope),
        sin, sin, zeros(LANES - rope)], axis=-1)

    sizes = (CONV_DIM, CONV_DIM, CONV_DIM, Q_RANK, KV_RANK, rope, d, d)
    offs = [0]
    for sz in sizes:
        offs.append(offs[-1] + sz)
    seg = lambda j: w_in[:, :, offs[j]:offs[j + 1]]
    w_kr = seg(5)
    pad = jnp.zeros(w_kr.shape[:2] + (LANES - rope,), w_in.dtype)
    w_krp = jnp.concatenate([-w_kr[..., half:], w_kr[..., :half]], axis=-1)
    wx = jnp.concatenate([seg(0), seg(1), seg(2), seg(3), seg(4), w_kr, pad, w_krp, pad,
                          seg(6), seg(7)], axis=-1).astype(BF16)

    per_q = nope + rope
    wq_main = _head_chunks(w_uq, [(0, per_q, 1.0)], MLA_HEADS, per_q)
    wq_part = _head_chunks(w_uq, [(0, nope, 0.0), (nope + half, per_q, -1.0), (nope, nope + half, 1.0)],
                           MLA_HEADS, per_q)
    wq = jnp.concatenate([wq_main, wq_part], axis=-1).astype(BF16)
    wk = _head_chunks(w_uk, [(0, nope, 1.0)], MLA_HEADS, nope)
    wkv = jnp.concatenate([wk, w_uv], axis=-1).astype(BF16)
    lane_src = jnp.arange(LANES)[:, None]
    lane_dst = jnp.arange(MLA_HEADS * LANES)[None, :] % LANES
    eplace = ((lane_dst == lane_src + nope) & (lane_src < rope)).astype(BF16)

    wconv = w_conv[:, :, 0, :]
    wco = w_conv_out.astype(BF16)
    qn = q_norm[:, None, :]
    kvn = kv_norm[:, None, :]
    wao = w_attn_out.astype(BF16)
    wmo = w_mix_out.astype(BF16)
    ln1 = jnp.stack([ln1_g, ln1_b], axis=1)
    ln2 = jnp.stack([ln2_g, ln2_b], axis=1)
    ln3 = jnp.stack([ln3_g, ln3_b], axis=1)
    wxq = w_xq.astype(BF16)
    wxo = w_xo.astype(BF16)
    wr = jnp.swapaxes(w_router, 1, 2)
    rb = router_bias[:, :, None]
    wsgu = jnp.concatenate([ws_gate, ws_up], axis=-1).astype(BF16)
    wsd = ws_down.astype(BF16)

    kmem, vmem = _mem_kv(mem, w_xk.astype(BF16), w_xv.astype(BF16))

    for l in range(depth):
        yag, sgb, q, k, v = _mixer_in(l, x, tabs, wx, wconv, wco, qn, wq, kvn, wkv, eplace, ts)
        o = _attention(q, k, v, tq)
        x2b, base, gates = _post(l, x, yag, sgb, o, kmem, vmem, wao, wmo, ln1, wxq, wxo, ln2,
                                 wr, rb, wsgu, wsd, ts, alpha)
        t = bn * s_len
        x = _moe(l, x2b.reshape(t, d), gates.reshape(t, N_EXPERTS), base.reshape(t, d),
                 w_gate, w_up, w_down, ln3, tm).reshape(bn, s_len, d)
    return x
```

```python
import functools

import jax
import jax.numpy as jnp
from jax import lax
from jax.experimental import pallas as pl
from jax.experimental.pallas import tpu as pltpu
from jax.experimental.pallas import tpu_sc as plsc

CONV_DIM = 512
CONV_WIDTH = 3
MLA_HEADS = 8
QK_NOPE_DIM = 64
QK_ROPE_DIM = 32
V_HEAD_DIM = 64
Q_RANK = 384
KV_RANK = 256
ROPE_BASE = 10000.0
XATTN_HEADS = 4
N_EXPERTS = 64
TOP_K = 8
N_GROUPS = 8
TOPK_GROUPS = 4
GROUP_SIZE = N_EXPERTS // N_GROUPS
ROUTE_SCALE = 2.5
LN_EPS = 1e-5
RMS_EPS = 1e-6

LANES = 128
HALO_ROWS = 8
VMEM_LIMIT = 56 * 1024 * 1024
SC_CORES = 2
SC_SUBCORES = 16
SC_CHUNK = 64
EXPERT_TILE = 512

BF16 = jnp.bfloat16
F32 = jnp.float32
I32 = jnp.int32


def _dot(a, b):
    return jnp.dot(a, b, preferred_element_type=F32)


def _dot_nt(a, b):
    return lax.dot_general(a, b, (((1,), (1,)), ((), ())), preferred_element_type=F32)


def _layer_norm(x, g, b):
    mu = jnp.mean(x, axis=-1, keepdims=True)
    xc = x - mu
    var = jnp.mean(xc * xc, axis=-1, keepdims=True)
    return xc * lax.rsqrt(var + LN_EPS) * g + b


def _rms_norm(x, g):
    ms = jnp.mean(x * x, axis=-1, keepdims=True)
    return x * lax.rsqrt(ms + RMS_EPS) * g


def _silu(x):
    return x * jax.nn.sigmoid(x)


_HI_MASK = -65536


def _pack_rows(y):
    w = y.shape[1] // 2
    lo = lax.bitcast_convert_type(y[:, :w].astype(BF16).astype(F32), I32)
    hi = lax.bitcast_convert_type(y[:, w:].astype(BF16).astype(F32), I32)
    return lax.shift_right_logical(lo, 16) | (hi & _HI_MASK)


def _unpack_rows(p):
    lo = lax.bitcast_convert_type(lax.shift_left(p, 16), F32)
    hi = lax.bitcast_convert_type(p & _HI_MASK, F32)
    return jnp.concatenate([lo, hi], axis=-1)


_C_BCH = 0
_C_CQ = _C_BCH + 3 * CONV_DIM
_C_CKV = _C_CQ + Q_RANK
_C_KR = _C_CKV + KV_RANK
_C_GA = _C_KR + 2 * LANES
_C_GB = _C_GA + 1024


def _mixer_in_kernel(x_ref, xh_ref, tab_ref, wx_ref, wconv_ref, wco_ref, qn_ref, wq_ref,
                     kvn_ref, wkv_ref, eplace_ref,
                     yag_ref, sgb_ref, q_ref, k_ref, v_ref, *, d_model):
    i = pl.program_id(1)
    d = d_model
    ts = x_ref.shape[1]
    xb = x_ref[0].astype(BF16)

    bch = _dot(xb, wx_ref[:, _C_BCH:_C_BCH + 3 * CONV_DIM])
    b_gate = bch[:, :CONV_DIM]
    u = bch[:, CONV_DIM:2 * CONV_DIM] * bch[:, 2 * CONV_DIM:]
    xh = xh_ref[0].astype(BF16)
    hp = _dot(xh, wx_ref[:, _C_BCH + CONV_DIM:_C_BCH + 3 * CONV_DIM])
    uh = hp[:, :CONV_DIM] * hp[:, CONV_DIM:]
    uh = jnp.where(i == 0, 0.0, uh)
    row = lax.broadcasted_iota(jnp.int32, (ts, CONV_DIM), 0)
    u1 = jnp.where(row == 0, uh[HALO_ROWS - 1:HALO_ROWS], pltpu.roll(u, 1, 0))
    u2 = jnp.where(row == 0, uh[HALO_ROWS - 2:HALO_ROWS - 1],
                   jnp.where(row == 1, uh[HALO_ROWS - 1:HALO_ROWS], pltpu.roll(u, 2, 0)))
    wc = wconv_ref[0]
    z = wc[0:1] * u2 + wc[1:2] * u1 + wc[2:3] * u
    y_a = _dot((b_gate * z).astype(BF16), wco_ref[0])
    g_a = _dot(xb, wx_ref[:, _C_GA:_C_GA + d])
    yag_ref[0] = (jax.nn.sigmoid(g_a) * y_a).astype(BF16)
    g_b = _dot(xb, wx_ref[:, _C_GB:_C_GB + d])
    sgb_ref[0] = jax.nn.sigmoid(g_b).astype(BF16)

    tab = tab_ref[0]
    t_a, t_b = tab[:, 0:LANES], tab[:, LANES:2 * LANES]
    t_ck, t_sk = tab[:, 2 * LANES:3 * LANES], tab[:, 3 * LANES:4 * LANES]
    c_q = _dot(xb, wx_ref[:, _C_CQ:_C_CQ + Q_RANK])
    cqn = _rms_norm(c_q, qn_ref[0]).astype(BF16)
    hw = MLA_HEADS * LANES
    q = _dot(cqn, wq_ref[0, :, :hw])
    qp = _dot(cqn, wq_ref[0, :, hw:])
    for h in range(MLA_HEADS):
        sl = slice(h * LANES, (h + 1) * LANES)
        q_ref[0, :, sl] = (q[:, sl] * t_a + qp[:, sl] * t_b).astype(BF16)

    c_kv = _dot(xb, wx_ref[:, _C_CKV:_C_CKV + KV_RANK])
    ckvn = _rms_norm(c_kv, kvn_ref[0]).astype(BF16)
    kr2 = _dot(xb, wx_ref[:, _C_KR:_C_KR + 2 * LANES])
    kr_rot = (kr2[:, :LANES] * t_ck + kr2[:, LANES:] * t_sk).astype(BF16)
    k = _dot(ckvn, wkv_ref[0, :, :hw]) + _dot(kr_rot, eplace_ref[...])
    k_ref[0] = k.astype(BF16)
    v_ref[0] = _dot(ckvn, wkv_ref[0, :, hw:]).astype(BF16)


def _mixer_in(l, x, tabs, wx, wconv, wco, qn, wq, kvn, wkv, eplace, ts):
    bn, s_len, d = x.shape
    n_t = s_len // ts
    hw = MLA_HEADS * LANES
    vw = MLA_HEADS * V_HEAD_DIM
    tile = lambda w: pl.BlockSpec((1, ts, w), lambda b, i: (b, i, 0))
    lay = lambda a: pl.BlockSpec((1,) + a.shape[1:], lambda b, i: (l,) + (0,) * (a.ndim - 1))
    halo = pl.BlockSpec((1, HALO_ROWS, d),
                        lambda b, i: (b, jnp.maximum(i * (ts // HALO_ROWS) - 1, 0), 0))
    return pl.pallas_call(
        functools.partial(_mixer_in_kernel, d_model=d),
        grid=(bn, n_t),
        in_specs=[tile(d), halo, tile(4 * LANES),
                  pl.BlockSpec((None,) + wx.shape[1:], lambda b, i: (l, 0, 0)),
                  lay(wconv), lay(wco), lay(qn), lay(wq), lay(kvn), lay(wkv),
                  pl.BlockSpec(eplace.shape, lambda b, i: (0, 0))],
        out_specs=[tile(d), tile(d), tile(hw), tile(hw), tile(vw)],
        out_shape=[jax.ShapeDtypeStruct((bn, s_len, d), BF16),
                   jax.ShapeDtypeStruct((bn, s_len, d), BF16),
                   jax.ShapeDtypeStruct((bn, s_len, hw), BF16),
                   jax.ShapeDtypeStruct((bn, s_len, hw), BF16),
                   jax.ShapeDtypeStruct((bn, s_len, vw), BF16)],
        compiler_params=pltpu.CompilerParams(
            dimension_semantics=("parallel", "arbitrary"), vmem_limit_bytes=VMEM_LIMIT),
        name="mixer_in",
    )(x, x, tabs, wx, wconv, wco, qn, wq, kvn, wkv, eplace)


def _attn_kernel(q_ref, k_ref, v_ref, o_ref, *, tq):
    s_len = q_ref.shape[1]
    n_q = s_len // tq
    lane = lax.broadcasted_iota(jnp.int32, (tq, 2 * V_HEAD_DIM), 1)
    for qi in range(n_q):
        kv_len = (qi + 1) * tq
        rows = slice(qi * tq, (qi + 1) * tq)
        v2 = v_ref[0, :kv_len, :]
        r = lax.broadcasted_iota(jnp.int32, (tq, kv_len), 0) + qi * tq
        c = lax.broadcasted_iota(jnp.int32, (tq, kv_len), 1)
        outs = []
        for h in range(2):
            sl = slice(h * LANES, (h + 1) * LANES)
            s = _dot_nt(q_ref[0, rows, sl], k_ref[0, :kv_len, sl])
            s = jnp.where(c <= r, s, -jnp.inf)
            m = jnp.max(s, axis=-1, keepdims=True)
            p = jnp.exp(s - m)
            den = jnp.sum(p, axis=-1, keepdims=True)
            outs.append(_dot(p.astype(BF16), v2) / den)
        o_ref[0, rows, :] = jnp.where(lane < V_HEAD_DIM, outs[0], outs[1]).astype(BF16)


def _attention(q, k, v, tq):
    bn, s_len, _ = q.shape
    n_pairs = MLA_HEADS // 2
    return pl.pallas_call(
        functools.partial(_attn_kernel, tq=tq),
        grid=(bn, n_pairs),
        in_specs=[pl.BlockSpec((1, s_len, 2 * LANES), lambda b, h: (b, 0, h)),
                  pl.BlockSpec((1, s_len, 2 * LANES), lambda b, h: (b, 0, h)),
                  pl.BlockSpec((1, s_len, 2 * V_HEAD_DIM), lambda b, h: (b, 0, h))],
        out_specs=pl.BlockSpec((1, s_len, 2 * V_HEAD_DIM), lambda b, h: (b, 0, h)),
        out_shape=jax.ShapeDtypeStruct((bn, s_len, MLA_HEADS * V_HEAD_DIM), BF16),
        compiler_params=pltpu.CompilerParams(
            dimension_semantics=("parallel", "parallel"), vmem_limit_bytes=VMEM_LIMIT),
        name="mla_attention",
    )(q, k, v)


def _mem_kv_kernel(mem_ref, wk_ref, wv_ref, k_ref, v_ref):
    mb = mem_ref[0].astype(BF16)
    k_ref[0, 0] = _dot(mb, wk_ref[0]).astype(BF16)
    v_ref[0, 0] = _dot(mb, wv_ref[0]).astype(BF16)


def _mem_kv(mem, w_xk, w_xv):
    bn, m_len, d = mem.shape
    n_l = w_xk.shape[0]
    wspec = pl.BlockSpec((1, d, d), lambda l, b: (l, 0, 0))
    ospec = pl.BlockSpec((1, 1, m_len, d), lambda l, b: (l, b, 0, 0))
    return pl.pallas_call(
        _mem_kv_kernel,
        grid=(n_l, bn),
        in_specs=[pl.BlockSpec((1, m_len, d), lambda l, b: (b, 0, 0)), wspec, wspec],
        out_specs=[ospec, ospec],
        out_shape=[jax.ShapeDtypeStruct((n_l, bn, m_len, d), BF16)] * 2,
        compiler_params=pltpu.CompilerParams(
            dimension_semantics=("arbitrary", "arbitrary"), vmem_limit_bytes=VMEM_LIMIT),
        name="mem_kv",
    )(mem, w_xk, w_xv)


def _first_argmax(vals, idx):
    m = functools.reduce(jnp.maximum, [jnp.max(v, axis=0, keepdims=True) for v in vals])
    big = jnp.int32(1 << 20)
    cand = [jnp.min(jnp.where(v == m, ix, big), axis=0, keepdims=True) for v, ix in zip(vals, idx)]
    first = functools.reduce(jnp.minimum, cand)
    return [ix == first for ix in idx], m, first


def _route(logits_t, bias):
    n = logits_t.shape[1]
    scores = jax.nn.sigmoid(logits_t)
    sel = scores + bias
    sub = lax.broadcasted_iota(jnp.int32, (GROUP_SIZE, n), 0)
    neg = -jnp.inf
    sel_g = [sel[g * GROUP_SIZE:(g + 1) * GROUP_SIZE] for g in range(N_GROUPS)]
    gs = []
    for g in range(N_GROUPS):
        (hot,), m1, _ = _first_argmax([sel_g[g]], [sub])
        m2 = jnp.max(jnp.where(hot, neg, sel_g[g]), axis=0, keepdims=True)
        gs.append(m1 + m2)
    gsv = jnp.concatenate(gs, axis=0)
    gmask = jnp.zeros(gsv.shape, F32)
    for _ in range(TOPK_GROUPS):
        (hot,), _m, _ = _first_argmax([gsv], [sub])
        gmask = jnp.where(hot, 1.0, gmask)
        gsv = jnp.where(hot, neg, gsv)
    msel = [jnp.where(gmask[g:g + 1] > 0.0, sel_g[g], neg) for g in range(N_GROUPS)]
    eidx = [sub + g * GROUP_SIZE for g in range(N_GROUPS)]
    picks, ids = [], []
    for _ in range(TOP_K):
        hots, _m, first = _first_argmax(msel, eidx)
        picks.append(hots)
        ids.append(first)
        msel = [jnp.where(h, neg, v) for h, v in zip(hots, msel)]
    chosen = [functools.reduce(jnp.logical_or, [p[g] for p in picks]) for g in range(N_GROUPS)]
    sc_g = [scores[g * GROUP_SIZE:(g + 1) * GROUP_SIZE] for g in range(N_GROUPS)]
    picked = [jnp.where(c, s, 0.0) for c, s in zip(chosen, sc_g)]
    wsum = functools.reduce(
        lambda a, b: a + b, [jnp.sum(p, axis=0, keepdims=True) for p in picked])
    gates = [p / wsum * ROUTE_SCALE for p in picked]
    return picks, ids, chosen, gates


def _pick_rows(hots, vals):
    return functools.reduce(
        lambda a, b: a + b,
        [jnp.sum(jnp.where(h, v, 0.0), axis=0, keepdims=True) for h, v in zip(hots, vals)])


def _post_kernel(x_ref, yag_ref, sgb_ref, o_ref, km_ref, vm_ref, wao_ref, wmo_ref, ln1_ref,
                 wxq_ref, wxo_ref, ln2_ref, wr_ref, rb_ref, wsgu_ref, wsd_ref,
                 x2p_ref, base_ref, eid_ref, rank_ref, gate_ref, cnt_ref, *, alpha):
    d = x_ref.shape[2]
    ts = x_ref.shape[1]
    first_step = jnp.logical_and(pl.program_id(0) == 0, pl.program_id(1) == 0)

    @pl.when(first_step)
    def _():
        cnt_ref[...] = jnp.zeros_like(cnt_ref)

    x = x_ref[0]
    y_b = _dot(o_ref[0], wao_ref[0])
    y = yag_ref[0].astype(F32) + sgb_ref[0].astype(F32) * y_b
    mix = _dot(y.astype(BF16), wmo_ref[0])
    x1 = _layer_norm(alpha * x + mix, ln1_ref[0, 0:1], ln1_ref[0, 1:2])

    hd = d // XATTN_HEADS
    xq = (_dot(x1.astype(BF16), wxq_ref[0]) * (hd ** -0.5)).astype(BF16)
    heads = []
    for h in range(XATTN_HEADS):
        sl = slice(h * hd, (h + 1) * hd)
        s = _dot_nt(xq[:, sl], km_ref[0, 0, :, sl])
        m = jnp.max(s, axis=-1, keepdims=True)
        p = jnp.exp(s - m)
        den = jnp.sum(p, axis=-1, keepdims=True)
        heads.append((_dot(p.astype(BF16), vm_ref[0, 0, :, sl]) / den).astype(BF16))
    xat = _dot(jnp.concatenate(heads, axis=-1), wxo_ref[0])
    x2 = _layer_norm(alpha * x1 + xat, ln2_ref[0, 0:1], ln2_ref[0, 1:2])

    logits_t = lax.dot_general(wr_ref[0], x2, (((1,), (1,)), ((), ())),
                               preferred_element_type=F32, precision=lax.Precision.HIGHEST)
    picks, ids, chosen, gates = _route(logits_t, rb_ref[0])

    sel_t = jnp.concatenate([c.astype(F32) for c in chosen], axis=0)
    before = (lax.broadcasted_iota(jnp.int32, (ts, ts), 0)
              < lax.broadcasted_iota(jnp.int32, (ts, ts), 1)).astype(BF16)
    running = cnt_ref[:, 0:1]
    rank_all = _dot(sel_t.astype(BF16), before) + running
    cnt_ref[...] = cnt_ref[...] + jnp.sum(sel_t, axis=1, keepdims=True)
    rank_g = [rank_all[g * GROUP_SIZE:(g + 1) * GROUP_SIZE] for g in range(N_GROUPS)]
    eid_ref[0] = jnp.concatenate(ids, axis=0)
    rank_ref[0] = jnp.concatenate([_pick_rows(h, rank_g) for h in picks], axis=0).astype(I32)
    gate_ref[0] = jnp.concatenate([_pick_rows(h, gates) for h in picks], axis=0)

    x2b = x2.astype(BF16)
    x2p_ref[0] = _pack_rows(x2)
    sh = _dot(x2b, wsgu_ref[0])
    f = sh.shape[1] // 2
    hid = _silu(sh[:, :f]) * sh[:, f:]
    base_ref[0] = alpha * x2 + _dot(hid.astype(BF16), wsd_ref[0])


def _post(l, x, yag, sgb, o, kmem, vmem, wao, wmo, ln1, wxq, wxo, ln2, wr, rb, wsgu, wsd,
          ts, alpha):
    bn, s_len, d = x.shape
    m_len = kmem.shape[2]
    tile = lambda w: pl.BlockSpec((1, ts, w), lambda b, i: (b, i, 0))
    pick = pl.BlockSpec((1, TOP_K, ts), lambda b, i: (b, 0, i))
    lay = lambda a: pl.BlockSpec((1,) + a.shape[1:], lambda b, i: (l,) + (0,) * (a.ndim - 1))
    memspec = pl.BlockSpec((1, 1, m_len, d), lambda b, i: (l, b, 0, 0))
    return pl.pallas_call(
        functools.partial(_post_kernel, alpha=alpha),
        grid=(bn, s_len // ts),
        in_specs=[tile(d), tile(d), tile(d), tile(o.shape[2]), memspec, memspec,
                  lay(wao), lay(wmo), lay(ln1), lay(wxq), lay(wxo), lay(ln2), lay(wr), lay(rb),
                  lay(wsgu), lay(wsd)],
        out_specs=[tile(d // 2), tile(d), pick, pick, pick,
                   pl.BlockSpec((N_EXPERTS, LANES), lambda b, i: (0, 0))],
        out_shape=[jax.ShapeDtypeStruct((bn, s_len, d // 2), I32),
                   jax.ShapeDtypeStruct((bn, s_len, d), F32),
                   jax.ShapeDtypeStruct((bn, TOP_K, s_len), I32),
                   jax.ShapeDtypeStruct((bn, TOP_K, s_len), I32),
                   jax.ShapeDtypeStruct((bn, TOP_K, s_len), F32),
                   jax.ShapeDtypeStruct((N_EXPERTS, LANES), F32)],
        compiler_params=pltpu.CompilerParams(
            dimension_semantics=("arbitrary", "arbitrary"), vmem_limit_bytes=VMEM_LIMIT),
        name="post_mixer",
    )(x, yag, sgb, o, kmem, vmem, wao, wmo, ln1, wxq, wxo, ln2, wr, rb, wsgu, wsd)


def _slot_kernel(offs_ref, eid_ref, rank_ref, slot_ref):
    eid = eid_ref[0]
    acc = rank_ref[0]
    for e in range(N_EXPERTS):
        acc = acc + jnp.where(eid == e, offs_ref[e], 0)
    slot_ref[0] = acc


def _slots(offs, eid, rank):
    bn, k, s_len = eid.shape
    spec = pl.BlockSpec((1, k, s_len), lambda b, offs_ref: (b, 0, 0))
    return pl.pallas_call(
        _slot_kernel,
        grid_spec=pltpu.PrefetchScalarGridSpec(
            num_scalar_prefetch=1, grid=(bn,), in_specs=[spec, spec], out_specs=spec),
        out_shape=jax.ShapeDtypeStruct((bn, k, s_len), I32),
        name="moe_slots",
    )(offs, eid, rank)


def _sc_mesh():
    return plsc.VectorSubcoreMesh(core_axis_name="c", subcore_axis_name="s")


def _sc_dispatch(xp, slot_chunks, n_slots):
    t, w = xp.shape
    n_chunks, k, ch = slot_chunks.shape
    per_worker = n_chunks // (SC_CORES * SC_SUBCORES)

    @functools.partial(
        pl.kernel, mesh=_sc_mesh(),
        out_type=jax.ShapeDtypeStruct((n_slots, w), I32),
        scratch_types=[pltpu.VMEM((k, ch), I32), pltpu.VMEM((ch, w), I32), pltpu.SemaphoreType.DMA],
        name="moe_dispatch")
    def body(x_hbm, slot_hbm, out_hbm, idx_v, rows_v, sem):
        worker = lax.axis_index("s") * SC_CORES + lax.axis_index("c")

        @pl.loop(0, per_worker)
        def _(j):
            c = worker * per_worker + j
            pltpu.sync_copy(slot_hbm.at[c], idx_v)
            pltpu.sync_copy(x_hbm.at[pl.ds(c * ch, ch)], rows_v)
            copies = [pltpu.async_copy(rows_v, out_hbm.at[idx_v.at[kk]], sem) for kk in range(k)]
            for cp in copies:
                cp.wait()

    return body(xp, slot_chunks)


def _sc_combine(ys, slot_chunks, t):
    w = ys.shape[1]
    n_chunks, k, ch = slot_chunks.shape
    per_worker = n_chunks // (SC_CORES * SC_SUBCORES)

    @functools.partial(
        pl.kernel, mesh=_sc_mesh(),
        out_type=jax.ShapeDtypeStruct((k, t, w), I32),
        scratch_types=[pltpu.VMEM((k, ch), I32), pltpu.VMEM((ch, w), I32), pltpu.VMEM((ch, w), I32),
                       pltpu.SemaphoreType.DMA, pltpu.SemaphoreType.DMA,
                       pltpu.SemaphoreType.DMA, pltpu.SemaphoreType.DMA],
        name="moe_combine")
    def body(y_hbm, slot_hbm, out_hbm, idx_v, rows0, rows1, g0, g1, w0, w1):
        worker = lax.axis_index("s") * SC_CORES + lax.axis_index("c")
        bufs, gsem, wsem = (rows0, rows1), (g0, g1), (w0, w1)

        @pl.loop(0, per_worker)
        def _(j):
            c = worker * per_worker + j
            pltpu.sync_copy(slot_hbm.at[c], idx_v)
            gathers = [None] * k
            writes = [None] * k
            gathers[0] = pltpu.async_copy(y_hbm.at[idx_v.at[0]], bufs[0], gsem[0])
            for kk in range(k):
                cur = kk % 2
                if kk + 1 < k:
                    if kk >= 1:
                        writes[kk - 1].wait()
                    gathers[kk + 1] = pltpu.async_copy(
                        y_hbm.at[idx_v.at[kk + 1]], bufs[1 - cur], gsem[1 - cur])
                gathers[kk].wait()
                writes[kk] = pltpu.async_copy(
                    bufs[cur], out_hbm.at[kk, pl.ds(c * ch, ch)], wsem[cur])
            writes[k - 2].wait()
            writes[k - 1].wait()

    return body(ys, slot_chunks)


def _experts_kernel(ti_ref, te_ref, nu_ref, xs_ref, wg_ref, wu_ref, wd_ref, ys_ref,
                    wg_b, wu_b, wd_b):
    j = pl.program_id(0)
    prev = te_ref[jnp.maximum(j - 1, 0)]
    new_expert = jnp.logical_or(j == 0, te_ref[j] != prev)

    @pl.when(new_expert)
    def _():
        wg_b[...] = wg_ref[0, 0].astype(BF16)
        wu_b[...] = wu_ref[0, 0].astype(BF16)
        wd_b[...] = wd_ref[0, 0].astype(BF16)

    @pl.when(j < nu_ref[0])
    def _():
        xb = _unpack_rows(xs_ref[...]).astype(BF16)
        hid = _silu(_dot(xb, wg_b[...])) * _dot(xb, wu_b[...])
        ys_ref[...] = _pack_rows(_dot(hid.astype(BF16), wd_b[...]))


def _experts(l, xs, ti, te, nu, w_gate, w_up, w_down):
    n_slots, w = xs.shape
    n_tiles = n_slots // EXPERT_TILE
    d, f = w_gate.shape[2:]
    rows = pl.BlockSpec((EXPERT_TILE, w), lambda j, ti, te, nu: (ti[j], 0))
    wspec = lambda a, b: pl.BlockSpec((1, 1, a, b), lambda j, ti, te, nu: (l, te[j], 0, 0))
    return pl.pallas_call(
        _experts_kernel,
        grid_spec=pltpu.PrefetchScalarGridSpec(
            num_scalar_prefetch=3, grid=(n_tiles,),
            in_specs=[rows, wspec(d, f), wspec(d, f), wspec(f, d)],
            out_specs=rows,
            scratch_shapes=[pltpu.VMEM((d, f), BF16), pltpu.VMEM((d, f), BF16),
                            pltpu.VMEM((f, d), BF16)]),
        out_shape=jax.ShapeDtypeStruct((n_slots, w), I32),
        compiler_params=pltpu.CompilerParams(
            dimension_semantics=("arbitrary",), vmem_limit_bytes=VMEM_LIMIT),
        name="moe_experts",
    )(ti, te, nu, xs, w_gate, w_up, w_down)


def _final_kernel(yg_ref, gate_ref, base_ref, ln_ref, out_ref):
    gates = gate_ref[0].T
    acc = base_ref[0]
    for k in range(yg_ref.shape[0]):
        acc = acc + gates[:, k:k + 1] * _unpack_rows(yg_ref[k, 0])
    out_ref[0] = _layer_norm(acc, ln_ref[0, 0:1], ln_ref[0, 1:2])


def _final(l, yg, gate8, base, ln3, tm):
    bn, s_len, d = base.shape
    k = yg.shape[0]
    return pl.pallas_call(
        _final_kernel,
        grid=(bn, s_len // tm),
        in_specs=[pl.BlockSpec((k, 1, tm, d // 2), lambda b, i: (0, b, i, 0)),
                  pl.BlockSpec((1, k, tm), lambda b, i: (b, 0, i)),
                  pl.BlockSpec((1, tm, d), lambda b, i: (b, i, 0)),
                  pl.BlockSpec((1,) + ln3.shape[1:], lambda b, i: (l, 0, 0))],
        out_specs=pl.BlockSpec((1, tm, d), lambda b, i: (b, i, 0)),
        out_shape=jax.ShapeDtypeStruct((bn, s_len, d), F32),
        compiler_params=pltpu.CompilerParams(
            dimension_semantics=("parallel", "parallel"), vmem_limit_bytes=VMEM_LIMIT),
        name="moe_final",
    )(yg, gate8, base, ln3)


def _moe(l, x2p, base, eid, rank, gate8, counts, w_gate, w_up, w_down, ln3):
    bn, s_len, d = base.shape
    t = bn * s_len
    n_tiles = (t * TOP_K) // EXPERT_TILE + N_EXPERTS
    n_slots = n_tiles * EXPERT_TILE

    cnt = counts[:, 0].astype(I32)
    padded = (cnt + EXPERT_TILE - 1) // EXPERT_TILE * EXPERT_TILE
    ends = jnp.cumsum(padded)
    offs = ends - padded
    n_used = ends[-1] // EXPERT_TILE
    ti = jnp.minimum(jnp.arange(n_tiles, dtype=I32), n_used - 1)
    te = jnp.searchsorted(ends, ti * EXPERT_TILE, side="right").astype(I32)

    slots = _slots(offs, eid, rank)
    slot_chunks = slots.reshape(bn, TOP_K, s_len // SC_CHUNK, SC_CHUNK).transpose(0, 2, 1, 3)
    slot_chunks = slot_chunks.reshape(t // SC_CHUNK, TOP_K, SC_CHUNK)

    xs = _sc_dispatch(x2p.reshape(t, d // 2), slot_chunks, n_slots)
    ys = _experts(l, xs, ti, te, n_used.reshape(1), w_gate, w_up, w_down)
    yg = _sc_combine(ys, slot_chunks, t)
    return _final(l, yg.reshape(TOP_K, bn, s_len, d // 2), gate8, base, ln3, min(512, s_len))


def _head_chunks(w, widths, n_heads, per_head):
    lead = w.shape[:-1]
    wh = w.reshape(lead + (n_heads, per_head))
    parts = [sign * wh[..., a:b] for a, b, sign in widths]
    used = sum(b - a for a, b, _ in widths)
    parts.append(jnp.zeros(lead + (n_heads, LANES - used), w.dtype))
    return jnp.concatenate(parts, axis=-1).reshape(lead + (n_heads * LANES,))


def kernel(x, mem, positions, w_in, w_conv, w_conv_out, q_norm, w_uq, kv_norm, w_uk, w_uv,
           w_attn_out, w_mix_out, ln1_g, ln1_b, w_xq, w_xk, w_xv, w_xo, ln2_g, ln2_b,
           w_router, router_bias, w_gate, w_up, w_down, ws_gate, ws_up, ws_down, ln3_g, ln3_b):
    bn, s_len, d = x.shape
    depth = w_in.shape[0]
    alpha = (2 * depth) ** 0.25
    ts = min(512, s_len)
    tq = min(512, s_len)
    half = QK_ROPE_DIM // 2
    nope, rope = QK_NOPE_DIM, QK_ROPE_DIM

    inv_freq = ROPE_BASE ** (-jnp.arange(half, dtype=F32) / half)
    ang = positions.astype(F32)[..., None] * inv_freq
    cos, sin = jnp.cos(ang), jnp.sin(ang)
    scale = (nope + rope) ** -0.5
    zeros = lambda w: jnp.zeros((bn, s_len, w), F32)
    tabs = jnp.concatenate([
        scale * jnp.ones((bn, s_len, nope), F32), scale * cos, scale * cos, zeros(LANES - nope - rope),
        zeros(nope), scale * sin, scale * sin, zeros(LANES - nope - rope),
        cos, cos, zeros(LANES - rope),
        sin, sin, zeros(LANES - rope)], axis=-1)

    sizes = (CONV_DIM, CONV_DIM, CONV_DIM, Q_RANK, KV_RANK, rope, d, d)
    offs = [0]
    for sz in sizes:
        offs.append(offs[-1] + sz)
    seg = lambda j: w_in[:, :, offs[j]:offs[j + 1]]
    w_kr = seg(5)
    pad = jnp.zeros(w_kr.shape[:2] + (LANES - rope,), w_in.dtype)
    w_krp = jnp.concatenate([-w_kr[..., half:], w_kr[..., :half]], axis=-1)
    wx = jnp.concatenate([seg(0), seg(1), seg(2), seg(3), seg(4), w_kr, pad, w_krp, pad,
                          seg(6), seg(7)], axis=-1).astype(BF16)

    per_q = nope + rope
    wq_main = _head_chunks(w_uq, [(0, per_q, 1.0)], MLA_HEADS, per_q)
    wq_part = _head_chunks(w_uq, [(0, nope, 0.0), (nope + half, per_q, -1.0), (nope, nope + half, 1.0)],
                           MLA_HEADS, per_q)
    wq = jnp.concatenate([wq_main, wq_part], axis=-1).astype(BF16)
    wk = _head_chunks(w_uk, [(0, nope, 1.0)], MLA_HEADS, nope)
    wkv = jnp.concatenate([wk, w_uv], axis=-1).astype(BF16)
    lane_src = jnp.arange(LANES)[:, None]
    lane_dst = jnp.arange(MLA_HEADS * LANES)[None, :] % LANES
    eplace = ((lane_dst == lane_src + nope) & (lane_src < rope)).astype(BF16)

    wconv = w_conv[:, :, 0, :]
    wco = w_conv_out.astype(BF16)
    qn = q_norm[:, None, :]
    kvn = kv_norm[:, None, :]
    wao = w_attn_out.astype(BF16)
    wmo = w_mix_out.astype(BF16)
    ln1 = jnp.stack([ln1_g, ln1_b], axis=1)
    ln2 = jnp.stack([ln2_g, ln2_b], axis=1)
    ln3 = jnp.stack([ln3_g, ln3_b], axis=1)
    wxq = w_xq.astype(BF16)
    wxo = w_xo.astype(BF16)
    wr = jnp.swapaxes(w_router, 1, 2)
    rb = router_bias[:, :, None]
    wsgu = jnp.concatenate([ws_gate, ws_up], axis=-1).astype(BF16)
    wsd = ws_down.astype(BF16)

    kmem, vmem = _mem_kv(mem, w_xk.astype(BF16), w_xv.astype(BF16))

    for l in range(depth):
        yag, sgb, q, k, v = _mixer_in(l, x, tabs, wx, wconv, wco, qn, wq, kvn, wkv, eplace, ts)
        o = _attention(q, k, v, tq)
        x2p, base, eid, rank, gate8, counts = _post(
            l, x, yag, sgb, o, kmem, vmem, wao, wmo, ln1, wxq, wxo, ln2, wr, rb, wsgu, wsd, ts, alpha)
        x = _moe(l, x2p, base, eid, rank, gate8, counts, w_gate, w_up, w_down, ln3)
    return x
```

```python
import functools

import jax
import jax.numpy as jnp
from jax import lax
from jax.experimental import pallas as pl
from jax.experimental.pallas import tpu as pltpu
from jax.experimental.pallas import tpu_sc as plsc

CONV_DIM = 512
CONV_WIDTH = 3
MLA_HEADS = 8
QK_NOPE_DIM = 64
QK_ROPE_DIM = 32
V_HEAD_DIM = 64
Q_RANK = 384
KV_RANK = 256
ROPE_BASE = 10000.0
XATTN_HEADS = 4
N_EXPERTS = 64
TOP_K = 8
N_GROUPS = 8
TOPK_GROUPS = 4
GROUP_SIZE = N_EXPERTS // N_GROUPS
ROUTE_SCALE = 2.5
LN_EPS = 1e-5
RMS_EPS = 1e-6

LANES = 128
HALO_ROWS = 8
VMEM_LIMIT = 56 * 1024 * 1024
SC_CORES = 2
SC_SUBCORES = 16
SC_CHUNK = 64
EXPERT_TILE = 512

BF16 = jnp.bfloat16
F32 = jnp.float32
I32 = jnp.int32


def _dot(a, b):
    return jnp.dot(a, b, preferred_element_type=F32)


def _dot_nt(a, b):
    return lax.dot_general(a, b, (((1,), (1,)), ((), ())), preferred_element_type=F32)


def _layer_norm(x, g, b):
    mu = jnp.mean(x, axis=-1, keepdims=True)
    xc = x - mu
    var = jnp.mean(xc * xc, axis=-1, keepdims=True)
    return xc * lax.rsqrt(var + LN_EPS) * g + b


def _rms_norm(x, g):
    ms = jnp.mean(x * x, axis=-1, keepdims=True)
    return x * lax.rsqrt(ms + RMS_EPS) * g


def _silu(x):
    return x * jax.nn.sigmoid(x)


_HI_MASK = -65536


def _pack_rows(y):
    w = y.shape[1] // 2
    lo = lax.bitcast_convert_type(y[:, :w].astype(BF16).astype(F32), I32)
    hi = lax.bitcast_convert_type(y[:, w:].astype(BF16).astype(F32), I32)
    return lax.shift_right_logical(lo, 16) | (hi & _HI_MASK)


def _unpack_rows(p):
    lo = lax.bitcast_convert_type(lax.shift_left(p, 16), F32)
    hi = lax.bitcast_convert_type(p & _HI_MASK, F32)
    return jnp.concatenate([lo, hi], axis=-1)


_C_BCH = 0
_C_CQ = _C_BCH + 3 * CONV_DIM
_C_CKV = _C_CQ + Q_RANK
_C_KR = _C_CKV + KV_RANK
_C_GA = _C_KR + 2 * LANES
_C_GB = _C_GA + 1024


def _mixer_in_kernel(x_ref, xh_ref, tab_ref, wx_ref, wconv_ref, wco_ref, qn_ref, wq_ref,
                     kvn_ref, wkv_ref, eplace_ref,
                     yag_ref, sgb_ref, q_ref, k_ref, v_ref, *, d_model):
    i = pl.program_id(1)
    d = d_model
    ts = x_ref.shape[1]
    xb = x_ref[0].astype(BF16)

    bch = _dot(xb, wx_ref[:, _C_BCH:_C_BCH + 3 * CONV_DIM])
    b_gate = bch[:, :CONV_DIM]
    u = bch[:, CONV_DIM:2 * CONV_DIM] * bch[:, 2 * CONV_DIM:]
    xh = xh_ref[0].astype(BF16)
    hp = _dot(xh, wx_ref[:, _C_BCH + CONV_DIM:_C_BCH + 3 * CONV_DIM])
    uh = hp[:, :CONV_DIM] * hp[:, CONV_DIM:]
    uh = jnp.where(i == 0, 0.0, uh)
    row = lax.broadcasted_iota(jnp.int32, (ts, CONV_DIM), 0)
    u1 = jnp.where(row == 0, uh[HALO_ROWS - 1:HALO_ROWS], pltpu.roll(u, 1, 0))
    u2 = jnp.where(row == 0, uh[HALO_ROWS - 2:HALO_ROWS - 1],
                   jnp.where(row == 1, uh[HALO_ROWS - 1:HALO_ROWS], pltpu.roll(u, 2, 0)))
    wc = wconv_ref[0]
    z = wc[0:1] * u2 + wc[1:2] * u1 + wc[2:3] * u
    y_a = _dot((b_gate * z).astype(BF16), wco_ref[0])
    g_a = _dot(xb, wx_ref[:, _C_GA:_C_GA + d])
    yag_ref[0] = (jax.nn.sigmoid(g_a) * y_a).astype(BF16)
    g_b = _dot(xb, wx_ref[:, _C_GB:_C_GB + d])
    sgb_ref[0] = jax.nn.sigmoid(g_b).astype(BF16)

    tab = tab_ref[0]
    t_a, t_b = tab[:, 0:LANES], tab[:, LANES:2 * LANES]
    t_ck, t_sk = tab[:, 2 * LANES:3 * LANES], tab[:, 3 * LANES:4 * LANES]
    c_q = _dot(xb, wx_ref[:, _C_CQ:_C_CQ + Q_RANK])
    cqn = _rms_norm(c_q, qn_ref[0]).astype(BF16)
    hw = MLA_HEADS * LANES
    q = _dot(cqn, wq_ref[0, :, :hw])
    qp = _dot(cqn, wq_ref[0, :, hw:])
    for h in range(MLA_HEADS):
        sl = slice(h * LANES, (h + 1) * LANES)
        q_ref[0, :, sl] = (q[:, sl] * t_a + qp[:, sl] * t_b).astype(BF16)

    c_kv = _dot(xb, wx_ref[:, _C_CKV:_C_CKV + KV_RANK])
    ckvn = _rms_norm(c_kv, kvn_ref[0]).astype(BF16)
    kr2 = _dot(xb, wx_ref[:, _C_KR:_C_KR + 2 * LANES])
    kr_rot = (kr2[:, :LANES] * t_ck + kr2[:, LANES:] * t_sk).astype(BF16)
    k = _dot(ckvn, wkv_ref[0, :, :hw]) + _dot(kr_rot, eplace_ref[...])
    k_ref[0] = k.astype(BF16)
    v_ref[0] = _dot(ckvn, wkv_ref[0, :, hw:]).astype(BF16)


def _mixer_in(l, x, tabs, wx, wconv, wco, qn, wq, kvn, wkv, eplace, ts):
    bn, s_len, d = x.shape
    n_t = s_len // ts
    hw = MLA_HEADS * LANES
    vw = MLA_HEADS * V_HEAD_DIM
    tile = lambda w: pl.BlockSpec((1, ts, w), lambda b, i: (b, i, 0))
    lay = lambda a: pl.BlockSpec((1,) + a.shape[1:], lambda b, i: (l,) + (0,) * (a.ndim - 1))
    halo = pl.BlockSpec((1, HALO_ROWS, d),
                        lambda b, i: (b, jnp.maximum(i * (ts // HALO_ROWS) - 1, 0), 0))
    return pl.pallas_call(
        functools.partial(_mixer_in_kernel, d_model=d),
        grid=(bn, n_t),
        in_specs=[tile(d), halo, tile(4 * LANES),
                  pl.BlockSpec((None,) + wx.shape[1:], lambda b, i: (l, 0, 0)),
                  lay(wconv), lay(wco), lay(qn), lay(wq), lay(kvn), lay(wkv),
                  pl.BlockSpec(eplace.shape, lambda b, i: (0, 0))],
        out_specs=[tile(d), tile(d), tile(hw), tile(hw), tile(vw)],
        out_shape=[jax.ShapeDtypeStruct((bn, s_len, d), BF16),
                   jax.ShapeDtypeStruct((bn, s_len, d), BF16),
                   jax.ShapeDtypeStruct((bn, s_len, hw), BF16),
                   jax.ShapeDtypeStruct((bn, s_len, hw), BF16),
                   jax.ShapeDtypeStruct((bn, s_len, vw), BF16)],
        compiler_params=pltpu.CompilerParams(
            dimension_semantics=("parallel", "arbitrary"), vmem_limit_bytes=VMEM_LIMIT),
        name="mixer_in",
    )(x, x, tabs, wx, wconv, wco, qn, wq, kvn, wkv, eplace)


def _attn_kernel(q_ref, k_ref, v_ref, o_ref, *, tq):
    s_len = q_ref.shape[1]
    n_q = s_len // tq
    lane = lax.broadcasted_iota(jnp.int32, (tq, 2 * V_HEAD_DIM), 1)
    for qi in range(n_q):
        kv_len = (qi + 1) * tq
        rows = slice(qi * tq, (qi + 1) * tq)
        v2 = v_ref[0, :kv_len, :]
        r = lax.broadcasted_iota(jnp.int32, (tq, kv_len), 0) + qi * tq
        c = lax.broadcasted_iota(jnp.int32, (tq, kv_len), 1)
        outs = []
        for h in range(2):
            sl = slice(h * LANES, (h + 1) * LANES)
            s = _dot_nt(q_ref[0, rows, sl], k_ref[0, :kv_len, sl])
            s = jnp.where(c <= r, s, -jnp.inf)
            m = jnp.max(s, axis=-1, keepdims=True)
            p = jnp.exp(s - m)
            den = jnp.sum(p, axis=-1, keepdims=True)
            outs.append(_dot(p.astype(BF16), v2) / den)
        o_ref[0, rows, :] = jnp.where(lane < V_HEAD_DIM, outs[0], outs[1]).astype(BF16)


def _attention(q, k, v, tq):
    bn, s_len, _ = q.shape
    n_pairs = MLA_HEADS // 2
    return pl.pallas_call(
        functools.partial(_attn_kernel, tq=tq),
        grid=(bn, n_pairs),
        in_specs=[pl.BlockSpec((1, s_len, 2 * LANES), lambda b, h: (b, 0, h)),
                  pl.BlockSpec((1, s_len, 2 * LANES), lambda b, h: (b, 0, h)),
                  pl.BlockSpec((1, s_len, 2 * V_HEAD_DIM), lambda b, h: (b, 0, h))],
        out_specs=pl.BlockSpec((1, s_len, 2 * V_HEAD_DIM), lambda b, h: (b, 0, h)),
        out_shape=jax.ShapeDtypeStruct((bn, s_len, MLA_HEADS * V_HEAD_DIM), BF16),
        compiler_params=pltpu.CompilerParams(
            dimension_semantics=("parallel", "parallel"), vmem_limit_bytes=VMEM_LIMIT),
        name="mla_attention",
    )(q, k, v)


def _mem_kv_kernel(mem_ref, wk_ref, wv_ref, k_ref, v_ref):
    mb = mem_ref[0].astype(BF16)
    k_ref[0, 0] = _dot(mb, wk_ref[0]).astype(BF16)
    v_ref[0, 0] = _dot(mb, wv_ref[0]).astype(BF16)


def _mem_kv(mem, w_xk, w_xv):
    bn, m_len, d = mem.shape
    n_l = w_xk.shape[0]
    wspec = pl.BlockSpec((1, d, d), lambda l, b: (l, 0, 0))
    ospec = pl.BlockSpec((1, 1, m_len, d), lambda l, b: (l, b, 0, 0))
    return pl.pallas_call(
        _mem_kv_kernel,
        grid=(n_l, bn),
        in_specs=[pl.BlockSpec((1, m_len, d), lambda l, b: (b, 0, 0)), wspec, wspec],
        out_specs=[ospec, ospec],
        out_shape=[jax.ShapeDtypeStruct((n_l, bn, m_len, d), BF16)] * 2,
        compiler_params=pltpu.CompilerParams(
            dimension_semantics=("arbitrary", "arbitrary"), vmem_limit_bytes=VMEM_LIMIT),
        name="mem_kv",
    )(mem, w_xk, w_xv)


def _first_argmax(vals, idx):
    m = functools.reduce(jnp.maximum, [jnp.max(v, axis=0, keepdims=True) for v in vals])
    big = jnp.int32(1 << 20)
    cand = [jnp.min(jnp.where(v == m, ix, big), axis=0, keepdims=True) for v, ix in zip(vals, idx)]
    first = functools.reduce(jnp.minimum, cand)
    return [ix == first for ix in idx], m, first


def _route(logits_t, bias):
    n = logits_t.shape[1]
    scores = jax.nn.sigmoid(logits_t)
    sel = scores + bias
    sub = lax.broadcasted_iota(jnp.int32, (GROUP_SIZE, n), 0)
    neg = -jnp.inf
    sel_g = [sel[g * GROUP_SIZE:(g + 1) * GROUP_SIZE] for g in range(N_GROUPS)]
    gs = []
    for g in range(N_GROUPS):
        (hot,), m1, _ = _first_argmax([sel_g[g]], [sub])
        m2 = jnp.max(jnp.where(hot, neg, sel_g[g]), axis=0, keepdims=True)
        gs.append(m1 + m2)
    gsv = jnp.concatenate(gs, axis=0)
    gmask = jnp.zeros(gsv.shape, F32)
    for _ in range(TOPK_GROUPS):
        (hot,), _m, _ = _first_argmax([gsv], [sub])
        gmask = jnp.where(hot, 1.0, gmask)
        gsv = jnp.where(hot, neg, gsv)
    msel = [jnp.where(gmask[g:g + 1] > 0.0, sel_g[g], neg) for g in range(N_GROUPS)]
    eidx = [sub + g * GROUP_SIZE for g in range(N_GROUPS)]
    picks, ids = [], []
    for _ in range(TOP_K):
        hots, _m, first = _first_argmax(msel, eidx)
        picks.append(hots)
        ids.append(first)
        msel = [jnp.where(h, neg, v) for h, v in zip(hots, msel)]
    chosen = [functools.reduce(jnp.logical_or, [p[g] for p in picks]) for g in range(N_GROUPS)]
    sc_g = [scores[g * GROUP_SIZE:(g + 1) * GROUP_SIZE] for g in range(N_GROUPS)]
    picked = [jnp.where(c, s, 0.0) for c, s in zip(chosen, sc_g)]
    wsum = functools.reduce(
        lambda a, b: a + b, [jnp.sum(p, axis=0, keepdims=True) for p in picked])
    gates = [p / wsum * ROUTE_SCALE for p in picked]
    return picks, ids, chosen, gates


def _pick_rows(hots, vals):
    return functools.reduce(
        lambda a, b: a + b,
        [jnp.sum(jnp.where(h, v, 0.0), axis=0, keepdims=True) for h, v in zip(hots, vals)])


def _post_kernel(x_ref, yag_ref, sgb_ref, o_ref, km_ref, vm_ref, wao_ref, wmo_ref, ln1_ref,
                 wxq_ref, wxo_ref, ln2_ref, wr_ref, rb_ref, wsgu_ref, wsd_ref,
                 x2p_ref, base_ref, eid_ref, rank_ref, gate_ref, cnt_ref, *, alpha):
    d = x_ref.shape[2]
    ts = x_ref.shape[1]
    first_step = jnp.logical_and(pl.program_id(0) == 0, pl.program_id(1) == 0)

    @pl.when(first_step)
    def _():
        cnt_ref[...] = jnp.zeros_like(cnt_ref)

    x = x_ref[0]
    y_b = _dot(o_ref[0], wao_ref[0])
    y = yag_ref[0].astype(F32) + sgb_ref[0].astype(F32) * y_b
    mix = _dot(y.astype(BF16), wmo_ref[0])
    x1 = _layer_norm(alpha * x + mix, ln1_ref[0, 0:1], ln1_ref[0, 1:2])

    hd = d // XATTN_HEADS
    xq = (_dot(x1.astype(BF16), wxq_ref[0]) * (hd ** -0.5)).astype(BF16)
    heads = []
    for h in range(XATTN_HEADS):
        sl = slice(h * hd, (h + 1) * hd)
        s = _dot_nt(xq[:, sl], km_ref[0, 0, :, sl])
        m = jnp.max(s, axis=-1, keepdims=True)
        p = jnp.exp(s - m)
        den = jnp.sum(p, axis=-1, keepdims=True)
        heads.append((_dot(p.astype(BF16), vm_ref[0, 0, :, sl]) / den).astype(BF16))
    xat = _dot(jnp.concatenate(heads, axis=-1), wxo_ref[0])
    x2 = _layer_norm(alpha * x1 + xat, ln2_ref[0, 0:1], ln2_ref[0, 1:2])

    logits_t = lax.dot_general(wr_ref[0], x2, (((1,), (1,)), ((), ())),
                               preferred_element_type=F32, precision=lax.Precision.HIGHEST)
    picks, ids, chosen, gates = _route(logits_t, rb_ref[0])

    sel_t = jnp.concatenate([c.astype(F32) for c in chosen], axis=0)
    before = (lax.broadcasted_iota(jnp.int32, (ts, ts), 0)
              < lax.broadcasted_iota(jnp.int32, (ts, ts), 1)).astype(BF16)
    running = cnt_ref[:, 0:1]
    rank_all = _dot(sel_t.astype(BF16), before) + running
    cnt_ref[...] = cnt_ref[...] + jnp.sum(sel_t, axis=1, keepdims=True)
    rank_g = [rank_all[g * GROUP_SIZE:(g + 1) * GROUP_SIZE] for g in range(N_GROUPS)]
    eid_ref[0] = jnp.concatenate(ids, axis=0)
    rank_ref[0] = jnp.concatenate([_pick_rows(h, rank_g) for h in picks], axis=0).astype(I32)
    gate_ref[0] = jnp.concatenate([_pick_rows(h, gates) for h in picks], axis=0)

    x2b = x2.astype(BF16)
    x2p_ref[0] = _pack_rows(x2)
    sh = _dot(x2b, wsgu_ref[0])
    f = sh.shape[1] // 2
    hid = _silu(sh[:, :f]) * sh[:, f:]
    base_ref[0] = alpha * x2 + _dot(hid.astype(BF16), wsd_ref[0])


def _post(l, b0, x, yag, sgb, o, kmem, vmem, wao, wmo, ln1, wxq, wxo, ln2, wr, rb, wsgu, wsd,
          ts, alpha):
    bn, s_len, d = x.shape
    m_len = kmem.shape[2]
    tile = lambda w: pl.BlockSpec((1, ts, w), lambda b, i: (b, i, 0))
    pick = pl.BlockSpec((1, TOP_K, ts), lambda b, i: (b, 0, i))
    lay = lambda a: pl.BlockSpec((1,) + a.shape[1:], lambda b, i: (l,) + (0,) * (a.ndim - 1))
    memspec = pl.BlockSpec((1, 1, m_len, d), lambda b, i: (l, b + b0, 0, 0))
    return pl.pallas_call(
        functools.partial(_post_kernel, alpha=alpha),
        grid=(bn, s_len // ts),
        in_specs=[tile(d), tile(d), tile(d), tile(o.shape[2]), memspec, memspec,
                  lay(wao), lay(wmo), lay(ln1), lay(wxq), lay(wxo), lay(ln2), lay(wr), lay(rb),
                  lay(wsgu), lay(wsd)],
        out_specs=[tile(d // 2), tile(d), pick, pick, pick,
                   pl.BlockSpec((N_EXPERTS, LANES), lambda b, i: (0, 0))],
        out_shape=[jax.ShapeDtypeStruct((bn, s_len, d // 2), I32),
                   jax.ShapeDtypeStruct((bn, s_len, d), F32),
                   jax.ShapeDtypeStruct((bn, TOP_K, s_len), I32),
                   jax.ShapeDtypeStruct((bn, TOP_K, s_len), I32),
                   jax.ShapeDtypeStruct((bn, TOP_K, s_len), F32),
                   jax.ShapeDtypeStruct((N_EXPERTS, LANES), F32)],
        compiler_params=pltpu.CompilerParams(
            dimension_semantics=("arbitrary", "arbitrary"), vmem_limit_bytes=VMEM_LIMIT),
        name="post_mixer",
    )(x, yag, sgb, o, kmem, vmem, wao, wmo, ln1, wxq, wxo, ln2, wr, rb, wsgu, wsd)


def _slot_kernel(offs_ref, eid_ref, rank_ref, slot_ref):
    eid = eid_ref[0]
    acc = rank_ref[0]
    for e in range(N_EXPERTS):
        acc = acc + jnp.where(eid == e, offs_ref[e], 0)
    slot_ref[0] = acc


def _slots(offs, eid, rank):
    bn, k, s_len = eid.shape
    spec = pl.BlockSpec((1, k, s_len), lambda b, offs_ref: (b, 0, 0))
    return pl.pallas_call(
        _slot_kernel,
        grid_spec=pltpu.PrefetchScalarGridSpec(
            num_scalar_prefetch=1, grid=(bn,), in_specs=[spec, spec], out_specs=spec),
        out_shape=jax.ShapeDtypeStruct((bn, k, s_len), I32),
        name="moe_slots",
    )(offs, eid, rank)


def _sc_mesh():
    return plsc.VectorSubcoreMesh(core_axis_name="c", subcore_axis_name="s")


def _sc_dispatch(xp, slot_chunks, n_slots):
    t, w = xp.shape
    n_chunks, k, ch = slot_chunks.shape
    per_worker = n_chunks // (SC_CORES * SC_SUBCORES)

    @functools.partial(
        pl.kernel, mesh=_sc_mesh(),
        out_type=jax.ShapeDtypeStruct((n_slots, w), I32),
        scratch_types=[pltpu.VMEM((k, ch), I32), pltpu.VMEM((ch, w), I32), pltpu.SemaphoreType.DMA],
        name="moe_dispatch")
    def body(x_hbm, slot_hbm, out_hbm, idx_v, rows_v, sem):
        worker = lax.axis_index("s") * SC_CORES + lax.axis_index("c")

        @pl.loop(0, per_worker)
        def _(j):
            c = worker * per_worker + j
            pltpu.sync_copy(slot_hbm.at[c], idx_v)
            pltpu.sync_copy(x_hbm.at[pl.ds(c * ch, ch)], rows_v)
            copies = [pltpu.async_copy(rows_v, out_hbm.at[idx_v.at[kk]], sem) for kk in range(k)]
            for cp in copies:
                cp.wait()

    return body(xp, slot_chunks)


def _sc_combine(ys, slot_chunks, t):
    w = ys.shape[1]
    n_chunks, k, ch = slot_chunks.shape
    per_worker = n_chunks // (SC_CORES * SC_SUBCORES)

    @functools.partial(
        pl.kernel, mesh=_sc_mesh(),
        out_type=jax.ShapeDtypeStruct((k, t, w), I32),
        scratch_types=[pltpu.VMEM((k, ch), I32), pltpu.VMEM((ch, w), I32), pltpu.VMEM((ch, w), I32),
                       pltpu.SemaphoreType.DMA, pltpu.SemaphoreType.DMA,
                       pltpu.SemaphoreType.DMA, pltpu.SemaphoreType.DMA],
        name="moe_combine")
    def body(y_hbm, slot_hbm, out_hbm, idx_v, rows0, rows1, g0, g1, w0, w1):
        worker = lax.axis_index("s") * SC_CORES + lax.axis_index("c")
        bufs, gsem, wsem = (rows0, rows1), (g0, g1), (w0, w1)

        @pl.loop(0, per_worker)
        def _(j):
            c = worker * per_worker + j
            pltpu.sync_copy(slot_hbm.at[c], idx_v)
            gathers = [None] * k
            writes = [None] * k
            gathers[0] = pltpu.async_copy(y_hbm.at[idx_v.at[0]], bufs[0], gsem[0])
            for kk in range(k):
                cur = kk % 2
                if kk + 1 < k:
                    if kk >= 1:
                        writes[kk - 1].wait()
                    gathers[kk + 1] = pltpu.async_copy(
                        y_hbm.at[idx_v.at[kk + 1]], bufs[1 - cur], gsem[1 - cur])
                gathers[kk].wait()
                writes[kk] = pltpu.async_copy(
                    bufs[cur], out_hbm.at[kk, pl.ds(c * ch, ch)], wsem[cur])
            writes[k - 2].wait()
            writes[k - 1].wait()

    return body(ys, slot_chunks)


def _experts_kernel(ti_ref, te_ref, nu_ref, xs_ref, wg_ref, wu_ref, wd_ref, ys_ref,
                    wg_b, wu_b, wd_b):
    j = pl.program_id(0)
    prev = te_ref[jnp.maximum(j - 1, 0)]
    new_expert = jnp.logical_or(j == 0, te_ref[j] != prev)

    @pl.when(new_expert)
    def _():
        wg_b[...] = wg_ref[0, 0].astype(BF16)
        wu_b[...] = wu_ref[0, 0].astype(BF16)
        wd_b[...] = wd_ref[0, 0].astype(BF16)

    @pl.when(j < nu_ref[0])
    def _():
        xb = _unpack_rows(xs_ref[...]).astype(BF16)
        hid = _silu(_dot(xb, wg_b[...])) * _dot(xb, wu_b[...])
        ys_ref[...] = _pack_rows(_dot(hid.astype(BF16), wd_b[...]))


def _experts(l, xs, ti, te, nu, w_gate, w_up, w_down):
    n_slots, w = xs.shape
    n_tiles = n_slots // EXPERT_TILE
    d, f = w_gate.shape[2:]
    rows = pl.BlockSpec((EXPERT_TILE, w), lambda j, ti, te, nu: (ti[j], 0))
    wspec = lambda a, b: pl.BlockSpec((1, 1, a, b), lambda j, ti, te, nu: (l, te[j], 0, 0))
    return pl.pallas_call(
        _experts_kernel,
        grid_spec=pltpu.PrefetchScalarGridSpec(
            num_scalar_prefetch=3, grid=(n_tiles,),
            in_specs=[rows, wspec(d, f), wspec(d, f), wspec(f, d)],
            out_specs=rows,
            scratch_shapes=[pltpu.VMEM((d, f), BF16), pltpu.VMEM((d, f), BF16),
                            pltpu.VMEM((f, d), BF16)]),
        out_shape=jax.ShapeDtypeStruct((n_slots, w), I32),
        compiler_params=pltpu.CompilerParams(
            dimension_semantics=("arbitrary",), vmem_limit_bytes=VMEM_LIMIT),
        name="moe_experts",
    )(ti, te, nu, xs, w_gate, w_up, w_down)


def _final_kernel(yg_ref, gate_ref, base_ref, ln_ref, out_ref):
    gates = gate_ref[0].T
    acc = base_ref[0]
    for k in range(yg_ref.shape[0]):
        acc = acc + gates[:, k:k + 1] * _unpack_rows(yg_ref[k, 0])
    out_ref[0] = _layer_norm(acc, ln_ref[0, 0:1], ln_ref[0, 1:2])


def _final(l, yg, gate8, base, ln3, tm):
    bn, s_len, d = base.shape
    k = yg.shape[0]
    return pl.pallas_call(
        _final_kernel,
        grid=(bn, s_len // tm),
        in_specs=[pl.BlockSpec((k, 1, tm, d // 2), lambda b, i: (0, b, i, 0)),
                  pl.BlockSpec((1, k, tm), lambda b, i: (b, 0, i)),
                  pl.BlockSpec((1, tm, d), lambda b, i: (b, i, 0)),
                  pl.BlockSpec((1,) + ln3.shape[1:], lambda b, i: (l, 0, 0))],
        out_specs=pl.BlockSpec((1, tm, d), lambda b, i: (b, i, 0)),
        out_shape=jax.ShapeDtypeStruct((bn, s_len, d), F32),
        compiler_params=pltpu.CompilerParams(
            dimension_semantics=("parallel", "parallel"), vmem_limit_bytes=VMEM_LIMIT),
        name="moe_final",
    )(yg, gate8, base, ln3)


def _moe_schedule(eid, rank, counts):
    bn, _, s_len = eid.shape
    t = bn * s_len
    n_tiles = (t * TOP_K) // EXPERT_TILE + N_EXPERTS

    cnt = counts[:, 0].astype(I32)
    padded = (cnt + EXPERT_TILE - 1) // EXPERT_TILE * EXPERT_TILE
    ends = jnp.cumsum(padded)
    offs = ends - padded
    n_used = ends[-1] // EXPERT_TILE
    ti = jnp.minimum(jnp.arange(n_tiles, dtype=I32), n_used - 1)
    te = jnp.sum((ends[None, :] <= (ti * EXPERT_TILE)[:, None]).astype(I32), axis=1)

    slots = _slots(offs, eid, rank)
    slot_chunks = slots.reshape(bn, TOP_K, s_len // SC_CHUNK, SC_CHUNK).transpose(0, 2, 1, 3)
    slot_chunks = slot_chunks.reshape(t // SC_CHUNK, TOP_K, SC_CHUNK)
    return slot_chunks, ti, te, n_used.reshape(1), n_tiles * EXPERT_TILE


def _head_chunks(w, widths, n_heads, per_head):
    lead = w.shape[:-1]
    wh = w.reshape(lead + (n_heads, per_head))
    parts = [sign * wh[..., a:b] for a, b, sign in widths]
    used = sum(b - a for a, b, _ in widths)
    parts.append(jnp.zeros(lead + (n_heads, LANES - used), w.dtype))
    return jnp.concatenate(parts, axis=-1).reshape(lead + (n_heads * LANES,))


def kernel(x, mem, positions, w_in, w_conv, w_conv_out, q_norm, w_uq, kv_norm, w_uk, w_uv,
           w_attn_out, w_mix_out, ln1_g, ln1_b, w_xq, w_xk, w_xv, w_xo, ln2_g, ln2_b,
           w_router, router_bias, w_gate, w_up, w_down, ws_gate, ws_up, ws_down, ln3_g, ln3_b):
    bn, s_len, d = x.shape
    depth = w_in.shape[0]
    alpha = (2 * depth) ** 0.25
    ts = min(512, s_len)
    tq = min(512, s_len)
    half = QK_ROPE_DIM // 2
    nope, rope = QK_NOPE_DIM, QK_ROPE_DIM

    inv_freq = ROPE_BASE ** (-jnp.arange(half, dtype=F32) / half)
    ang = positions.astype(F32)[..., None] * inv_freq
    cos, sin = jnp.cos(ang), jnp.sin(ang)
    scale = (nope + rope) ** -0.5
    zeros = lambda w: jnp.zeros((bn, s_len, w), F32)
    tabs = jnp.concatenate([
        scale * jnp.ones((bn, s_len, nope), F32), scale * cos, scale * cos, zeros(LANES - nope - rope),
        zeros(nope), scale * sin, scale * sin, zeros(LANES - nope - rope),
        cos, cos, zeros(LANES - rope),
        sin, sin, zeros(LANES - rope)], axis=-1)

    sizes = (CONV_DIM, CONV_DIM, CONV_DIM, Q_RANK, KV_RANK, rope, d, d)
    offs = [0]
    for sz in sizes:
        offs.append(offs[-1] + sz)
    seg = lambda j: w_in[:, :, offs[j]:offs[j + 1]]
    w_kr = seg(5)
    pad = jnp.zeros(w_kr.shape[:2] + (LANES - rope,), w_in.dtype)
    w_krp = jnp.concatenate([-w_kr[..., half:], w_kr[..., :half]], axis=-1)
    wx = jnp.concatenate([seg(0), seg(1), seg(2), seg(3), seg(4), w_kr, pad, w_krp, pad,
                          seg(6), seg(7)], axis=-1).astype(BF16)

    per_q = nope + rope
    wq_main = _head_chunks(w_uq, [(0, per_q, 1.0)], MLA_HEADS, per_q)
    wq_part = _head_chunks(w_uq, [(0, nope, 0.0), (nope + half, per_q, -1.0), (nope, nope + half, 1.0)],
                           MLA_HEADS, per_q)
    wq = jnp.concatenate([wq_main, wq_part], axis=-1).astype(BF16)
    wk = _head_chunks(w_uk, [(0, nope, 1.0)], MLA_HEADS, nope)
    wkv = jnp.concatenate([wk, w_uv], axis=-1).astype(BF16)
    lane_src = jnp.arange(LANES)[:, None]
    lane_dst = jnp.arange(MLA_HEADS * LANES)[None, :] % LANES
    eplace = ((lane_dst == lane_src + nope) & (lane_src < rope)).astype(BF16)

    wconv = w_conv[:, :, 0, :]
    wco = w_conv_out.astype(BF16)
    qn = q_norm[:, None, :]
    kvn = kv_norm[:, None, :]
    wao = w_attn_out.astype(BF16)
    wmo = w_mix_out.astype(BF16)
    ln1 = jnp.stack([ln1_g, ln1_b], axis=1)
    ln2 = jnp.stack([ln2_g, ln2_b], axis=1)
    ln3 = jnp.stack([ln3_g, ln3_b], axis=1)
    wxq = w_xq.astype(BF16)
    wxo = w_xo.astype(BF16)
    wr = jnp.swapaxes(w_router, 1, 2)
    rb = router_bias[:, :, None]
    wsgu = jnp.concatenate([ws_gate, ws_up], axis=-1).astype(BF16)
    wsd = ws_down.astype(BF16)

    kmem, vmem = _mem_kv(mem, w_xk.astype(BF16), w_xv.astype(BF16))

    n_chains = 2 if bn % 2 == 0 else 1
    hb = bn // n_chains
    xs_c = [x[c * hb:(c + 1) * hb] for c in range(n_chains)]
    tabs_c = [tabs[c * hb:(c + 1) * hb] for c in range(n_chains)]
    t = hb * s_len
    for l in range(depth):
        st = []
        for c in range(n_chains):
            yag, sgb, q, k, v = _mixer_in(l, xs_c[c], tabs_c[c], wx, wconv, wco, qn, wq, kvn, wkv,
                                          eplace, ts)
            o = _attention(q, k, v, tq)
            x2p, base, eid, rank, gate8, counts = _post(
                l, c * hb, xs_c[c], yag, sgb, o, kmem, vmem, wao, wmo, ln1, wxq, wxo, ln2, wr, rb,
                wsgu, wsd, ts, alpha)
            slot_chunks, ti, te, nu, n_slots = _moe_schedule(eid, rank, counts)
            xs = _sc_dispatch(x2p.reshape(t, d // 2), slot_chunks, n_slots)
            st.append((xs, slot_chunks, ti, te, nu, gate8, base))
        ygs = []
        for xs, slot_chunks, ti, te, nu, _, _ in st:
            ys = _experts(l, xs, ti, te, nu, w_gate, w_up, w_down)
            ygs.append(_sc_combine(ys, slot_chunks, t))
        for c in range(n_chains):
            xs_c[c] = _final(l, ygs[c].reshape(TOP_K, hb, s_len, d // 2), st[c][5], st[c][6], ln3,
                             min(512, s_len))
    return jnp.concatenate(xs_c, axis=0) if n_chains > 1 else xs_c[0]
```

```python
import functools

import jax
import jax.numpy as jnp
from jax import lax
from jax.experimental import pallas as pl
from jax.experimental.pallas import tpu as pltpu
from jax.experimental.pallas import tpu_sc as plsc

CONV_DIM = 512
CONV_WIDTH = 3
MLA_HEADS = 8
QK_NOPE_DIM = 64
QK_ROPE_DIM = 32
V_HEAD_DIM = 64
Q_RANK = 384
KV_RANK = 256
ROPE_BASE = 10000.0
XATTN_HEADS = 4
N_EXPERTS = 64
TOP_K = 8
N_GROUPS = 8
TOPK_GROUPS = 4
GROUP_SIZE = N_EXPERTS // N_GROUPS
ROUTE_SCALE = 2.5
LN_EPS = 1e-5
RMS_EPS = 1e-6

LANES = 128
HALO_ROWS = 8
VMEM_LIMIT = 56 * 1024 * 1024
SC_CORES = 2
SC_SUBCORES = 16
SC_CHUNK = 64
EXPERT_TILE = 512

BF16 = jnp.bfloat16
F32 = jnp.float32
I32 = jnp.int32


def _dot(a, b):
    return jnp.dot(a, b, preferred_element_type=F32)


def _dot_nt(a, b):
    return lax.dot_general(a, b, (((1,), (1,)), ((), ())), preferred_element_type=F32)


def _layer_norm(x, g, b):
    mu = jnp.mean(x, axis=-1, keepdims=True)
    xc = x - mu
    var = jnp.mean(xc * xc, axis=-1, keepdims=True)
    return xc * lax.rsqrt(var + LN_EPS) * g + b


def _rms_norm(x, g):
    ms = jnp.mean(x * x, axis=-1, keepdims=True)
    return x * lax.rsqrt(ms + RMS_EPS) * g


def _silu(x):
    return x * jax.nn.sigmoid(x)


_HI_MASK = -65536


def _pack_rows(y):
    w = y.shape[1] // 2
    lo = lax.bitcast_convert_type(y[:, :w].astype(BF16).astype(F32), I32)
    hi = lax.bitcast_convert_type(y[:, w:].astype(BF16).astype(F32), I32)
    return lax.shift_right_logical(lo, 16) | (hi & _HI_MASK)


def _unpack_rows(p):
    lo = lax.bitcast_convert_type(lax.shift_left(p, 16), F32)
    hi = lax.bitcast_convert_type(p & _HI_MASK, F32)
    return jnp.concatenate([lo, hi], axis=-1)


_C_BCH = 0
_C_CQ = _C_BCH + 3 * CONV_DIM
_C_CKV = _C_CQ + Q_RANK
_C_KR = _C_CKV + KV_RANK
_C_GA = _C_KR + LANES
_C_GB = _C_GA + 1024


def _mixer_in_kernel(x_ref, xh_ref, tab_ref, wx_ref, wconv_ref, wco_ref, qn_ref, wq_ref,
                     kvn_ref, wkv_ref,
                     yag_ref, sgb_ref, q_ref, k_ref, v_ref, *, d_model):
    i = pl.program_id(1)
    d = d_model
    ts = x_ref.shape[1]
    xb = x_ref[0].astype(BF16)

    bch = _dot(xb, wx_ref[:, _C_BCH:_C_BCH + 3 * CONV_DIM])
    b_gate = bch[:, :CONV_DIM]
    u = bch[:, CONV_DIM:2 * CONV_DIM] * bch[:, 2 * CONV_DIM:]
    xh = xh_ref[0].astype(BF16)
    hp = _dot(xh, wx_ref[:, _C_BCH + CONV_DIM:_C_BCH + 3 * CONV_DIM])
    uh = hp[:, :CONV_DIM] * hp[:, CONV_DIM:]
    uh = jnp.where(i == 0, 0.0, uh)
    row = lax.broadcasted_iota(jnp.int32, (ts, CONV_DIM), 0)
    u1 = jnp.where(row == 0, uh[HALO_ROWS - 1:HALO_ROWS], pltpu.roll(u, 1, 0))
    u2 = jnp.where(row == 0, uh[HALO_ROWS - 2:HALO_ROWS - 1],
                   jnp.where(row == 1, uh[HALO_ROWS - 1:HALO_ROWS], pltpu.roll(u, 2, 0)))
    wc = wconv_ref[0]
    z = wc[0:1] * u2 + wc[1:2] * u1 + wc[2:3] * u
    y_a = _dot((b_gate * z).astype(BF16), wco_ref[0])
    g_a = _dot(xb, wx_ref[:, _C_GA:_C_GA + d])
    yag_ref[0] = (jax.nn.sigmoid(g_a) * y_a).astype(BF16)
    g_b = _dot(xb, wx_ref[:, _C_GB:_C_GB + d])
    sgb_ref[0] = jax.nn.sigmoid(g_b).astype(BF16)

    tab = tab_ref[0]
    t_a, t_m, t_p = tab[:, 0:LANES], tab[:, LANES:2 * LANES], tab[:, 2 * LANES:3 * LANES]
    half = QK_ROPE_DIM // 2

    def rotate(c):
        return (c * t_a + pltpu.roll(c, LANES - half, 1) * t_m + pltpu.roll(c, half, 1) * t_p)

    c_q = _dot(xb, wx_ref[:, _C_CQ:_C_CQ + Q_RANK])
    cqn = _rms_norm(c_q, qn_ref[0]).astype(BF16)
    hw = MLA_HEADS * LANES
    q = _dot(cqn, wq_ref[0])
    for h in range(MLA_HEADS):
        sl = slice(h * LANES, (h + 1) * LANES)
        q_ref[0, :, sl] = rotate(q[:, sl]).astype(BF16)

    c_kv = _dot(xb, wx_ref[:, _C_CKV:_C_CKV + KV_RANK])
    ckvn = _rms_norm(c_kv, kvn_ref[0]).astype(BF16)
    kr_rot = rotate(_dot(xb, wx_ref[:, _C_KR:_C_KR + LANES]))
    k_nope = _dot(ckvn, wkv_ref[0, :, :hw])
    for h in range(MLA_HEADS):
        sl = slice(h * LANES, (h + 1) * LANES)
        k_ref[0, :, sl] = (k_nope[:, sl] + kr_rot).astype(BF16)
    v_ref[0] = _dot(ckvn, wkv_ref[0, :, hw:]).astype(BF16)


def _mixer_in(l, bn, x_b0, tab_b0, x, tabs, wx, wconv, wco, qn, wq, kvn, wkv, ts):
    _, s_len, d = x.shape
    n_t = s_len // ts
    hw = MLA_HEADS * LANES
    vw = MLA_HEADS * V_HEAD_DIM
    tile = lambda w: pl.BlockSpec((1, ts, w), lambda b, i: (b, i, 0))
    lay = lambda a: pl.BlockSpec((1,) + a.shape[1:], lambda b, i: (l,) + (0,) * (a.ndim - 1))
    halo = pl.BlockSpec((1, HALO_ROWS, d),
                        lambda b, i: (b + x_b0, jnp.maximum(i * (ts // HALO_ROWS) - 1, 0), 0))
    return pl.pallas_call(
        functools.partial(_mixer_in_kernel, d_model=d),
        grid=(bn, n_t),
        in_specs=[pl.BlockSpec((1, ts, d), lambda b, i: (b + x_b0, i, 0)), halo,
                  pl.BlockSpec((1, ts, 3 * LANES), lambda b, i: (b + tab_b0, i, 0)),
                  pl.BlockSpec((None,) + wx.shape[1:], lambda b, i: (l, 0, 0)),
                  lay(wconv), lay(wco), lay(qn), lay(wq), lay(kvn), lay(wkv)],
        out_specs=[tile(d), tile(d), tile(hw), tile(hw), tile(vw)],
        out_shape=[jax.ShapeDtypeStruct((bn, s_len, d), BF16),
                   jax.ShapeDtypeStruct((bn, s_len, d), BF16),
                   jax.ShapeDtypeStruct((bn, s_len, hw), BF16),
                   jax.ShapeDtypeStruct((bn, s_len, hw), BF16),
                   jax.ShapeDtypeStruct((bn, s_len, vw), BF16)],
        compiler_params=pltpu.CompilerParams(
            dimension_semantics=("parallel", "arbitrary"), vmem_limit_bytes=VMEM_LIMIT),
        name="mixer_in",
    )(x, x, tabs, wx, wconv, wco, qn, wq, kvn, wkv)


def _attn_kernel(q_ref, k_ref, v_ref, o_ref, *, tq):
    s_len = q_ref.shape[1]
    n_q = s_len // tq
    lane = lax.broadcasted_iota(jnp.int32, (tq, 2 * V_HEAD_DIM), 1)
    for qi in range(n_q):
        kv_len = (qi + 1) * tq
        rows = slice(qi * tq, (qi + 1) * tq)
        v2 = v_ref[0, :kv_len, :]
        r = lax.broadcasted_iota(jnp.int32, (tq, kv_len), 0) + qi * tq
        c = lax.broadcasted_iota(jnp.int32, (tq, kv_len), 1)
        outs = []
        for h in range(2):
            sl = slice(h * LANES, (h + 1) * LANES)
            s = _dot_nt(q_ref[0, rows, sl], k_ref[0, :kv_len, sl])
            s = jnp.where(c <= r, s, -jnp.inf)
            m = jnp.max(s, axis=-1, keepdims=True)
            p = jnp.exp(s - m)
            den = jnp.sum(p, axis=-1, keepdims=True)
            outs.append(_dot(p.astype(BF16), v2) / den)
        o_ref[0, rows, :] = jnp.where(lane < V_HEAD_DIM, outs[0], outs[1]).astype(BF16)


def _attention(q, k, v, tq):
    bn, s_len, _ = q.shape
    n_pairs = MLA_HEADS // 2
    return pl.pallas_call(
        functools.partial(_attn_kernel, tq=tq),
        grid=(bn, n_pairs),
        in_specs=[pl.BlockSpec((1, s_len, 2 * LANES), lambda b, h: (b, 0, h)),
                  pl.BlockSpec((1, s_len, 2 * LANES), lambda b, h: (b, 0, h)),
                  pl.BlockSpec((1, s_len, 2 * V_HEAD_DIM), lambda b, h: (b, 0, h))],
        out_specs=pl.BlockSpec((1, s_len, 2 * V_HEAD_DIM), lambda b, h: (b, 0, h)),
        out_shape=jax.ShapeDtypeStruct((bn, s_len, MLA_HEADS * V_HEAD_DIM), BF16),
        compiler_params=pltpu.CompilerParams(
            dimension_semantics=("parallel", "parallel"), vmem_limit_bytes=VMEM_LIMIT),
        name="mla_attention",
    )(q, k, v)


def _mem_kv_kernel(mem_ref, wk_ref, wv_ref, k_ref, v_ref):
    mb = mem_ref[0].astype(BF16)
    k_ref[0, 0] = _dot(mb, wk_ref[0]).astype(BF16)
    v_ref[0, 0] = _dot(mb, wv_ref[0]).astype(BF16)


def _mem_kv(mem, w_xk, w_xv):
    bn, m_len, d = mem.shape
    n_l = w_xk.shape[0]
    wspec = pl.BlockSpec((1, d, d), lambda l, b: (l, 0, 0))
    ospec = pl.BlockSpec((1, 1, m_len, d), lambda l, b: (l, b, 0, 0))
    return pl.pallas_call(
        _mem_kv_kernel,
        grid=(n_l, bn),
        in_specs=[pl.BlockSpec((1, m_len, d), lambda l, b: (b, 0, 0)), wspec, wspec],
        out_specs=[ospec, ospec],
        out_shape=[jax.ShapeDtypeStruct((n_l, bn, m_len, d), BF16)] * 2,
        compiler_params=pltpu.CompilerParams(
            dimension_semantics=("arbitrary", "arbitrary"), vmem_limit_bytes=VMEM_LIMIT),
        name="mem_kv",
    )(mem, w_xk, w_xv)


def _first_argmax(vals, idx):
    m = jnp.max(functools.reduce(jnp.maximum, vals), axis=0, keepdims=True)
    big = jnp.int32(1 << 20)
    cand = functools.reduce(jnp.minimum, [jnp.where(v == m, ix, big) for v, ix in zip(vals, idx)])
    first = jnp.min(cand, axis=0, keepdims=True)
    return [ix == first for ix in idx], m, first


def _route(logits_t, bias):
    n = logits_t.shape[1]
    scores = jax.nn.sigmoid(logits_t)
    sel = scores + bias
    sub = lax.broadcasted_iota(jnp.int32, (GROUP_SIZE, n), 0)
    neg = -jnp.inf
    sel_g = [sel[g * GROUP_SIZE:(g + 1) * GROUP_SIZE] for g in range(N_GROUPS)]
    gs = []
    for g in range(N_GROUPS):
        (hot,), m1, _ = _first_argmax([sel_g[g]], [sub])
        m2 = jnp.max(jnp.where(hot, neg, sel_g[g]), axis=0, keepdims=True)
        gs.append(m1 + m2)
    gsv = jnp.concatenate(gs, axis=0)
    gmask = jnp.zeros(gsv.shape, F32)
    for _ in range(TOPK_GROUPS):
        (hot,), _m, _ = _first_argmax([gsv], [sub])
        gmask = jnp.where(hot, 1.0, gmask)
        gsv = jnp.where(hot, neg, gsv)
    msel = [jnp.where(gmask[g:g + 1] > 0.0, sel_g[g], neg) for g in range(N_GROUPS)]
    eidx = [sub + g * GROUP_SIZE for g in range(N_GROUPS)]
    picks, ids = [], []
    for _ in range(TOP_K):
        hots, _m, first = _first_argmax(msel, eidx)
        picks.append(hots)
        ids.append(first)
        msel = [jnp.where(h, neg, v) for h, v in zip(hots, msel)]
    chosen = [functools.reduce(jnp.logical_or, [p[g] for p in picks]) for g in range(N_GROUPS)]
    sc_g = [scores[g * GROUP_SIZE:(g + 1) * GROUP_SIZE] for g in range(N_GROUPS)]
    picked = [jnp.where(c, s, 0.0) for c, s in zip(chosen, sc_g)]
    wsum = jnp.sum(functools.reduce(lambda a, b: a + b, picked), axis=0, keepdims=True)
    gates = [p / wsum * ROUTE_SCALE for p in picked]
    return picks, ids, chosen, gates


def _pick_rows(hots, vals):
    acc = functools.reduce(lambda a, b: a + b, [jnp.where(h, v, 0.0) for h, v in zip(hots, vals)])
    return jnp.sum(acc, axis=0, keepdims=True)


def _post_kernel(x_ref, yag_ref, sgb_ref, o_ref, km_ref, vm_ref, wao_ref, wmo_ref, ln1_ref,
                 wxq_ref, wxo_ref, ln2_ref, wr_ref, wsgu_ref, wsd_ref,
                 x2p_ref, base_ref, logit_ref, *, alpha):
    d = x_ref.shape[2]
    x = x_ref[0]
    y_b = _dot(o_ref[0], wao_ref[0])
    y = yag_ref[0].astype(F32) + sgb_ref[0].astype(F32) * y_b
    mix = _dot(y.astype(BF16), wmo_ref[0])
    x1 = _layer_norm(alpha * x + mix, ln1_ref[0, 0:1], ln1_ref[0, 1:2])

    hd = d // XATTN_HEADS
    xq = (_dot(x1.astype(BF16), wxq_ref[0]) * (hd ** -0.5)).astype(BF16)
    heads = []
    for h in range(XATTN_HEADS):
        sl = slice(h * hd, (h + 1) * hd)
        s = _dot_nt(xq[:, sl], km_ref[0, 0, :, sl])
        m = jnp.max(s, axis=-1, keepdims=True)
        p = jnp.exp(s - m)
        den = jnp.sum(p, axis=-1, keepdims=True)
        heads.append((_dot(p.astype(BF16), vm_ref[0, 0, :, sl]) / den).astype(BF16))
    xat = _dot(jnp.concatenate(heads, axis=-1), wxo_ref[0])
    x2 = _layer_norm(alpha * x1 + xat, ln2_ref[0, 0:1], ln2_ref[0, 1:2])

    logit_ref[0] = lax.dot_general(wr_ref[0], x2, (((1,), (1,)), ((), ())),
                                   preferred_element_type=F32, precision=lax.Precision.HIGHEST)

    x2b = x2.astype(BF16)
    x2p_ref[0] = _pack_rows(x2)
    sh = _dot(x2b, wsgu_ref[0])
    f = sh.shape[1] // 2
    hid = _silu(sh[:, :f]) * sh[:, f:]
    base_ref[0] = alpha * x2 + _dot(hid.astype(BF16), wsd_ref[0])


def _post(l, bn, x_b0, mem_b0, x, yag, sgb, o, kmem, vmem, wao, wmo, ln1, wxq, wxo, ln2, wr,
          wsgu, wsd, ts, alpha):
    _, s_len, d = x.shape
    m_len = kmem.shape[2]
    tile = lambda w: pl.BlockSpec((1, ts, w), lambda b, i: (b, i, 0))
    lay = lambda a: pl.BlockSpec((1,) + a.shape[1:], lambda b, i: (l,) + (0,) * (a.ndim - 1))
    memspec = pl.BlockSpec((1, 1, m_len, d), lambda b, i: (l, b + mem_b0, 0, 0))
    return pl.pallas_call(
        functools.partial(_post_kernel, alpha=alpha),
        grid=(bn, s_len // ts),
        in_specs=[pl.BlockSpec((1, ts, d), lambda b, i: (b + x_b0, i, 0)),
                  tile(d), tile(d), tile(o.shape[2]), memspec, memspec,
                  lay(wao), lay(wmo), lay(ln1), lay(wxq), lay(wxo), lay(ln2), lay(wr),
                  lay(wsgu), lay(wsd)],
        out_specs=[tile(d // 2), tile(d),
                   pl.BlockSpec((1, N_EXPERTS, ts), lambda b, i: (b, 0, i))],
        out_shape=[jax.ShapeDtypeStruct((bn, s_len, d // 2), I32),
                   jax.ShapeDtypeStruct((bn, s_len, d), F32),
                   jax.ShapeDtypeStruct((bn, N_EXPERTS, s_len), F32)],
        compiler_params=pltpu.CompilerParams(
            dimension_semantics=("parallel", "arbitrary"), vmem_limit_bytes=VMEM_LIMIT),
        name="post_mixer",
    )(x, yag, sgb, o, kmem, vmem, wao, wmo, ln1, wxq, wxo, ln2, wr, wsgu, wsd)


RANK_BLOCK = 512


def _route_kernel(lg_ref, rb_ref, eid_ref, rank_ref, gate_ref, cnt_ref):
    @pl.when(pl.program_id(0) == 0)
    def _():
        cnt_ref[...] = jnp.zeros_like(cnt_ref)

    n = lg_ref.shape[2]
    blk = min(RANK_BLOCK, n)
    picks, ids, chosen, gates = _route(lg_ref[0], rb_ref[0])

    sel_t = jnp.concatenate([c.astype(F32) for c in chosen], axis=0)
    before = (lax.broadcasted_iota(jnp.int32, (blk, blk), 0)
              < lax.broadcasted_iota(jnp.int32, (blk, blk), 1)).astype(BF16)
    running = cnt_ref[:, 0:1]
    ranks = []
    for j in range(n // blk):
        sb = sel_t[:, j * blk:(j + 1) * blk]
        ranks.append(_dot(sb.astype(BF16), before) + running)
        running = running + jnp.sum(sb, axis=1, keepdims=True)
    cnt_ref[...] = jnp.broadcast_to(running, cnt_ref.shape)
    rank_all = jnp.concatenate(ranks, axis=1)
    rank_g = [rank_all[g * GROUP_SIZE:(g + 1) * GROUP_SIZE] for g in range(N_GROUPS)]
    eid_ref[0] = jnp.concatenate(ids, axis=0)
    rank_ref[0] = jnp.concatenate([_pick_rows(h, rank_g) for h in picks], axis=0).astype(I32)
    gate_ref[0] = jnp.concatenate([_pick_rows(h, gates) for h in picks], axis=0)


def _route_call(l, logits, rb):
    bn, n_e, s_len = logits.shape
    pick = pl.BlockSpec((1, TOP_K, s_len), lambda b: (b, 0, 0))
    return pl.pallas_call(
        _route_kernel,
        grid=(bn,),
        in_specs=[pl.BlockSpec((1, n_e, s_len), lambda b: (b, 0, 0)),
                  pl.BlockSpec((1,) + rb.shape[1:], lambda b: (l, 0, 0))],
        out_specs=[pick, pick, pick, pl.BlockSpec((n_e, LANES), lambda b: (0, 0))],
        out_shape=[jax.ShapeDtypeStruct((bn, TOP_K, s_len), I32),
                   jax.ShapeDtypeStruct((bn, TOP_K, s_len), I32),
                   jax.ShapeDtypeStruct((bn, TOP_K, s_len), F32),
                   jax.ShapeDtypeStruct((n_e, LANES), F32)],
        compiler_params=pltpu.CompilerParams(
            dimension_semantics=("arbitrary",), vmem_limit_bytes=VMEM_LIMIT),
        name="moe_route",
    )(logits, rb)


def _slot_kernel(offs_ref, eid_ref, rank_ref, slot_ref):
    eid = eid_ref[0]
    acc = rank_ref[0]
    for e in range(N_EXPERTS):
        acc = acc + jnp.where(eid == e, offs_ref[e], 0)
    slot_ref[0] = acc


def _slots(offs, eid, rank):
    bn, k, s_len = eid.shape
    spec = pl.BlockSpec((1, k, s_len), lambda b, offs_ref: (b, 0, 0))
    return pl.pallas_call(
        _slot_kernel,
        grid_spec=pltpu.PrefetchScalarGridSpec(
            num_scalar_prefetch=1, grid=(bn,), in_specs=[spec, spec], out_specs=spec),
        out_shape=jax.ShapeDtypeStruct((bn, k, s_len), I32),
        name="moe_slots",
    )(offs, eid, rank)


def _sc_mesh():
    return plsc.VectorSubcoreMesh(core_axis_name="c", subcore_axis_name="s")


def _sc_dispatch(xp, slot_chunks, n_slots):
    t, w = xp.shape
    n_chunks, k, ch = slot_chunks.shape
    per_worker = n_chunks // (SC_CORES * SC_SUBCORES)

    @functools.partial(
        pl.kernel, mesh=_sc_mesh(),
        out_type=jax.ShapeDtypeStruct((n_slots, w), I32),
        scratch_types=[pltpu.VMEM((k, ch), I32), pltpu.VMEM((ch, w), I32), pltpu.SemaphoreType.DMA],
        name="moe_dispatch")
    def body(x_hbm, slot_hbm, out_hbm, idx_v, rows_v, sem):
        worker = lax.axis_index("s") * SC_CORES + lax.axis_index("c")

        @pl.loop(0, per_worker)
        def _(j):
            c = worker * per_worker + j
            pltpu.sync_copy(slot_hbm.at[c], idx_v)
            pltpu.sync_copy(x_hbm.at[pl.ds(c * ch, ch)], rows_v)
            copies = [pltpu.async_copy(rows_v, out_hbm.at[idx_v.at[kk]], sem) for kk in range(k)]
            for cp in copies:
                cp.wait()

    return body(xp, slot_chunks)


def _sc_combine(ys, slot_chunks, t):
    w = ys.shape[1]
    n_chunks, k, ch = slot_chunks.shape
    per_worker = n_chunks // (SC_CORES * SC_SUBCORES)

    @functools.partial(
        pl.kernel, mesh=_sc_mesh(),
        out_type=jax.ShapeDtypeStruct((k, t, w), I32),
        scratch_types=[pltpu.VMEM((k, ch), I32), pltpu.VMEM((ch, w), I32), pltpu.VMEM((ch, w), I32),
                       pltpu.SemaphoreType.DMA, pltpu.SemaphoreType.DMA,
                       pltpu.SemaphoreType.DMA, pltpu.SemaphoreType.DMA],
        name="moe_combine")
    def body(y_hbm, slot_hbm, out_hbm, idx_v, rows0, rows1, g0, g1, w0, w1):
        worker = lax.axis_index("s") * SC_CORES + lax.axis_index("c")
        bufs, gsem, wsem = (rows0, rows1), (g0, g1), (w0, w1)

        @pl.loop(0, per_worker)
        def _(j):
            c = worker * per_worker + j
            pltpu.sync_copy(slot_hbm.at[c], idx_v)
            gathers = [None] * k
            writes = [None] * k
            gathers[0] = pltpu.async_copy(y_hbm.at[idx_v.at[0]], bufs[0], gsem[0])
            for kk in range(k):
                cur = kk % 2
                if kk + 1 < k:
                    if kk >= 1:
                        writes[kk - 1].wait()
                    gathers[kk + 1] = pltpu.async_copy(
                        y_hbm.at[idx_v.at[kk + 1]], bufs[1 - cur], gsem[1 - cur])
                gathers[kk].wait()
                writes[kk] = pltpu.async_copy(
                    bufs[cur], out_hbm.at[kk, pl.ds(c * ch, ch)], wsem[cur])
            writes[k - 2].wait()
            writes[k - 1].wait()

    return body(ys, slot_chunks)


def _experts_kernel(ti_ref, te_ref, nu_ref, xs_ref, wg_ref, wu_ref, wd_ref, ys_ref,
                    wg_b, wu_b, wd_b):
    j = pl.program_id(0)
    prev = te_ref[jnp.maximum(j - 1, 0)]
    new_expert = jnp.logical_or(j == 0, te_ref[j] != prev)

    @pl.when(new_expert)
    def _():
        wg_b[...] = wg_ref[0, 0].astype(BF16)
        wu_b[...] = wu_ref[0, 0].astype(BF16)
        wd_b[...] = wd_ref[0, 0].astype(BF16)

    @pl.when(j < nu_ref[0])
    def _():
        xb = _unpack_rows(xs_ref[...]).astype(BF16)
        hid = _silu(_dot(xb, wg_b[...])) * _dot(xb, wu_b[...])
        ys_ref[...] = _pack_rows(_dot(hid.astype(BF16), wd_b[...]))


def _experts(l, xs, ti, te, nu, w_gate, w_up, w_down):
    n_slots, w = xs.shape
    n_tiles = n_slots // EXPERT_TILE
    d, f = w_gate.shape[2:]
    rows = pl.BlockSpec((EXPERT_TILE, w), lambda j, ti, te, nu: (ti[j], 0))
    wspec = lambda a, b: pl.BlockSpec((1, 1, a, b), lambda j, ti, te, nu: (l, te[j], 0, 0))
    return pl.pallas_call(
        _experts_kernel,
        grid_spec=pltpu.PrefetchScalarGridSpec(
            num_scalar_prefetch=3, grid=(n_tiles,),
            in_specs=[rows, wspec(d, f), wspec(d, f), wspec(f, d)],
            out_specs=rows,
            scratch_shapes=[pltpu.VMEM((d, f), BF16), pltpu.VMEM((d, f), BF16),
                            pltpu.VMEM((f, d), BF16)]),
        out_shape=jax.ShapeDtypeStruct((n_slots, w), I32),
        compiler_params=pltpu.CompilerParams(
            dimension_semantics=("arbitrary",), vmem_limit_bytes=VMEM_LIMIT),
        name="moe_experts",
    )(ti, te, nu, xs, w_gate, w_up, w_down)


def _final_kernel(yg_ref, gate_ref, base_ref, ln_ref, *rest):
    out_ref = rest[-1]
    gates = gate_ref[0].T
    acc = base_ref[0]
    for k in range(yg_ref.shape[0]):
        acc = acc + gates[:, k:k + 1] * _unpack_rows(yg_ref[k, 0])
    out_ref[0] = _layer_norm(acc, ln_ref[0, 0:1], ln_ref[0, 1:2])


def _final(l, yg, gate8, base, ln3, tm, out_rows, out_b0, out_prev):
    bn, s_len, d = base.shape
    k = yg.shape[0]
    in_specs = [pl.BlockSpec((k, 1, tm, d // 2), lambda b, i: (0, b, i, 0)),
                pl.BlockSpec((1, k, tm), lambda b, i: (b, 0, i)),
                pl.BlockSpec((1, tm, d), lambda b, i: (b, i, 0)),
                pl.BlockSpec((1,) + ln3.shape[1:], lambda b, i: (l, 0, 0))]
    args = [yg, gate8, base, ln3]
    aliases = {}
    if out_prev is not None:
        in_specs.append(pl.BlockSpec(memory_space=pl.ANY))
        args.append(out_prev)
        aliases = {len(args) - 1: 0}
    return pl.pallas_call(
        _final_kernel,
        grid=(bn, s_len // tm),
        in_specs=in_specs,
        out_specs=pl.BlockSpec((1, tm, d), lambda b, i: (b + out_b0, i, 0)),
        out_shape=jax.ShapeDtypeStruct((out_rows, s_len, d), F32),
        input_output_aliases=aliases,
        compiler_params=pltpu.CompilerParams(
            dimension_semantics=("parallel", "parallel"), vmem_limit_bytes=VMEM_LIMIT),
        name="moe_final",
    )(*args)


def _moe_schedule(eid, rank, counts):
    bn, _, s_len = eid.shape
    t = bn * s_len
    n_tiles = (t * TOP_K) // EXPERT_TILE + N_EXPERTS

    cnt = counts[:, 0].astype(I32)
    padded = (cnt + EXPERT_TILE - 1) // EXPERT_TILE * EXPERT_TILE
    ends = jnp.cumsum(padded)
    offs = ends - padded
    n_used = ends[-1] // EXPERT_TILE
    ti = jnp.minimum(jnp.arange(n_tiles, dtype=I32), n_used - 1)
    te = jnp.sum((ends[None, :] <= (ti * EXPERT_TILE)[:, None]).astype(I32), axis=1)

    slots = _slots(offs, eid, rank)
    slot_chunks = slots.reshape(bn, TOP_K, s_len // SC_CHUNK, SC_CHUNK).transpose(0, 2, 1, 3)
    slot_chunks = slot_chunks.reshape(t // SC_CHUNK, TOP_K, SC_CHUNK)
    return slot_chunks, ti, te, n_used.reshape(1), n_tiles * EXPERT_TILE


def _head_chunks(w, widths, n_heads, per_head):
    lead = w.shape[:-1]
    wh = w.reshape(lead + (n_heads, per_head))
    parts = [sign * wh[..., a:b] for a, b, sign in widths]
    used = sum(b - a for a, b, _ in widths)
    parts.append(jnp.zeros(lead + (n_heads, LANES - used), w.dtype))
    return jnp.concatenate(parts, axis=-1).reshape(lead + (n_heads * LANES,))


def kernel(x, mem, positions, w_in, w_conv, w_conv_out, q_norm, w_uq, kv_norm, w_uk, w_uv,
           w_attn_out, w_mix_out, ln1_g, ln1_b, w_xq, w_xk, w_xv, w_xo, ln2_g, ln2_b,
           w_router, router_bias, w_gate, w_up, w_down, ws_gate, ws_up, ws_down, ln3_g, ln3_b):
    bn, s_len, d = x.shape
    depth = w_in.shape[0]
    alpha = (2 * depth) ** 0.25
    ts = min(512, s_len)
    tq = min(512, s_len)
    half = QK_ROPE_DIM // 2
    nope, rope = QK_NOPE_DIM, QK_ROPE_DIM

    inv_freq = ROPE_BASE ** (-jnp.arange(half, dtype=F32) / half)
    ang = positions.astype(F32)[..., None] * inv_freq
    cos, sin = jnp.cos(ang), jnp.sin(ang)
    scale = (nope + rope) ** -0.5
    zeros = lambda w: jnp.zeros((bn, s_len, w), F32)
    tail = LANES - nope - rope
    tabs = jnp.concatenate([
        jnp.ones((bn, s_len, nope), F32), cos, cos, zeros(tail),
        zeros(nope), -sin, zeros(half), zeros(tail),
        zeros(nope), zeros(half), sin, zeros(tail)], axis=-1)

    sizes = (CONV_DIM, CONV_DIM, CONV_DIM, Q_RANK, KV_RANK, rope, d, d)
    offs = [0]
    for sz in sizes:
        offs.append(offs[-1] + sz)
    seg = lambda j: w_in[:, :, offs[j]:offs[j + 1]]
    zpad = lambda w: jnp.zeros(w_in.shape[:2] + (w,), w_in.dtype)
    wx = jnp.concatenate([seg(0), seg(1), seg(2), seg(3), seg(4), zpad(nope), seg(5), zpad(tail),
                          seg(6), seg(7)], axis=-1).astype(BF16)

    per_q = nope + rope
    wq = _head_chunks(w_uq, [(0, per_q, scale)], MLA_HEADS, per_q).astype(BF16)
    wk = _head_chunks(w_uk, [(0, nope, 1.0)], MLA_HEADS, nope)
    wkv = jnp.concatenate([wk, w_uv], axis=-1).astype(BF16)

    wconv = w_conv[:, :, 0, :]
    wco = w_conv_out.astype(BF16)
    qn = q_norm[:, None, :]
    kvn = kv_norm[:, None, :]
    wao = w_attn_out.astype(BF16)
    wmo = w_mix_out.astype(BF16)
    ln1 = jnp.stack([ln1_g, ln1_b], axis=1)
    ln2 = jnp.stack([ln2_g, ln2_b], axis=1)
    ln3 = jnp.stack([ln3_g, ln3_b], axis=1)
    wxq = w_xq.astype(BF16)
    wxo = w_xo.astype(BF16)
    wr = jnp.swapaxes(w_router, 1, 2)
    rb = router_bias[:, :, None]
    wsgu = jnp.concatenate([ws_gate, ws_up], axis=-1).astype(BF16)
    wsd = ws_down.astype(BF16)

    kmem, vmem = _mem_kv(mem, w_xk.astype(BF16), w_xv.astype(BF16))

    n_chains = 2 if bn % 2 == 0 else 1
    hb = bn // n_chains
    t = hb * s_len
    tm = min(512, s_len)
    xs_c = [x] * n_chains
    for l in range(depth):
        last = l == depth - 1
        st = []
        for c in range(n_chains):
            x_b0 = c * hb if l == 0 else 0
            yag, sgb, q, k, v = _mixer_in(l, hb, x_b0, c * hb, xs_c[c], tabs, wx, wconv, wco, qn, wq,
                                          kvn, wkv, ts)
            o = _attention(q, k, v, tq)
            x2p, base, logits = _post(l, hb, x_b0, c * hb, xs_c[c], yag, sgb, o, kmem, vmem, wao, wmo,
                                      ln1, wxq, wxo, ln2, wr, wsgu, wsd, ts, alpha)
            eid, rank, gate8, counts = _route_call(l, logits, rb)
            slot_chunks, ti, te, nu, n_slots = _moe_schedule(eid, rank, counts)
            xs = _sc_dispatch(x2p.reshape(t, d // 2), slot_chunks, n_slots)
            st.append((xs, slot_chunks, ti, te, nu, gate8, base))
        ygs = []
        for xs, slot_chunks, ti, te, nu, _, _ in st:
            ys = _experts(l, xs, ti, te, nu, w_gate, w_up, w_down)
            ygs.append(_sc_combine(ys, slot_chunks, t))
        out = None
        for c in range(n_chains):
            yg = ygs[c].reshape(TOP_K, hb, s_len, d // 2)
            if last:
                out = _final(l, yg, st[c][5], st[c][6], ln3, tm, bn, c * hb, out)
            else:
                xs_c[c] = _final(l, yg, st[c][5], st[c][6], ln3, tm, hb, 0, None)
    return out
```

```python
import functools

import jax
import jax.numpy as jnp
from jax import lax
from jax.experimental import pallas as pl
from jax.experimental.pallas import tpu as pltpu
from jax.experimental.pallas import tpu_sc as plsc

CONV_DIM = 512
CONV_WIDTH = 3
MLA_HEADS = 8
QK_NOPE_DIM = 64
QK_ROPE_DIM = 32
V_HEAD_DIM = 64
Q_RANK = 384
KV_RANK = 256
ROPE_BASE = 10000.0
XATTN_HEADS = 4
N_EXPERTS = 64
TOP_K = 8
N_GROUPS = 8
TOPK_GROUPS = 4
GROUP_SIZE = N_EXPERTS // N_GROUPS
ROUTE_SCALE = 2.5
LN_EPS = 1e-5
RMS_EPS = 1e-6

LANES = 128
HALO_ROWS = 8
VMEM_LIMIT = 56 * 1024 * 1024
SC_CORES = 2
SC_SUBCORES = 16
SC_CHUNK = 64
EXPERT_TILE = 512

BF16 = jnp.bfloat16
F32 = jnp.float32
I32 = jnp.int32


_DEP_SPEC = pl.BlockSpec(memory_space=pl.ANY)


def _dot(a, b):
    return jnp.dot(a, b, preferred_element_type=F32)


def _dot_nt(a, b):
    return lax.dot_general(a, b, (((1,), (1,)), ((), ())), preferred_element_type=F32)


def _layer_norm(x, g, b):
    mu = jnp.mean(x, axis=-1, keepdims=True)
    xc = x - mu
    var = jnp.mean(xc * xc, axis=-1, keepdims=True)
    return xc * lax.rsqrt(var + LN_EPS) * g + b


def _rms_norm(x, g):
    ms = jnp.mean(x * x, axis=-1, keepdims=True)
    return x * lax.rsqrt(ms + RMS_EPS) * g


def _silu(x):
    return x * jax.nn.sigmoid(x)


_HI_MASK = -65536


def _pack_rows(y):
    w = y.shape[1] // 2
    lo = lax.bitcast_convert_type(y[:, :w].astype(BF16).astype(F32), I32)
    hi = lax.bitcast_convert_type(y[:, w:].astype(BF16).astype(F32), I32)
    return lax.shift_right_logical(lo, 16) | (hi & _HI_MASK)


def _unpack_rows(p):
    lo = lax.bitcast_convert_type(lax.shift_left(p, 16), F32)
    hi = lax.bitcast_convert_type(p & _HI_MASK, F32)
    return jnp.concatenate([lo, hi], axis=-1)


_C_BCH = 0
_C_CQ = _C_BCH + 3 * CONV_DIM
_C_CKV = _C_CQ + Q_RANK
_C_KR = _C_CKV + KV_RANK
_C_GA = _C_KR + LANES
_C_GB = _C_GA + 1024


def _mixer_in_kernel(x_ref, xh_ref, tab_ref, wx_ref, wconv_ref, wco_ref, qn_ref, wq_ref,
                     kvn_ref, wkv_ref, dep_ref,
                     yag_ref, sgb_ref, q_ref, k_ref, v_ref, *, d_model):
    i = pl.program_id(1)
    d = d_model
    ts = x_ref.shape[1]
    xb = x_ref[0].astype(BF16)

    bch = _dot(xb, wx_ref[:, _C_BCH:_C_BCH + 3 * CONV_DIM])
    b_gate = bch[:, :CONV_DIM]
    u = bch[:, CONV_DIM:2 * CONV_DIM] * bch[:, 2 * CONV_DIM:]
    xh = xh_ref[0].astype(BF16)
    hp = _dot(xh, wx_ref[:, _C_BCH + CONV_DIM:_C_BCH + 3 * CONV_DIM])
    uh = hp[:, :CONV_DIM] * hp[:, CONV_DIM:]
    uh = jnp.where(i == 0, 0.0, uh)
    row = lax.broadcasted_iota(jnp.int32, (ts, CONV_DIM), 0)
    u1 = jnp.where(row == 0, uh[HALO_ROWS - 1:HALO_ROWS], pltpu.roll(u, 1, 0))
    u2 = jnp.where(row == 0, uh[HALO_ROWS - 2:HALO_ROWS - 1],
                   jnp.where(row == 1, uh[HALO_ROWS - 1:HALO_ROWS], pltpu.roll(u, 2, 0)))
    wc = wconv_ref[0]
    z = wc[0:1] * u2 + wc[1:2] * u1 + wc[2:3] * u
    y_a = _dot((b_gate * z).astype(BF16), wco_ref[0])
    g_a = _dot(xb, wx_ref[:, _C_GA:_C_GA + d])
    yag_ref[0] = (jax.nn.sigmoid(g_a) * y_a).astype(BF16)
    g_b = _dot(xb, wx_ref[:, _C_GB:_C_GB + d])
    sgb_ref[0] = jax.nn.sigmoid(g_b).astype(BF16)

    tab = tab_ref[0]
    t_a, t_m, t_p = tab[:, 0:LANES], tab[:, LANES:2 * LANES], tab[:, 2 * LANES:3 * LANES]
    half = QK_ROPE_DIM // 2

    def rotate(c):
        return (c * t_a + pltpu.roll(c, LANES - half, 1) * t_m + pltpu.roll(c, half, 1) * t_p)

    c_q = _dot(xb, wx_ref[:, _C_CQ:_C_CQ + Q_RANK])
    cqn = _rms_norm(c_q, qn_ref[0]).astype(BF16)
    hw = MLA_HEADS * LANES
    q = _dot(cqn, wq_ref[0])
    for h in range(MLA_HEADS):
        sl = slice(h * LANES, (h + 1) * LANES)
        q_ref[0, :, sl] = rotate(q[:, sl]).astype(BF16)

    c_kv = _dot(xb, wx_ref[:, _C_CKV:_C_CKV + KV_RANK])
    ckvn = _rms_norm(c_kv, kvn_ref[0]).astype(BF16)
    kr_rot = rotate(_dot(xb, wx_ref[:, _C_KR:_C_KR + LANES]))
    k_nope = _dot(ckvn, wkv_ref[0, :, :hw])
    for h in range(MLA_HEADS):
        sl = slice(h * LANES, (h + 1) * LANES)
        k_ref[0, :, sl] = (k_nope[:, sl] + kr_rot).astype(BF16)
    v_ref[0] = _dot(ckvn, wkv_ref[0, :, hw:]).astype(BF16)


def _mixer_in(l, bn, x_b0, tab_b0, x, tabs, wx, wconv, wco, qn, wq, kvn, wkv, ts, dep):
    _, s_len, d = x.shape
    n_t = s_len // ts
    hw = MLA_HEADS * LANES
    vw = MLA_HEADS * V_HEAD_DIM
    tile = lambda w: pl.BlockSpec((1, ts, w), lambda b, i: (b, i, 0))
    lay = lambda a: pl.BlockSpec((1,) + a.shape[1:], lambda b, i: (l,) + (0,) * (a.ndim - 1))
    halo = pl.BlockSpec((1, HALO_ROWS, d),
                        lambda b, i: (b + x_b0, jnp.maximum(i * (ts // HALO_ROWS) - 1, 0), 0))
    return pl.pallas_call(
        functools.partial(_mixer_in_kernel, d_model=d),
        grid=(bn, n_t),
        in_specs=[pl.BlockSpec((1, ts, d), lambda b, i: (b + x_b0, i, 0)), halo,
                  pl.BlockSpec((1, ts, 3 * LANES), lambda b, i: (b + tab_b0, i, 0)),
                  pl.BlockSpec((None,) + wx.shape[1:], lambda b, i: (l, 0, 0)),
                  lay(wconv), lay(wco), lay(qn), lay(wq), lay(kvn), lay(wkv), _DEP_SPEC],
        out_specs=[tile(d), tile(d), tile(hw), tile(hw), tile(vw)],
        out_shape=[jax.ShapeDtypeStruct((bn, s_len, d), BF16),
                   jax.ShapeDtypeStruct((bn, s_len, d), BF16),
                   jax.ShapeDtypeStruct((bn, s_len, hw), BF16),
                   jax.ShapeDtypeStruct((bn, s_len, hw), BF16),
                   jax.ShapeDtypeStruct((bn, s_len, vw), BF16)],
        compiler_params=pltpu.CompilerParams(
            dimension_semantics=("parallel", "arbitrary"), vmem_limit_bytes=VMEM_LIMIT),
        name="mixer_in",
    )(x, x, tabs, wx, wconv, wco, qn, wq, kvn, wkv, dep)


def _attn_kernel(q_ref, k_ref, v_ref, dep_ref, o_ref, *, tq):
    s_len = q_ref.shape[1]
    n_q = s_len // tq
    lane = lax.broadcasted_iota(jnp.int32, (tq, 2 * V_HEAD_DIM), 1)
    for qi in range(n_q):
        kv_len = (qi + 1) * tq
        rows = slice(qi * tq, (qi + 1) * tq)
        v2 = v_ref[0, :kv_len, :]
        r = lax.broadcasted_iota(jnp.int32, (tq, kv_len), 0) + qi * tq
        c = lax.broadcasted_iota(jnp.int32, (tq, kv_len), 1)
        outs = []
        for h in range(2):
            sl = slice(h * LANES, (h + 1) * LANES)
            s = _dot_nt(q_ref[0, rows, sl], k_ref[0, :kv_len, sl])
            s = jnp.where(c <= r, s, -jnp.inf)
            m = jnp.max(s, axis=-1, keepdims=True)
            p = jnp.exp(s - m)
            den = jnp.sum(p, axis=-1, keepdims=True)
            outs.append(_dot(p.astype(BF16), v2) / den)
        o_ref[0, rows, :] = jnp.where(lane < V_HEAD_DIM, outs[0], outs[1]).astype(BF16)


def _attention(q, k, v, tq, dep):
    bn, s_len, _ = q.shape
    n_pairs = MLA_HEADS // 2
    return pl.pallas_call(
        functools.partial(_attn_kernel, tq=tq),
        grid=(bn, n_pairs),
        in_specs=[pl.BlockSpec((1, s_len, 2 * LANES), lambda b, h: (b, 0, h)),
                  pl.BlockSpec((1, s_len, 2 * LANES), lambda b, h: (b, 0, h)),
                  pl.BlockSpec((1, s_len, 2 * V_HEAD_DIM), lambda b, h: (b, 0, h)), _DEP_SPEC],
        out_specs=pl.BlockSpec((1, s_len, 2 * V_HEAD_DIM), lambda b, h: (b, 0, h)),
        out_shape=jax.ShapeDtypeStruct((bn, s_len, MLA_HEADS * V_HEAD_DIM), BF16),
        compiler_params=pltpu.CompilerParams(
            dimension_semantics=("parallel", "parallel"), vmem_limit_bytes=VMEM_LIMIT),
        name="mla_attention",
    )(q, k, v, dep)


def _mem_kv_kernel(mem_ref, wk_ref, wv_ref, k_ref, v_ref):
    mb = mem_ref[0].astype(BF16)
    k_ref[0, 0] = _dot(mb, wk_ref[0]).astype(BF16)
    v_ref[0, 0] = _dot(mb, wv_ref[0]).astype(BF16)


def _mem_kv(mem, w_xk, w_xv):
    bn, m_len, d = mem.shape
    n_l = w_xk.shape[0]
    wspec = pl.BlockSpec((1, d, d), lambda l, b: (l, 0, 0))
    ospec = pl.BlockSpec((1, 1, m_len, d), lambda l, b: (l, b, 0, 0))
    return pl.pallas_call(
        _mem_kv_kernel,
        grid=(n_l, bn),
        in_specs=[pl.BlockSpec((1, m_len, d), lambda l, b: (b, 0, 0)), wspec, wspec],
        out_specs=[ospec, ospec],
        out_shape=[jax.ShapeDtypeStruct((n_l, bn, m_len, d), BF16)] * 2,
        compiler_params=pltpu.CompilerParams(
            dimension_semantics=("arbitrary", "arbitrary"), vmem_limit_bytes=VMEM_LIMIT),
        name="mem_kv",
    )(mem, w_xk, w_xv)


def _first_argmax(vals, idx):
    m = jnp.max(functools.reduce(jnp.maximum, vals), axis=0, keepdims=True)
    big = jnp.int32(1 << 20)
    cand = functools.reduce(jnp.minimum, [jnp.where(v == m, ix, big) for v, ix in zip(vals, idx)])
    first = jnp.min(cand, axis=0, keepdims=True)
    return [ix == first for ix in idx], m, first


def _route(logits_t, bias):
    n = logits_t.shape[1]
    scores = jax.nn.sigmoid(logits_t)
    sel = scores + bias
    sub = lax.broadcasted_iota(jnp.int32, (GROUP_SIZE, n), 0)
    neg = -jnp.inf
    sel_g = [sel[g * GROUP_SIZE:(g + 1) * GROUP_SIZE] for g in range(N_GROUPS)]
    gs = []
    for g in range(N_GROUPS):
        (hot,), m1, _ = _first_argmax([sel_g[g]], [sub])
        m2 = jnp.max(jnp.where(hot, neg, sel_g[g]), axis=0, keepdims=True)
        gs.append(m1 + m2)
    gsv = jnp.concatenate(gs, axis=0)
    gmask = jnp.zeros(gsv.shape, F32)
    for _ in range(TOPK_GROUPS):
        (hot,), _m, _ = _first_argmax([gsv], [sub])
        gmask = jnp.where(hot, 1.0, gmask)
        gsv = jnp.where(hot, neg, gsv)
    msel = [jnp.where(gmask[g:g + 1] > 0.0, sel_g[g], neg) for g in range(N_GROUPS)]
    eidx = [sub + g * GROUP_SIZE for g in range(N_GROUPS)]
    picks, ids = [], []
    for _ in range(TOP_K):
        hots, _m, first = _first_argmax(msel, eidx)
        picks.append(hots)
        ids.append(first)
        msel = [jnp.where(h, neg, v) for h, v in zip(hots, msel)]
    chosen = [functools.reduce(jnp.logical_or, [p[g] for p in picks]) for g in range(N_GROUPS)]
    sc_g = [scores[g * GROUP_SIZE:(g + 1) * GROUP_SIZE] for g in range(N_GROUPS)]
    picked = [jnp.where(c, s, 0.0) for c, s in zip(chosen, sc_g)]
    wsum = jnp.sum(functools.reduce(lambda a, b: a + b, picked), axis=0, keepdims=True)
    gates = [p / wsum * ROUTE_SCALE for p in picked]
    return picks, ids, chosen, gates


def _pick_rows(hots, vals):
    acc = functools.reduce(lambda a, b: a + b, [jnp.where(h, v, 0.0) for h, v in zip(hots, vals)])
    return jnp.sum(acc, axis=0, keepdims=True)


def _post_kernel(x_ref, yag_ref, sgb_ref, o_ref, km_ref, vm_ref, wao_ref, wmo_ref, ln1_ref,
                 wxq_ref, wxo_ref, ln2_ref, wr_ref, wsgu_ref, wsd_ref, dep_ref,
                 x2p_ref, base_ref, logit_ref, *, alpha):
    d = x_ref.shape[2]
    x = x_ref[0]
    y_b = _dot(o_ref[0], wao_ref[0])
    y = yag_ref[0].astype(F32) + sgb_ref[0].astype(F32) * y_b
    mix = _dot(y.astype(BF16), wmo_ref[0])
    x1 = _layer_norm(alpha * x + mix, ln1_ref[0, 0:1], ln1_ref[0, 1:2])

    hd = d // XATTN_HEADS
    xq = (_dot(x1.astype(BF16), wxq_ref[0]) * (hd ** -0.5)).astype(BF16)
    heads = []
    for h in range(XATTN_HEADS):
        sl = slice(h * hd, (h + 1) * hd)
        s = _dot_nt(xq[:, sl], km_ref[0, 0, :, sl])
        m = jnp.max(s, axis=-1, keepdims=True)
        p = jnp.exp(s - m)
        den = jnp.sum(p, axis=-1, keepdims=True)
        heads.append((_dot(p.astype(BF16), vm_ref[0, 0, :, sl]) / den).astype(BF16))
    xat = _dot(jnp.concatenate(heads, axis=-1), wxo_ref[0])
    x2 = _layer_norm(alpha * x1 + xat, ln2_ref[0, 0:1], ln2_ref[0, 1:2])

    logit_ref[0] = lax.dot_general(wr_ref[0], x2, (((1,), (1,)), ((), ())),
                                   preferred_element_type=F32, precision=lax.Precision.HIGHEST)

    x2b = x2.astype(BF16)
    x2p_ref[0] = _pack_rows(x2)
    sh = _dot(x2b, wsgu_ref[0])
    f = sh.shape[1] // 2
    hid = _silu(sh[:, :f]) * sh[:, f:]
    base_ref[0] = alpha * x2 + _dot(hid.astype(BF16), wsd_ref[0])


def _post(l, bn, x_b0, mem_b0, x, yag, sgb, o, kmem, vmem, wao, wmo, ln1, wxq, wxo, ln2, wr,
          wsgu, wsd, ts, alpha, dep):
    _, s_len, d = x.shape
    m_len = kmem.shape[2]
    tile = lambda w: pl.BlockSpec((1, ts, w), lambda b, i: (b, i, 0))
    lay = lambda a: pl.BlockSpec((1,) + a.shape[1:], lambda b, i: (l,) + (0,) * (a.ndim - 1))
    memspec = pl.BlockSpec((1, 1, m_len, d), lambda b, i: (l, b + mem_b0, 0, 0))
    return pl.pallas_call(
        functools.partial(_post_kernel, alpha=alpha),
        grid=(bn, s_len // ts),
        in_specs=[pl.BlockSpec((1, ts, d), lambda b, i: (b + x_b0, i, 0)),
                  tile(d), tile(d), tile(o.shape[2]), memspec, memspec,
                  lay(wao), lay(wmo), lay(ln1), lay(wxq), lay(wxo), lay(ln2), lay(wr),
                  lay(wsgu), lay(wsd), _DEP_SPEC],
        out_specs=[tile(d // 2), tile(d),
                   pl.BlockSpec((1, N_EXPERTS, ts), lambda b, i: (b, 0, i))],
        out_shape=[jax.ShapeDtypeStruct((bn, s_len, d // 2), I32),
                   jax.ShapeDtypeStruct((bn, s_len, d), F32),
                   jax.ShapeDtypeStruct((bn, N_EXPERTS, s_len), F32)],
        compiler_params=pltpu.CompilerParams(
            dimension_semantics=("parallel", "arbitrary"), vmem_limit_bytes=VMEM_LIMIT),
        name="post_mixer",
    )(x, yag, sgb, o, kmem, vmem, wao, wmo, ln1, wxq, wxo, ln2, wr, wsgu, wsd, dep)


RANK_BLOCK = 512


def _route_kernel(lg_ref, rb_ref, eid_ref, rank_ref, gate_ref, cnt_ref):
    @pl.when(pl.program_id(0) == 0)
    def _():
        cnt_ref[...] = jnp.zeros_like(cnt_ref)

    n = lg_ref.shape[2]
    blk = min(RANK_BLOCK, n)
    picks, ids, chosen, gates = _route(lg_ref[0], rb_ref[0])

    sel_t = jnp.concatenate([c.astype(F32) for c in chosen], axis=0)
    before = (lax.broadcasted_iota(jnp.int32, (blk, blk), 0)
              < lax.broadcasted_iota(jnp.int32, (blk, blk), 1)).astype(BF16)
    running = cnt_ref[:, 0:1]
    ranks = []
    for j in range(n // blk):
        sb = sel_t[:, j * blk:(j + 1) * blk]
        ranks.append(_dot(sb.astype(BF16), before) + running)
        running = running + jnp.sum(sb, axis=1, keepdims=True)
    cnt_ref[...] = jnp.broadcast_to(running, cnt_ref.shape)
    rank_all = jnp.concatenate(ranks, axis=1)
    rank_g = [rank_all[g * GROUP_SIZE:(g + 1) * GROUP_SIZE] for g in range(N_GROUPS)]
    eid_ref[0] = jnp.concatenate(ids, axis=0)
    rank_ref[0] = jnp.concatenate([_pick_rows(h, rank_g) for h in picks], axis=0).astype(I32)
    gate_ref[0] = jnp.concatenate([_pick_rows(h, gates) for h in picks], axis=0)


def _route_call(l, logits, rb):
    bn, n_e, s_len = logits.shape
    pick = pl.BlockSpec((1, TOP_K, s_len), lambda b: (b, 0, 0))
    return pl.pallas_call(
        _route_kernel,
        grid=(bn,),
        in_specs=[pl.BlockSpec((1, n_e, s_len), lambda b: (b, 0, 0)),
                  pl.BlockSpec((1,) + rb.shape[1:], lambda b: (l, 0, 0))],
        out_specs=[pick, pick, pick, pl.BlockSpec((n_e, LANES), lambda b: (0, 0))],
        out_shape=[jax.ShapeDtypeStruct((bn, TOP_K, s_len), I32),
                   jax.ShapeDtypeStruct((bn, TOP_K, s_len), I32),
                   jax.ShapeDtypeStruct((bn, TOP_K, s_len), F32),
                   jax.ShapeDtypeStruct((n_e, LANES), F32)],
        compiler_params=pltpu.CompilerParams(
            dimension_semantics=("arbitrary",), vmem_limit_bytes=VMEM_LIMIT),
        name="moe_route",
    )(logits, rb)


def _slot_kernel(offs_ref, eid_ref, rank_ref, slot_ref):
    eid = eid_ref[0]
    acc = rank_ref[0]
    for e in range(N_EXPERTS):
        acc = acc + jnp.where(eid == e, offs_ref[e], 0)
    slot_ref[0] = acc


def _slots(offs, eid, rank):
    bn, k, s_len = eid.shape
    spec = pl.BlockSpec((1, k, s_len), lambda b, offs_ref: (b, 0, 0))
    return pl.pallas_call(
        _slot_kernel,
        grid_spec=pltpu.PrefetchScalarGridSpec(
            num_scalar_prefetch=1, grid=(bn,), in_specs=[spec, spec], out_specs=spec),
        out_shape=jax.ShapeDtypeStruct((bn, k, s_len), I32),
        name="moe_slots",
    )(offs, eid, rank)


def _sc_mesh():
    return plsc.VectorSubcoreMesh(core_axis_name="c", subcore_axis_name="s")


def _sc_dispatch(xp, slot_chunks, n_slots, dep):
    t, w = xp.shape
    n_chunks, k, ch = slot_chunks.shape
    per_worker = n_chunks // (SC_CORES * SC_SUBCORES)

    @functools.partial(
        pl.kernel, mesh=_sc_mesh(),
        out_type=jax.ShapeDtypeStruct((n_slots, w), I32),
        scratch_types=[pltpu.VMEM((k, ch), I32), pltpu.VMEM((ch, w), I32), pltpu.SemaphoreType.DMA],
        name="moe_dispatch")
    def body(x_hbm, slot_hbm, dep_hbm, out_hbm, idx_v, rows_v, sem):
        worker = lax.axis_index("s") * SC_CORES + lax.axis_index("c")

        @pl.loop(0, per_worker)
        def _(j):
            c = worker * per_worker + j
            pltpu.sync_copy(slot_hbm.at[c], idx_v)
            pltpu.sync_copy(x_hbm.at[pl.ds(c * ch, ch)], rows_v)
            copies = [pltpu.async_copy(rows_v, out_hbm.at[idx_v.at[kk]], sem) for kk in range(k)]
            for cp in copies:
                cp.wait()

    return body(xp, slot_chunks, dep)


def _sc_combine(ys, slot_chunks, t):
    w = ys.shape[1]
    n_chunks, k, ch = slot_chunks.shape
    per_worker = n_chunks // (SC_CORES * SC_SUBCORES)

    @functools.partial(
        pl.kernel, mesh=_sc_mesh(),
        out_type=jax.ShapeDtypeStruct((k, t, w), I32),
        scratch_types=[pltpu.VMEM((k, ch), I32), pltpu.VMEM((ch, w), I32), pltpu.VMEM((ch, w), I32),
                       pltpu.SemaphoreType.DMA, pltpu.SemaphoreType.DMA,
                       pltpu.SemaphoreType.DMA, pltpu.SemaphoreType.DMA],
        name="moe_combine")
    def body(y_hbm, slot_hbm, out_hbm, idx_v, rows0, rows1, g0, g1, w0, w1):
        worker = lax.axis_index("s") * SC_CORES + lax.axis_index("c")
        bufs, gsem, wsem = (rows0, rows1), (g0, g1), (w0, w1)

        @pl.loop(0, per_worker)
        def _(j):
            c = worker * per_worker + j
            pltpu.sync_copy(slot_hbm.at[c], idx_v)
            gathers = [None] * k
            writes = [None] * k
            gathers[0] = pltpu.async_copy(y_hbm.at[idx_v.at[0]], bufs[0], gsem[0])
            for kk in range(k):
                cur = kk % 2
                if kk + 1 < k:
                    if kk >= 1:
                        writes[kk - 1].wait()
                    gathers[kk + 1] = pltpu.async_copy(
                        y_hbm.at[idx_v.at[kk + 1]], bufs[1 - cur], gsem[1 - cur])
                gathers[kk].wait()
                writes[kk] = pltpu.async_copy(
                    bufs[cur], out_hbm.at[kk, pl.ds(c * ch, ch)], wsem[cur])
            writes[k - 2].wait()
            writes[k - 1].wait()

    return body(ys, slot_chunks)


def _experts_kernel(ti_ref, te_ref, nu_ref, xs_ref, wg_ref, wu_ref, wd_ref, dep_ref, ys_ref,
                    wg_b, wu_b, wd_b):
    j = pl.program_id(0)
    prev = te_ref[jnp.maximum(j - 1, 0)]
    new_expert = jnp.logical_or(j == 0, te_ref[j] != prev)

    @pl.when(new_expert)
    def _():
        wg_b[...] = wg_ref[0, 0].astype(BF16)
        wu_b[...] = wu_ref[0, 0].astype(BF16)
        wd_b[...] = wd_ref[0, 0].astype(BF16)

    @pl.when(j < nu_ref[0])
    def _():
        xb = _unpack_rows(xs_ref[...]).astype(BF16)
        hid = _silu(_dot(xb, wg_b[...])) * _dot(xb, wu_b[...])
        ys_ref[...] = _pack_rows(_dot(hid.astype(BF16), wd_b[...]))


def _experts(l, xs, ti, te, nu, w_gate, w_up, w_down, dep):
    n_slots, w = xs.shape
    n_tiles = n_slots // EXPERT_TILE
    d, f = w_gate.shape[2:]
    rows = pl.BlockSpec((EXPERT_TILE, w), lambda j, ti, te, nu: (ti[j], 0))
    wspec = lambda a, b: pl.BlockSpec((1, 1, a, b), lambda j, ti, te, nu: (l, te[j], 0, 0))
    return pl.pallas_call(
        _experts_kernel,
        grid_spec=pltpu.PrefetchScalarGridSpec(
            num_scalar_prefetch=3, grid=(n_tiles,),
            in_specs=[rows, wspec(d, f), wspec(d, f), wspec(f, d), _DEP_SPEC],
            out_specs=rows,
            scratch_shapes=[pltpu.VMEM((d, f), BF16), pltpu.VMEM((d, f), BF16),
                            pltpu.VMEM((f, d), BF16)]),
        out_shape=jax.ShapeDtypeStruct((n_slots, w), I32),
        compiler_params=pltpu.CompilerParams(
            dimension_semantics=("arbitrary",), vmem_limit_bytes=VMEM_LIMIT),
        name="moe_experts",
    )(ti, te, nu, xs, w_gate, w_up, w_down, dep)


def _final_kernel(yg_ref, gate_ref, base_ref, ln_ref, *rest):
    out_ref = rest[-1]
    gates = gate_ref[0].T
    acc = base_ref[0]
    for k in range(yg_ref.shape[0]):
        acc = acc + gates[:, k:k + 1] * _unpack_rows(yg_ref[k, 0])
    out_ref[0] = _layer_norm(acc, ln_ref[0, 0:1], ln_ref[0, 1:2])


def _final(l, yg, gate8, base, ln3, tm, out_rows, out_b0, out_prev, dep):
    bn, s_len, d = base.shape
    k = yg.shape[0]
    in_specs = [pl.BlockSpec((k, 1, tm, d // 2), lambda b, i: (0, b, i, 0)),
                pl.BlockSpec((1, k, tm), lambda b, i: (b, 0, i)),
                pl.BlockSpec((1, tm, d), lambda b, i: (b, i, 0)),
                pl.BlockSpec((1,) + ln3.shape[1:], lambda b, i: (l, 0, 0)), _DEP_SPEC]
    args = [yg, gate8, base, ln3, dep]
    aliases = {}
    if out_prev is not None:
        in_specs.append(pl.BlockSpec(memory_space=pl.ANY))
        args.append(out_prev)
        aliases = {len(args) - 1: 0}
    return pl.pallas_call(
        _final_kernel,
        grid=(bn, s_len // tm),
        in_specs=in_specs,
        out_specs=pl.BlockSpec((1, tm, d), lambda b, i: (b + out_b0, i, 0)),
        out_shape=jax.ShapeDtypeStruct((out_rows, s_len, d), F32),
        input_output_aliases=aliases,
        compiler_params=pltpu.CompilerParams(
            dimension_semantics=("parallel", "parallel"), vmem_limit_bytes=VMEM_LIMIT),
        name="moe_final",
    )(*args)


def _moe_schedule(eid, rank, counts):
    bn, _, s_len = eid.shape
    t = bn * s_len
    n_tiles = (t * TOP_K) // EXPERT_TILE + N_EXPERTS

    cnt = counts[:, 0].astype(I32)
    padded = (cnt + EXPERT_TILE - 1) // EXPERT_TILE * EXPERT_TILE
    ends = jnp.cumsum(padded)
    offs = ends - padded
    n_used = ends[-1] // EXPERT_TILE
    ti = jnp.minimum(jnp.arange(n_tiles, dtype=I32), n_used - 1)
    te = jnp.sum((ends[None, :] <= (ti * EXPERT_TILE)[:, None]).astype(I32), axis=1)

    slots = _slots(offs, eid, rank)
    slot_chunks = slots.reshape(bn, TOP_K, s_len // SC_CHUNK, SC_CHUNK).transpose(0, 2, 1, 3)
    slot_chunks = slot_chunks.reshape(t // SC_CHUNK, TOP_K, SC_CHUNK)
    return slot_chunks, ti, te, n_used.reshape(1), n_tiles * EXPERT_TILE


def _head_chunks(w, widths, n_heads, per_head):
    lead = w.shape[:-1]
    wh = w.reshape(lead + (n_heads, per_head))
    parts = [sign * wh[..., a:b] for a, b, sign in widths]
    used = sum(b - a for a, b, _ in widths)
    parts.append(jnp.zeros(lead + (n_heads, LANES - used), w.dtype))
    return jnp.concatenate(parts, axis=-1).reshape(lead + (n_heads * LANES,))


def kernel(x, mem, positions, w_in, w_conv, w_conv_out, q_norm, w_uq, kv_norm, w_uk, w_uv,
           w_attn_out, w_mix_out, ln1_g, ln1_b, w_xq, w_xk, w_xv, w_xo, ln2_g, ln2_b,
           w_router, router_bias, w_gate, w_up, w_down, ws_gate, ws_up, ws_down, ln3_g, ln3_b):
    bn, s_len, d = x.shape
    depth = w_in.shape[0]
    alpha = (2 * depth) ** 0.25
    ts = min(512, s_len)
    tq = min(512, s_len)
    half = QK_ROPE_DIM // 2
    nope, rope = QK_NOPE_DIM, QK_ROPE_DIM

    inv_freq = ROPE_BASE ** (-jnp.arange(half, dtype=F32) / half)
    ang = positions.astype(F32)[..., None] * inv_freq
    cos, sin = jnp.cos(ang), jnp.sin(ang)
    scale = (nope + rope) ** -0.5
    zeros = lambda w: jnp.zeros((bn, s_len, w), F32)
    tail = LANES - nope - rope
    tabs = jnp.concatenate([
        jnp.ones((bn, s_len, nope), F32), cos, cos, zeros(tail),
        zeros(nope), -sin, zeros(half), zeros(tail),
        zeros(nope), zeros(half), sin, zeros(tail)], axis=-1)

    sizes = (CONV_DIM, CONV_DIM, CONV_DIM, Q_RANK, KV_RANK, rope, d, d)
    offs = [0]
    for sz in sizes:
        offs.append(offs[-1] + sz)
    seg = lambda j: w_in[:, :, offs[j]:offs[j + 1]]
    zpad = lambda w: jnp.zeros(w_in.shape[:2] + (w,), w_in.dtype)
    wx = jnp.concatenate([seg(0), seg(1), seg(2), seg(3), seg(4), zpad(nope), seg(5), zpad(tail),
                          seg(6), seg(7)], axis=-1).astype(BF16)

    per_q = nope + rope
    wq = _head_chunks(w_uq, [(0, per_q, scale)], MLA_HEADS, per_q).astype(BF16)
    wk = _head_chunks(w_uk, [(0, nope, 1.0)], MLA_HEADS, nope)
    wkv = jnp.concatenate([wk, w_uv], axis=-1).astype(BF16)

    wconv = w_conv[:, :, 0, :]
    wco = w_conv_out.astype(BF16)
    qn = q_norm[:, None, :]
    kvn = kv_norm[:, None, :]
    wao = w_attn_out.astype(BF16)
    wmo = w_mix_out.astype(BF16)
    ln1 = jnp.stack([ln1_g, ln1_b], axis=1)
    ln2 = jnp.stack([ln2_g, ln2_b], axis=1)
    ln3 = jnp.stack([ln3_g, ln3_b], axis=1)
    wxq = w_xq.astype(BF16)
    wxo = w_xo.astype(BF16)
    wr = jnp.swapaxes(w_router, 1, 2)
    rb = router_bias[:, :, None]
    wsgu = jnp.concatenate([ws_gate, ws_up], axis=-1).astype(BF16)
    wsd = ws_down.astype(BF16)

    kmem, vmem = _mem_kv(mem, w_xk.astype(BF16), w_xv.astype(BF16))

    assert bn % 2 == 0
    hb = bn // 2
    t = hb * s_len
    tm = min(512, s_len)

    def mix_attn(l, c, xc, dep):
        x_b0 = c * hb if l == 0 else 0
        yag, sgb, q, k, v = _mixer_in(l, hb, x_b0, c * hb, xc, tabs, wx, wconv, wco, qn, wq, kvn, wkv,
                                      ts, dep)
        return yag, sgb, _attention(q, k, v, tq, yag)

    def post_route(l, c, xc, ma, dep):
        x_b0 = c * hb if l == 0 else 0
        yag, sgb, o = ma
        x2p, base, logits = _post(l, hb, x_b0, c * hb, xc, yag, sgb, o, kmem, vmem, wao, wmo, ln1,
                                  wxq, wxo, ln2, wr, wsgu, wsd, ts, alpha, dep)
        eid, rank, gate8, counts = _route_call(l, logits, rb)
        slot_chunks, ti, te, nu, n_slots = _moe_schedule(eid, rank, counts)
        return dict(x2p=x2p.reshape(t, d // 2), base=base, gate8=gate8, slot_chunks=slot_chunks,
                    ti=ti, te=te, nu=nu, n_slots=n_slots)

    def dispatch(st, dep):
        st["xs"] = _sc_dispatch(st["x2p"], st["slot_chunks"], st["n_slots"], dep)

    def experts(l, st, dep):
        st["ys"] = _experts(l, st["xs"], st["ti"], st["te"], st["nu"], w_gate, w_up, w_down, dep)
        st["yg"] = _sc_combine(st["ys"], st["slot_chunks"], t).reshape(TOP_K, hb, s_len, d // 2)

    def final(l, c, st, dep, out_prev):
        if l == depth - 1:
            return _final(l, st["yg"], st["gate8"], st["base"], ln3, tm, bn, c * hb, out_prev, dep)
        return _final(l, st["yg"], st["gate8"], st["base"], ln3, tm, hb, 0, None, dep)

    xa = xb = x
    ma_a = mix_attn(0, 0, xa, x)
    st_a = post_route(0, 0, xa, ma_a, ma_a[2])
    dispatch(st_a, st_a["gate8"])
    for l in range(depth):
        more = l + 1 < depth
        ma_b = mix_attn(l, 1, xb, st_a["gate8"])
        experts(l, st_a, ma_b[2])
        st_b = post_route(l, 1, xb, ma_b, st_a["ys"])
        xa = final(l, 0, st_a, st_b["gate8"], None)
        dispatch(st_b, xa if more else st_b["gate8"])
        if more:
            ma_a = mix_attn(l + 1, 0, xa, xa)
        experts(l, st_b, ma_a[2] if more else xa)
        if more:
            st_a = post_route(l + 1, 0, xa, ma_a, st_b["ys"])
        xb = final(l, 1, st_b, st_a["gate8"] if more else st_b["ys"], xa if not more else None)
        if more:
            dispatch(st_a, xb)
    return xb
```

```python
import functools

import jax
import jax.numpy as jnp
from jax import lax
from jax.experimental import pallas as pl
from jax.experimental.pallas import tpu as pltpu
from jax.experimental.pallas import tpu_sc as plsc

CONV_DIM = 512
CONV_WIDTH = 3
MLA_HEADS = 8
QK_NOPE_DIM = 64
QK_ROPE_DIM = 32
V_HEAD_DIM = 64
Q_RANK = 384
KV_RANK = 256
ROPE_BASE = 10000.0
XATTN_HEADS = 4
N_EXPERTS = 64
TOP_K = 8
N_GROUPS = 8
TOPK_GROUPS = 4
GROUP_SIZE = N_EXPERTS // N_GROUPS
ROUTE_SCALE = 2.5
LN_EPS = 1e-5
RMS_EPS = 1e-6

LANES = 128
HALO_ROWS = 8
VMEM_LIMIT = 56 * 1024 * 1024
SC_CORES = 2
SC_SUBCORES = 16
SC_CHUNK = 64
EXPERT_TILE = 1024

BF16 = jnp.bfloat16
F32 = jnp.float32
I32 = jnp.int32


_DEP_SPEC = pl.BlockSpec(memory_space=pl.ANY)


def _dot(a, b):
    return jnp.dot(a, b, preferred_element_type=F32)


def _dot_nt(a, b):
    return lax.dot_general(a, b, (((1,), (1,)), ((), ())), preferred_element_type=F32)


def _layer_norm(x, g, b):
    mu = jnp.mean(x, axis=-1, keepdims=True)
    xc = x - mu
    var = jnp.mean(xc * xc, axis=-1, keepdims=True)
    return xc * lax.rsqrt(var + LN_EPS) * g + b


def _rms_norm(x, g):
    ms = jnp.mean(x * x, axis=-1, keepdims=True)
    return x * lax.rsqrt(ms + RMS_EPS) * g


def _silu(x):
    return x * jax.nn.sigmoid(x)


_HI_MASK = -65536


def _pack_rows(y):
    w = y.shape[1] // 2
    lo = lax.bitcast_convert_type(y[:, :w].astype(BF16).astype(F32), I32)
    hi = lax.bitcast_convert_type(y[:, w:].astype(BF16).astype(F32), I32)
    return lax.shift_right_logical(lo, 16) | (hi & _HI_MASK)


def _unpack_rows(p):
    lo = lax.bitcast_convert_type(lax.shift_left(p, 16), F32)
    hi = lax.bitcast_convert_type(p & _HI_MASK, F32)
    return jnp.concatenate([lo, hi], axis=-1)


_C_BCH = 0
_C_CQ = _C_BCH + 3 * CONV_DIM
_C_CKV = _C_CQ + Q_RANK
_C_KR = _C_CKV + KV_RANK
_C_GA = _C_KR + LANES
_C_GB = _C_GA + 1024


def _mixer_in_kernel(x_ref, xh_ref, tab_ref, wx_ref, wconv_ref, wco_ref, qn_ref, wq_ref,
                     kvn_ref, wkv_ref, dep_ref,
                     yag_ref, sgb_ref, q_ref, k_ref, v_ref, *, d_model):
    i = pl.program_id(1)
    d = d_model
    ts = x_ref.shape[1]
    xb = x_ref[0].astype(BF16)

    bch = _dot(xb, wx_ref[:, _C_BCH:_C_BCH + 3 * CONV_DIM])
    b_gate = bch[:, :CONV_DIM]
    u = bch[:, CONV_DIM:2 * CONV_DIM] * bch[:, 2 * CONV_DIM:]
    xh = xh_ref[0].astype(BF16)
    hp = _dot(xh, wx_ref[:, _C_BCH + CONV_DIM:_C_BCH + 3 * CONV_DIM])
    uh = hp[:, :CONV_DIM] * hp[:, CONV_DIM:]
    uh = jnp.where(i == 0, 0.0, uh)
    row = lax.broadcasted_iota(jnp.int32, (ts, CONV_DIM), 0)
    u1 = jnp.where(row == 0, uh[HALO_ROWS - 1:HALO_ROWS], pltpu.roll(u, 1, 0))
    u2 = jnp.where(row == 0, uh[HALO_ROWS - 2:HALO_ROWS - 1],
                   jnp.where(row == 1, uh[HALO_ROWS - 1:HALO_ROWS], pltpu.roll(u, 2, 0)))
    wc = wconv_ref[0]
    z = wc[0:1] * u2 + wc[1:2] * u1 + wc[2:3] * u
    y_a = _dot((b_gate * z).astype(BF16), wco_ref[0])
    g_a = _dot(xb, wx_ref[:, _C_GA:_C_GA + d])
    yag_ref[0] = (jax.nn.sigmoid(g_a) * y_a).astype(BF16)
    g_b = _dot(xb, wx_ref[:, _C_GB:_C_GB + d])
    sgb_ref[0] = jax.nn.sigmoid(g_b).astype(BF16)

    tab = tab_ref[0]
    t_a, t_m, t_p = tab[:, 0:LANES], tab[:, LANES:2 * LANES], tab[:, 2 * LANES:3 * LANES]
    half = QK_ROPE_DIM // 2

    def rotate(c):
        return (c * t_a + pltpu.roll(c, LANES - half, 1) * t_m + pltpu.roll(c, half, 1) * t_p)

    c_q = _dot(xb, wx_ref[:, _C_CQ:_C_CQ + Q_RANK])
    cqn = _rms_norm(c_q, qn_ref[0]).astype(BF16)
    hw = MLA_HEADS * LANES
    q = _dot(cqn, wq_ref[0])
    for h in range(MLA_HEADS):
        sl = slice(h * LANES, (h + 1) * LANES)
        q_ref[0, :, sl] = rotate(q[:, sl]).astype(BF16)

    c_kv = _dot(xb, wx_ref[:, _C_CKV:_C_CKV + KV_RANK])
    ckvn = _rms_norm(c_kv, kvn_ref[0]).astype(BF16)
    kr_rot = rotate(_dot(xb, wx_ref[:, _C_KR:_C_KR + LANES]))
    k_nope = _dot(ckvn, wkv_ref[0, :, :hw])
    for h in range(MLA_HEADS):
        sl = slice(h * LANES, (h + 1) * LANES)
        k_ref[0, :, sl] = (k_nope[:, sl] + kr_rot).astype(BF16)
    v_ref[0] = _dot(ckvn, wkv_ref[0, :, hw:]).astype(BF16)


def _mixer_in(l, bn, x_b0, tab_b0, x, tabs, wx, wconv, wco, qn, wq, kvn, wkv, ts, dep):
    _, s_len, d = x.shape
    n_t = s_len // ts
    hw = MLA_HEADS * LANES
    vw = MLA_HEADS * V_HEAD_DIM
    tile = lambda w: pl.BlockSpec((1, ts, w), lambda b, i: (b, i, 0))
    lay = lambda a: pl.BlockSpec((1,) + a.shape[1:], lambda b, i: (l,) + (0,) * (a.ndim - 1))
    halo = pl.BlockSpec((1, HALO_ROWS, d),
                        lambda b, i: (b + x_b0, jnp.maximum(i * (ts // HALO_ROWS) - 1, 0), 0))
    return pl.pallas_call(
        functools.partial(_mixer_in_kernel, d_model=d),
        grid=(bn, n_t),
        in_specs=[pl.BlockSpec((1, ts, d), lambda b, i: (b + x_b0, i, 0)), halo,
                  pl.BlockSpec((1, ts, 3 * LANES), lambda b, i: (b + tab_b0, i, 0)),
                  pl.BlockSpec((None,) + wx.shape[1:], lambda b, i: (l, 0, 0)),
                  lay(wconv), lay(wco), lay(qn), lay(wq), lay(kvn), lay(wkv), _DEP_SPEC],
        out_specs=[tile(d), tile(d), tile(hw), tile(hw), tile(vw)],
        out_shape=[jax.ShapeDtypeStruct((bn, s_len, d), BF16),
                   jax.ShapeDtypeStruct((bn, s_len, d), BF16),
                   jax.ShapeDtypeStruct((bn, s_len, hw), BF16),
                   jax.ShapeDtypeStruct((bn, s_len, hw), BF16),
                   jax.ShapeDtypeStruct((bn, s_len, vw), BF16)],
        compiler_params=pltpu.CompilerParams(
            dimension_semantics=("parallel", "arbitrary"), vmem_limit_bytes=VMEM_LIMIT),
        name="mixer_in",
    )(x, x, tabs, wx, wconv, wco, qn, wq, kvn, wkv, dep)


def _attn_kernel(q_ref, k_ref, v_ref, dep_ref, o_ref, *, tq):
    s_len = q_ref.shape[1]
    n_q = s_len // tq
    lane = lax.broadcasted_iota(jnp.int32, (tq, 2 * V_HEAD_DIM), 1)
    for qi in range(n_q):
        kv_len = (qi + 1) * tq
        rows = slice(qi * tq, (qi + 1) * tq)
        v2 = v_ref[0, :kv_len, :]
        r = lax.broadcasted_iota(jnp.int32, (tq, kv_len), 0) + qi * tq
        c = lax.broadcasted_iota(jnp.int32, (tq, kv_len), 1)
        outs = []
        for h in range(2):
            sl = slice(h * LANES, (h + 1) * LANES)
            s = _dot_nt(q_ref[0, rows, sl], k_ref[0, :kv_len, sl])
            s = jnp.where(c <= r, s, -jnp.inf)
            m = jnp.max(s, axis=-1, keepdims=True)
            p = jnp.exp(s - m)
            den = jnp.sum(p, axis=-1, keepdims=True)
            outs.append(_dot(p.astype(BF16), v2) / den)
        o_ref[0, rows, :] = jnp.where(lane < V_HEAD_DIM, outs[0], outs[1]).astype(BF16)


def _attention(q, k, v, tq, dep):
    bn, s_len, _ = q.shape
    n_pairs = MLA_HEADS // 2
    return pl.pallas_call(
        functools.partial(_attn_kernel, tq=tq),
        grid=(bn, n_pairs),
        in_specs=[pl.BlockSpec((1, s_len, 2 * LANES), lambda b, h: (b, 0, h)),
                  pl.BlockSpec((1, s_len, 2 * LANES), lambda b, h: (b, 0, h)),
                  pl.BlockSpec((1, s_len, 2 * V_HEAD_DIM), lambda b, h: (b, 0, h)), _DEP_SPEC],
        out_specs=pl.BlockSpec((1, s_len, 2 * V_HEAD_DIM), lambda b, h: (b, 0, h)),
        out_shape=jax.ShapeDtypeStruct((bn, s_len, MLA_HEADS * V_HEAD_DIM), BF16),
        compiler_params=pltpu.CompilerParams(
            dimension_semantics=("parallel", "parallel"), vmem_limit_bytes=VMEM_LIMIT),
        name="mla_attention",
    )(q, k, v, dep)


def _mem_kv_kernel(mem_ref, wk_ref, wv_ref, k_ref, v_ref):
    mb = mem_ref[0].astype(BF16)
    k_ref[0, 0] = _dot(mb, wk_ref[0]).astype(BF16)
    v_ref[0, 0] = _dot(mb, wv_ref[0]).astype(BF16)


def _mem_kv(mem, w_xk, w_xv):
    bn, m_len, d = mem.shape
    n_l = w_xk.shape[0]
    wspec = pl.BlockSpec((1, d, d), lambda l, b: (l, 0, 0))
    ospec = pl.BlockSpec((1, 1, m_len, d), lambda l, b: (l, b, 0, 0))
    return pl.pallas_call(
        _mem_kv_kernel,
        grid=(n_l, bn),
        in_specs=[pl.BlockSpec((1, m_len, d), lambda l, b: (b, 0, 0)), wspec, wspec],
        out_specs=[ospec, ospec],
        out_shape=[jax.ShapeDtypeStruct((n_l, bn, m_len, d), BF16)] * 2,
        compiler_params=pltpu.CompilerParams(
            dimension_semantics=("arbitrary", "arbitrary"), vmem_limit_bytes=VMEM_LIMIT),
        name="mem_kv",
    )(mem, w_xk, w_xv)


def _first_argmax(vals, idx):
    m = jnp.max(functools.reduce(jnp.maximum, vals), axis=0, keepdims=True)
    big = jnp.int32(1 << 20)
    cand = functools.reduce(jnp.minimum, [jnp.where(v == m, ix, big) for v, ix in zip(vals, idx)])
    first = jnp.min(cand, axis=0, keepdims=True)
    return [ix == first for ix in idx], m, first


def _route(logits_t, bias):
    n = logits_t.shape[1]
    scores = jax.nn.sigmoid(logits_t)
    sel = scores + bias
    sub = lax.broadcasted_iota(jnp.int32, (GROUP_SIZE, n), 0)
    neg = -jnp.inf
    sel_g = [sel[g * GROUP_SIZE:(g + 1) * GROUP_SIZE] for g in range(N_GROUPS)]
    gs = []
    for g in range(N_GROUPS):
        (hot,), m1, _ = _first_argmax([sel_g[g]], [sub])
        m2 = jnp.max(jnp.where(hot, neg, sel_g[g]), axis=0, keepdims=True)
        gs.append(m1 + m2)
    gsv = jnp.concatenate(gs, axis=0)
    gmask = jnp.zeros(gsv.shape, F32)
    for _ in range(TOPK_GROUPS):
        (hot,), _m, _ = _first_argmax([gsv], [sub])
        gmask = jnp.where(hot, 1.0, gmask)
        gsv = jnp.where(hot, neg, gsv)
    msel = [jnp.where(gmask[g:g + 1] > 0.0, sel_g[g], neg) for g in range(N_GROUPS)]
    eidx = [sub + g * GROUP_SIZE for g in range(N_GROUPS)]
    picks, ids = [], []
    for _ in range(TOP_K):
        hots, _m, first = _first_argmax(msel, eidx)
        picks.append(hots)
        ids.append(first)
        msel = [jnp.where(h, neg, v) for h, v in zip(hots, msel)]
    chosen = [functools.reduce(jnp.logical_or, [p[g] for p in picks]) for g in range(N_GROUPS)]
    sc_g = [scores[g * GROUP_SIZE:(g + 1) * GROUP_SIZE] for g in range(N_GROUPS)]
    picked = [jnp.where(c, s, 0.0) for c, s in zip(chosen, sc_g)]
    wsum = jnp.sum(functools.reduce(lambda a, b: a + b, picked), axis=0, keepdims=True)
    gates = [p / wsum * ROUTE_SCALE for p in picked]
    return picks, ids, chosen, gates


def _pick_rows(hots, vals):
    acc = functools.reduce(lambda a, b: a + b, [jnp.where(h, v, 0.0) for h, v in zip(hots, vals)])
    return jnp.sum(acc, axis=0, keepdims=True)


def _post_kernel(x_ref, yag_ref, sgb_ref, o_ref, km_ref, vm_ref, wao_ref, wmo_ref, ln1_ref,
                 wxq_ref, wxo_ref, ln2_ref, wr_ref, wsgu_ref, wsd_ref, dep_ref,
                 x2p_ref, base_ref, logit_ref, *, alpha):
    d = x_ref.shape[2]
    x = x_ref[0]
    y_b = _dot(o_ref[0], wao_ref[0])
    y = yag_ref[0].astype(F32) + sgb_ref[0].astype(F32) * y_b
    mix = _dot(y.astype(BF16), wmo_ref[0])
    x1 = _layer_norm(alpha * x + mix, ln1_ref[0, 0:1], ln1_ref[0, 1:2])

    hd = d // XATTN_HEADS
    xq = (_dot(x1.astype(BF16), wxq_ref[0]) * (hd ** -0.5)).astype(BF16)
    heads = []
    for h in range(XATTN_HEADS):
        sl = slice(h * hd, (h + 1) * hd)
        s = _dot_nt(xq[:, sl], km_ref[0, 0, :, sl])
        m = jnp.max(s, axis=-1, keepdims=True)
        p = jnp.exp(s - m)
        den = jnp.sum(p, axis=-1, keepdims=True)
        heads.append((_dot(p.astype(BF16), vm_ref[0, 0, :, sl]) / den).astype(BF16))
    xat = _dot(jnp.concatenate(heads, axis=-1), wxo_ref[0])
    x2 = _layer_norm(alpha * x1 + xat, ln2_ref[0, 0:1], ln2_ref[0, 1:2])

    x2b = x2.astype(BF16)
    x2l = (x2 - x2b.astype(F32)).astype(BF16)
    wr = wr_ref[0]
    wrh = wr.astype(BF16)
    wrl = (wr - wrh.astype(F32)).astype(BF16)
    logit_ref[0] = _dot_nt(wrh, x2b) + (_dot_nt(wrh, x2l) + _dot_nt(wrl, x2b))

    x2p_ref[0] = _pack_rows(x2)
    sh = _dot(x2b, wsgu_ref[0])
    f = sh.shape[1] // 2
    hid = _silu(sh[:, :f]) * sh[:, f:]
    base_ref[0] = alpha * x2 + _dot(hid.astype(BF16), wsd_ref[0])


def _post(l, bn, x_b0, mem_b0, x, yag, sgb, o, kmem, vmem, wao, wmo, ln1, wxq, wxo, ln2, wr,
          wsgu, wsd, ts, alpha, dep):
    _, s_len, d = x.shape
    m_len = kmem.shape[2]
    tile = lambda w: pl.BlockSpec((1, ts, w), lambda b, i: (b, i, 0))
    lay = lambda a: pl.BlockSpec((1,) + a.shape[1:], lambda b, i: (l,) + (0,) * (a.ndim - 1))
    memspec = pl.BlockSpec((1, 1, m_len, d), lambda b, i: (l, b + mem_b0, 0, 0))
    return pl.pallas_call(
        functools.partial(_post_kernel, alpha=alpha),
        grid=(bn, s_len // ts),
        in_specs=[pl.BlockSpec((1, ts, d), lambda b, i: (b + x_b0, i, 0)),
                  tile(d), tile(d), tile(o.shape[2]), memspec, memspec,
                  lay(wao), lay(wmo), lay(ln1), lay(wxq), lay(wxo), lay(ln2), lay(wr),
                  lay(wsgu), lay(wsd), _DEP_SPEC],
        out_specs=[tile(d // 2), tile(d),
                   pl.BlockSpec((1, N_EXPERTS, ts), lambda b, i: (b, 0, i))],
        out_shape=[jax.ShapeDtypeStruct((bn, s_len, d // 2), I32),
                   jax.ShapeDtypeStruct((bn, s_len, d), F32),
                   jax.ShapeDtypeStruct((bn, N_EXPERTS, s_len), F32)],
        compiler_params=pltpu.CompilerParams(
            dimension_semantics=("parallel", "arbitrary"), vmem_limit_bytes=VMEM_LIMIT),
        name="post_mixer",
    )(x, yag, sgb, o, kmem, vmem, wao, wmo, ln1, wxq, wxo, ln2, wr, wsgu, wsd, dep)


RANK_BLOCK = 512


def _route_kernel(lg_ref, rb_ref, eid_ref, rank_ref, gate_ref, cnt_ref):
    @pl.when(pl.program_id(0) == 0)
    def _():
        cnt_ref[...] = jnp.zeros_like(cnt_ref)

    n = lg_ref.shape[2]
    blk = min(RANK_BLOCK, n)
    picks, ids, chosen, gates = _route(lg_ref[0], rb_ref[0])

    sel_t = jnp.concatenate([c.astype(F32) for c in chosen], axis=0)
    before = (lax.broadcasted_iota(jnp.int32, (blk, blk), 0)
              < lax.broadcasted_iota(jnp.int32, (blk, blk), 1)).astype(BF16)
    running = cnt_ref[:, 0:1]
    ranks = []
    for j in range(n // blk):
        sb = sel_t[:, j * blk:(j + 1) * blk]
        ranks.append(_dot(sb.astype(BF16), before) + running)
        running = running + jnp.sum(sb, axis=1, keepdims=True)
    cnt_ref[...] = jnp.broadcast_to(running, cnt_ref.shape)
    rank_all = jnp.concatenate(ranks, axis=1)
    rank_g = [rank_all[g * GROUP_SIZE:(g + 1) * GROUP_SIZE] for g in range(N_GROUPS)]
    eid_ref[0] = jnp.concatenate(ids, axis=0)
    rank_ref[0] = jnp.concatenate([_pick_rows(h, rank_g) for h in picks], axis=0).astype(I32)
    gate_ref[0] = jnp.concatenate([_pick_rows(h, gates) for h in picks], axis=0)


def _route_call(l, logits, rb):
    bn, n_e, s_len = logits.shape
    pick = pl.BlockSpec((1, TOP_K, s_len), lambda b: (b, 0, 0))
    return pl.pallas_call(
        _route_kernel,
        grid=(bn,),
        in_specs=[pl.BlockSpec((1, n_e, s_len), lambda b: (b, 0, 0)),
                  pl.BlockSpec((1,) + rb.shape[1:], lambda b: (l, 0, 0))],
        out_specs=[pick, pick, pick, pl.BlockSpec((n_e, LANES), lambda b: (0, 0))],
        out_shape=[jax.ShapeDtypeStruct((bn, TOP_K, s_len), I32),
                   jax.ShapeDtypeStruct((bn, TOP_K, s_len), I32),
                   jax.ShapeDtypeStruct((bn, TOP_K, s_len), F32),
                   jax.ShapeDtypeStruct((n_e, LANES), F32)],
        compiler_params=pltpu.CompilerParams(
            dimension_semantics=("arbitrary",), vmem_limit_bytes=VMEM_LIMIT),
        name="moe_route",
    )(logits, rb)


def _slot_kernel(offs_ref, eid_ref, rank_ref, slot_ref):
    eid = eid_ref[0]
    acc = rank_ref[0]
    for e in range(N_EXPERTS):
        acc = acc + jnp.where(eid == e, offs_ref[e], 0)
    slot_ref[0] = acc


def _slots(offs, eid, rank):
    bn, k, s_len = eid.shape
    spec = pl.BlockSpec((1, k, s_len), lambda b, offs_ref: (b, 0, 0))
    return pl.pallas_call(
        _slot_kernel,
        grid_spec=pltpu.PrefetchScalarGridSpec(
            num_scalar_prefetch=1, grid=(bn,), in_specs=[spec, spec], out_specs=spec),
        out_shape=jax.ShapeDtypeStruct((bn, k, s_len), I32),
        name="moe_slots",
    )(offs, eid, rank)


def _sc_mesh():
    return plsc.VectorSubcoreMesh(core_axis_name="c", subcore_axis_name="s")


def _sc_dispatch(xp, slot_chunks, n_slots, dep):
    t, w = xp.shape
    n_chunks, k, ch = slot_chunks.shape
    per_worker = n_chunks // (SC_CORES * SC_SUBCORES)

    @functools.partial(
        pl.kernel, mesh=_sc_mesh(),
        out_type=jax.ShapeDtypeStruct((n_slots, w), I32),
        scratch_types=[pltpu.VMEM((k, ch), I32), pltpu.VMEM((ch, w), I32), pltpu.SemaphoreType.DMA],
        name="moe_dispatch")
    def body(x_hbm, slot_hbm, dep_hbm, out_hbm, idx_v, rows_v, sem):
        worker = lax.axis_index("s") * SC_CORES + lax.axis_index("c")

        @pl.loop(0, per_worker)
        def _(j):
            c = worker * per_worker + j
            pltpu.sync_copy(slot_hbm.at[c], idx_v)
            pltpu.sync_copy(x_hbm.at[pl.ds(c * ch, ch)], rows_v)
            copies = [pltpu.async_copy(rows_v, out_hbm.at[idx_v.at[kk]], sem) for kk in range(k)]
            for cp in copies:
                cp.wait()

    return body(xp, slot_chunks, dep)


def _sc_combine(ys, slot_chunks, t):
    w = ys.shape[1]
    n_chunks, k, ch = slot_chunks.shape
    per_worker = n_chunks // (SC_CORES * SC_SUBCORES)

    @functools.partial(
        pl.kernel, mesh=_sc_mesh(),
        out_type=jax.ShapeDtypeStruct((k, t, w), I32),
        scratch_types=[pltpu.VMEM((k, ch), I32), pltpu.VMEM((ch, w), I32), pltpu.VMEM((ch, w), I32),
                       pltpu.SemaphoreType.DMA, pltpu.SemaphoreType.DMA,
                       pltpu.SemaphoreType.DMA, pltpu.SemaphoreType.DMA],
        name="moe_combine")
    def body(y_hbm, slot_hbm, out_hbm, idx_v, rows0, rows1, g0, g1, w0, w1):
        worker = lax.axis_index("s") * SC_CORES + lax.axis_index("c")
        bufs, gsem, wsem = (rows0, rows1), (g0, g1), (w0, w1)

        @pl.loop(0, per_worker)
        def _(j):
            c = worker * per_worker + j
            pltpu.sync_copy(slot_hbm.at[c], idx_v)
            gathers = [None] * k
            writes = [None] * k
            gathers[0] = pltpu.async_copy(y_hbm.at[idx_v.at[0]], bufs[0], gsem[0])
            for kk in range(k):
                cur = kk % 2
                if kk + 1 < k:
                    if kk >= 1:
                        writes[kk - 1].wait()
                    gathers[kk + 1] = pltpu.async_copy(
                        y_hbm.at[idx_v.at[kk + 1]], bufs[1 - cur], gsem[1 - cur])
                gathers[kk].wait()
                writes[kk] = pltpu.async_copy(
                    bufs[cur], out_hbm.at[kk, pl.ds(c * ch, ch)], wsem[cur])
            writes[k - 2].wait()
            writes[k - 1].wait()

    return body(ys, slot_chunks)


def _experts_kernel(ti_ref, te_ref, nv_ref, nu_ref, xs_ref, wg_ref, wu_ref, wd_ref, dep_ref, ys_ref,
                    wg_b, wu_b, wd_b):
    j = pl.program_id(0)
    prev = te_ref[jnp.maximum(j - 1, 0)]
    new_expert = jnp.logical_or(j == 0, te_ref[j] != prev)

    @pl.when(new_expert)
    def _():
        wg_b[...] = wg_ref[0, 0].astype(BF16)
        wu_b[...] = wu_ref[0, 0].astype(BF16)
        wd_b[...] = wd_ref[0, 0].astype(BF16)

    def ffn(n_rows):
        rows = pl.ds(0, n_rows)
        xb = _unpack_rows(xs_ref[rows, :]).astype(BF16)
        hid = _silu(_dot(xb, wg_b[...])) * _dot(xb, wu_b[...])
        ys_ref[rows, :] = _pack_rows(_dot(hid.astype(BF16), wd_b[...]))

    used = j < nu_ref[0]
    half = xs_ref.shape[0] // 2
    many = nv_ref[j] > half

    @pl.when(jnp.logical_and(used, many))
    def _():
        ffn(2 * half)

    @pl.when(jnp.logical_and(used, jnp.logical_not(many)))
    def _():
        ffn(half)


def _experts(l, xs, ti, te, nv, nu, w_gate, w_up, w_down, dep):
    n_slots, w = xs.shape
    n_tiles = n_slots // EXPERT_TILE
    d, f = w_gate.shape[2:]
    rows = pl.BlockSpec((EXPERT_TILE, w), lambda j, ti, te, nv, nu: (ti[j], 0))
    wspec = lambda a, b: pl.BlockSpec((1, 1, a, b), lambda j, ti, te, nv, nu: (l, te[j], 0, 0))
    return pl.pallas_call(
        _experts_kernel,
        grid_spec=pltpu.PrefetchScalarGridSpec(
            num_scalar_prefetch=4, grid=(n_tiles,),
            in_specs=[rows, wspec(d, f), wspec(d, f), wspec(f, d), _DEP_SPEC],
            out_specs=rows,
            scratch_shapes=[pltpu.VMEM((d, f), BF16), pltpu.VMEM((d, f), BF16),
                            pltpu.VMEM((f, d), BF16)]),
        out_shape=jax.ShapeDtypeStruct((n_slots, w), I32),
        compiler_params=pltpu.CompilerParams(
            dimension_semantics=("arbitrary",), vmem_limit_bytes=VMEM_LIMIT),
        name="moe_experts",
    )(ti, te, nv, nu, xs, w_gate, w_up, w_down, dep)


def _final_kernel(yg_ref, gate_ref, base_ref, ln_ref, *rest):
    out_ref = rest[-1]
    gates = gate_ref[0].T
    acc = base_ref[0]
    for k in range(yg_ref.shape[0]):
        acc = acc + gates[:, k:k + 1] * _unpack_rows(yg_ref[k, 0])
    out_ref[0] = _layer_norm(acc, ln_ref[0, 0:1], ln_ref[0, 1:2])


def _final(l, yg, gate8, base, ln3, tm, out_rows, out_b0, out_prev, dep):
    bn, s_len, d = base.shape
    k = yg.shape[0]
    in_specs = [pl.BlockSpec((k, 1, tm, d // 2), lambda b, i: (0, b, i, 0)),
                pl.BlockSpec((1, k, tm), lambda b, i: (b, 0, i)),
                pl.BlockSpec((1, tm, d), lambda b, i: (b, i, 0)),
                pl.BlockSpec((1,) + ln3.shape[1:], lambda b, i: (l, 0, 0)), _DEP_SPEC]
    args = [yg, gate8, base, ln3, dep]
    aliases = {}
    if out_prev is not None:
        in_specs.append(pl.BlockSpec(memory_space=pl.ANY))
        args.append(out_prev)
        aliases = {len(args) - 1: 0}
    return pl.pallas_call(
        _final_kernel,
        grid=(bn, s_len // tm),
        in_specs=in_specs,
        out_specs=pl.BlockSpec((1, tm, d), lambda b, i: (b + out_b0, i, 0)),
        out_shape=jax.ShapeDtypeStruct((out_rows, s_len, d), F32),
        input_output_aliases=aliases,
        compiler_params=pltpu.CompilerParams(
            dimension_semantics=("parallel", "parallel"), vmem_limit_bytes=VMEM_LIMIT),
        name="moe_final",
    )(*args)


def _moe_schedule(eid, rank, counts):
    bn, _, s_len = eid.shape
    t = bn * s_len
    n_tiles = (t * TOP_K) // EXPERT_TILE + N_EXPERTS

    cnt = counts[:, 0].astype(I32)
    padded = (cnt + EXPERT_TILE - 1) // EXPERT_TILE * EXPERT_TILE
    ends = jnp.cumsum(padded)
    offs = ends - padded
    n_used = ends[-1] // EXPERT_TILE
    ti = jnp.minimum(jnp.arange(n_tiles, dtype=I32), n_used - 1)
    te = jnp.sum((ends[None, :] <= (ti * EXPERT_TILE)[:, None]).astype(I32), axis=1)
    nv = jnp.take(cnt + offs, te) - ti * EXPERT_TILE

    slots = _slots(offs, eid, rank)
    slot_chunks = slots.reshape(bn, TOP_K, s_len // SC_CHUNK, SC_CHUNK).transpose(0, 2, 1, 3)
    slot_chunks = slot_chunks.reshape(t // SC_CHUNK, TOP_K, SC_CHUNK)
    return slot_chunks, ti, te, nv, n_used.reshape(1), n_tiles * EXPERT_TILE


def _head_chunks(w, widths, n_heads, per_head):
    lead = w.shape[:-1]
    wh = w.reshape(lead + (n_heads, per_head))
    parts = [sign * wh[..., a:b] for a, b, sign in widths]
    used = sum(b - a for a, b, _ in widths)
    parts.append(jnp.zeros(lead + (n_heads, LANES - used), w.dtype))
    return jnp.concatenate(parts, axis=-1).reshape(lead + (n_heads * LANES,))


def kernel(x, mem, positions, w_in, w_conv, w_conv_out, q_norm, w_uq, kv_norm, w_uk, w_uv,
           w_attn_out, w_mix_out, ln1_g, ln1_b, w_xq, w_xk, w_xv, w_xo, ln2_g, ln2_b,
           w_router, router_bias, w_gate, w_up, w_down, ws_gate, ws_up, ws_down, ln3_g, ln3_b):
    bn, s_len, d = x.shape
    depth = w_in.shape[0]
    alpha = (2 * depth) ** 0.25
    ts = min(512, s_len)
    tq = min(512, s_len)
    half = QK_ROPE_DIM // 2
    nope, rope = QK_NOPE_DIM, QK_ROPE_DIM

    inv_freq = ROPE_BASE ** (-jnp.arange(half, dtype=F32) / half)
    ang = positions.astype(F32)[..., None] * inv_freq
    cos, sin = jnp.cos(ang), jnp.sin(ang)
    scale = (nope + rope) ** -0.5
    zeros = lambda w: jnp.zeros((bn, s_len, w), F32)
    tail = LANES - nope - rope
    tabs = jnp.concatenate([
        jnp.ones((bn, s_len, nope), F32), cos, cos, zeros(tail),
        zeros(nope), -sin, zeros(half), zeros(tail),
        zeros(nope), zeros(half), sin, zeros(tail)], axis=-1)

    sizes = (CONV_DIM, CONV_DIM, CONV_DIM, Q_RANK, KV_RANK, rope, d, d)
    offs = [0]
    for sz in sizes:
        offs.append(offs[-1] + sz)
    seg = lambda j: w_in[:, :, offs[j]:offs[j + 1]]
    zpad = lambda w: jnp.zeros(w_in.shape[:2] + (w,), w_in.dtype)
    wx = jnp.concatenate([w_in[:, :, :offs[5]], zpad(nope), seg(5), zpad(tail), w_in[:, :, offs[6]:]],
                         axis=-1).astype(BF16)

    per_q = nope + rope
    wq = _head_chunks(w_uq, [(0, per_q, scale)], MLA_HEADS, per_q).astype(BF16)
    wk = _head_chunks(w_uk, [(0, nope, 1.0)], MLA_HEADS, nope)
    wkv = jnp.concatenate([wk, w_uv], axis=-1).astype(BF16)

    wconv = w_conv[:, :, 0, :]
    wco = w_conv_out.astype(BF16)
    qn = q_norm[:, None, :]
    kvn = kv_norm[:, None, :]
    wao = w_attn_out.astype(BF16)
    wmo = w_mix_out.astype(BF16)
    ln1 = jnp.stack([ln1_g, ln1_b], axis=1)
    ln2 = jnp.stack([ln2_g, ln2_b], axis=1)
    ln3 = jnp.stack([ln3_g, ln3_b], axis=1)
    wxq = w_xq.astype(BF16)
    wxo = w_xo.astype(BF16)
    wr = jnp.swapaxes(w_router, 1, 2)
    rb = router_bias[:, :, None]
    wsgu = jnp.concatenate([ws_gate, ws_up], axis=-1).astype(BF16)
    wsd = ws_down.astype(BF16)

    kmem, vmem = _mem_kv(mem, w_xk.astype(BF16), w_xv.astype(BF16))

    assert bn % 2 == 0
    hb = bn // 2
    t = hb * s_len
    tm = min(512, s_len)

    def mix_attn(l, c, xc, dep):
        x_b0 = c * hb if l == 0 else 0
        yag, sgb, q, k, v = _mixer_in(l, hb, x_b0, c * hb, xc, tabs, wx, wconv, wco, qn, wq, kvn, wkv,
                                      ts, dep)
        return yag, sgb, _attention(q, k, v, tq, yag)

    def post_route(l, c, xc, ma, dep):
        x_b0 = c * hb if l == 0 else 0
        yag, sgb, o = ma
        x2p, base, logits = _post(l, hb, x_b0, c * hb, xc, yag, sgb, o, kmem, vmem, wao, wmo, ln1,
                                  wxq, wxo, ln2, wr, wsgu, wsd, ts, alpha, dep)
        eid, rank, gate8, counts = _route_call(l, logits, rb)
        slot_chunks, ti, te, nv, nu, n_slots = _moe_schedule(eid, rank, counts)
        return dict(x2p=x2p.reshape(t, d // 2), base=base, gate8=gate8, slot_chunks=slot_chunks,
                    ti=ti, te=te, nv=nv, nu=nu, n_slots=n_slots)

    def dispatch(st, dep):
        st["xs"] = _sc_dispatch(st["x2p"], st["slot_chunks"], st["n_slots"], dep)

    def experts(l, st, dep):
        st["ys"] = _experts(l, st["xs"], st["ti"], st["te"], st["nv"], st["nu"], w_gate, w_up, w_down,
                            dep)
        st["yg"] = _sc_combine(st["ys"], st["slot_chunks"], t).reshape(TOP_K, hb, s_len, d // 2)

    def final(l, c, st, dep, out_prev):
        if l == depth - 1:
            return _final(l, st["yg"], st["gate8"], st["base"], ln3, tm, bn, c * hb, out_prev, dep)
        return _final(l, st["yg"], st["gate8"], st["base"], ln3, tm, hb, 0, None, dep)

    xa = xb = x
    ma_a = mix_attn(0, 0, xa, x)
    st_a = post_route(0, 0, xa, ma_a, ma_a[2])
    dispatch(st_a, st_a["gate8"])
    for l in range(depth):
        more = l + 1 < depth
        ma_b = mix_attn(l, 1, xb, st_a["gate8"])
        experts(l, st_a, ma_b[2])
        st_b = post_route(l, 1, xb, ma_b, st_a["ys"])
        xa = final(l, 0, st_a, st_b["gate8"], None)
        dispatch(st_b, xa if more else st_b["gate8"])
        if more:
            ma_a = mix_attn(l + 1, 0, xa, xa)
        experts(l, st_b, ma_a[2] if more else xa)
        if more:
            st_a = post_route(l + 1, 0, xa, ma_a, st_b["ys"])
        xb = final(l, 1, st_b, st_a["gate8"] if more else st_b["ys"], xa if not more else None)
        if more:
            dispatch(st_a, xb)
    return xb
```

```python
import functools

import jax
import jax.numpy as jnp
from jax import lax
from jax.experimental import pallas as pl
from jax.experimental.pallas import tpu as pltpu
from jax.experimental.pallas import tpu_sc as plsc

CONV_DIM = 512
CONV_WIDTH = 3
MLA_HEADS = 8
QK_NOPE_DIM = 64
QK_ROPE_DIM = 32
V_HEAD_DIM = 64
Q_RANK = 384
KV_RANK = 256
ROPE_BASE = 10000.0
XATTN_HEADS = 4
N_EXPERTS = 64
TOP_K = 8
N_GROUPS = 8
TOPK_GROUPS = 4
GROUP_SIZE = N_EXPERTS // N_GROUPS
ROUTE_SCALE = 2.5
LN_EPS = 1e-5
RMS_EPS = 1e-6

LANES = 128
HALO_ROWS = 8
VMEM_LIMIT = 56 * 1024 * 1024
SC_CORES = 2
SC_SUBCORES = 16
SC_CHUNK = 64
EXPERT_TILE = 1024

BF16 = jnp.bfloat16
F32 = jnp.float32
I32 = jnp.int32


_DEP_SPEC = pl.BlockSpec(memory_space=pl.ANY)


def _dot(a, b):
    return jnp.dot(a, b, preferred_element_type=F32)


def _dot_nt(a, b):
    return lax.dot_general(a, b, (((1,), (1,)), ((), ())), preferred_element_type=F32)


def _layer_norm(x, g, b):
    mu = jnp.mean(x, axis=-1, keepdims=True)
    xc = x - mu
    var = jnp.mean(xc * xc, axis=-1, keepdims=True)
    return xc * lax.rsqrt(var + LN_EPS) * g + b


def _rms_norm(x, g):
    ms = jnp.mean(x * x, axis=-1, keepdims=True)
    return x * lax.rsqrt(ms + RMS_EPS) * g


def _silu(x):
    return x * jax.nn.sigmoid(x)


_HI_MASK = -65536


def _pack_rows(y):
    w = y.shape[1] // 2
    lo = lax.bitcast_convert_type(y[:, :w].astype(BF16).astype(F32), I32)
    hi = lax.bitcast_convert_type(y[:, w:].astype(BF16).astype(F32), I32)
    return lax.shift_right_logical(lo, 16) | (hi & _HI_MASK)


def _unpack_rows(p):
    lo = lax.bitcast_convert_type(lax.shift_left(p, 16), F32)
    hi = lax.bitcast_convert_type(p & _HI_MASK, F32)
    return jnp.concatenate([lo, hi], axis=-1)


_C_BCH = 0
_C_CQ = _C_BCH + 3 * CONV_DIM
_C_CKV = _C_CQ + Q_RANK
_C_KR = _C_CKV + KV_RANK
_C_GA = _C_KR + LANES
_C_GB = _C_GA + 1024


def _mixer_in_kernel(x_ref, xh_ref, tab_ref, wx_ref, wconv_ref, wco_ref, qn_ref, wq_ref,
                     kvn_ref, wkv_ref, dep_ref,
                     yag_ref, sgb_ref, q_ref, k_ref, v_ref, *, d_model):
    i = pl.program_id(1)
    d = d_model
    ts = x_ref.shape[1]
    xb = x_ref[0].astype(BF16)

    bch = _dot(xb, wx_ref[:, _C_BCH:_C_BCH + 3 * CONV_DIM])
    b_gate = bch[:, :CONV_DIM]
    u = bch[:, CONV_DIM:2 * CONV_DIM] * bch[:, 2 * CONV_DIM:]
    xh = xh_ref[0].astype(BF16)
    hp = _dot(xh, wx_ref[:, _C_BCH + CONV_DIM:_C_BCH + 3 * CONV_DIM])
    uh = hp[:, :CONV_DIM] * hp[:, CONV_DIM:]
    uh = jnp.where(i == 0, 0.0, uh)
    row = lax.broadcasted_iota(jnp.int32, (ts, CONV_DIM), 0)
    u1 = jnp.where(row == 0, uh[HALO_ROWS - 1:HALO_ROWS], pltpu.roll(u, 1, 0))
    u2 = jnp.where(row == 0, uh[HALO_ROWS - 2:HALO_ROWS - 1],
                   jnp.where(row == 1, uh[HALO_ROWS - 1:HALO_ROWS], pltpu.roll(u, 2, 0)))
    wc = wconv_ref[0]
    z = wc[0:1] * u2 + wc[1:2] * u1 + wc[2:3] * u
    y_a = _dot((b_gate * z).astype(BF16), wco_ref[0])
    g_a = _dot(xb, wx_ref[:, _C_GA:_C_GA + d])
    yag_ref[0] = (jax.nn.sigmoid(g_a) * y_a).astype(BF16)
    g_b = _dot(xb, wx_ref[:, _C_GB:_C_GB + d])
    sgb_ref[0] = jax.nn.sigmoid(g_b).astype(BF16)

    tab = tab_ref[0]
    t_a, t_m, t_p = tab[:, 0:LANES], tab[:, LANES:2 * LANES], tab[:, 2 * LANES:3 * LANES]
    half = QK_ROPE_DIM // 2

    def rotate(c):
        return (c * t_a + pltpu.roll(c, LANES - half, 1) * t_m + pltpu.roll(c, half, 1) * t_p)

    c_q = _dot(xb, wx_ref[:, _C_CQ:_C_CQ + Q_RANK])
    cqn = _rms_norm(c_q, qn_ref[0]).astype(BF16)
    hw = MLA_HEADS * LANES
    q = _dot(cqn, wq_ref[0])
    for h in range(MLA_HEADS):
        sl = slice(h * LANES, (h + 1) * LANES)
        q_ref[0, :, sl] = rotate(q[:, sl]).astype(BF16)

    c_kv = _dot(xb, wx_ref[:, _C_CKV:_C_CKV + KV_RANK])
    ckvn = _rms_norm(c_kv, kvn_ref[0]).astype(BF16)
    kr_rot = rotate(_dot(xb, wx_ref[:, _C_KR:_C_KR + LANES]))
    k_nope = _dot(ckvn, wkv_ref[0, :, :hw])
    for h in range(MLA_HEADS):
        sl = slice(h * LANES, (h + 1) * LANES)
        k_ref[0, :, sl] = (k_nope[:, sl] + kr_rot).astype(BF16)
    v_ref[0] = _dot(ckvn, wkv_ref[0, :, hw:]).astype(BF16)


def _mixer_in(l, bn, x_b0, tab_b0, x, tabs, wx, wconv, wco, qn, wq, kvn, wkv, ts, dep):
    _, s_len, d = x.shape
    n_t = s_len // ts
    hw = MLA_HEADS * LANES
    vw = MLA_HEADS * V_HEAD_DIM
    tile = lambda w: pl.BlockSpec((1, ts, w), lambda b, i: (b, i, 0))
    once = pl.Buffered(1)
    lay = lambda a: pl.BlockSpec((1,) + a.shape[1:], lambda b, i: (l,) + (0,) * (a.ndim - 1),
                                 pipeline_mode=once)
    halo = pl.BlockSpec((1, HALO_ROWS, d),
                        lambda b, i: (b + x_b0, jnp.maximum(i * (ts // HALO_ROWS) - 1, 0), 0))
    return pl.pallas_call(
        functools.partial(_mixer_in_kernel, d_model=d),
        grid=(bn, n_t),
        in_specs=[pl.BlockSpec((1, ts, d), lambda b, i: (b + x_b0, i, 0)), halo,
                  pl.BlockSpec((1, ts, 3 * LANES), lambda b, i: (b + tab_b0, i, 0)),
                  pl.BlockSpec((None,) + wx.shape[1:], lambda b, i: (l, 0, 0), pipeline_mode=once),
                  lay(wconv), lay(wco), lay(qn), lay(wq), lay(kvn), lay(wkv), _DEP_SPEC],
        out_specs=[tile(d), tile(d), tile(hw), tile(hw), tile(vw)],
        out_shape=[jax.ShapeDtypeStruct((bn, s_len, d), BF16),
                   jax.ShapeDtypeStruct((bn, s_len, d), BF16),
                   jax.ShapeDtypeStruct((bn, s_len, hw), BF16),
                   jax.ShapeDtypeStruct((bn, s_len, hw), BF16),
                   jax.ShapeDtypeStruct((bn, s_len, vw), BF16)],
        compiler_params=pltpu.CompilerParams(
            dimension_semantics=("parallel", "arbitrary"), vmem_limit_bytes=VMEM_LIMIT),
        name="mixer_in",
    )(x, x, tabs, wx, wconv, wco, qn, wq, kvn, wkv, dep)


def _attn_kernel(q_ref, k_ref, v_ref, dep_ref, o_ref, *, tq):
    s_len = q_ref.shape[1]
    n_q = s_len // tq
    lane = lax.broadcasted_iota(jnp.int32, (tq, 2 * V_HEAD_DIM), 1)
    for qi in range(n_q):
        kv_len = (qi + 1) * tq
        rows = slice(qi * tq, (qi + 1) * tq)
        v2 = v_ref[0, :kv_len, :]
        r = lax.broadcasted_iota(jnp.int32, (tq, kv_len), 0) + qi * tq
        c = lax.broadcasted_iota(jnp.int32, (tq, kv_len), 1)
        outs = []
        for h in range(2):
            sl = slice(h * LANES, (h + 1) * LANES)
            s = _dot_nt(q_ref[0, rows, sl], k_ref[0, :kv_len, sl])
            s = jnp.where(c <= r, s, -jnp.inf)
            m = jnp.max(s, axis=-1, keepdims=True)
            p = jnp.exp(s - m)
            den = jnp.sum(p, axis=-1, keepdims=True)
            outs.append(_dot(p.astype(BF16), v2) / den)
        o_ref[0, rows, :] = jnp.where(lane < V_HEAD_DIM, outs[0], outs[1]).astype(BF16)


def _attention(q, k, v, tq, dep):
    bn, s_len, _ = q.shape
    n_pairs = MLA_HEADS // 2
    return pl.pallas_call(
        functools.partial(_attn_kernel, tq=tq),
        grid=(bn, n_pairs),
        in_specs=[pl.BlockSpec((1, s_len, 2 * LANES), lambda b, h: (b, 0, h)),
                  pl.BlockSpec((1, s_len, 2 * LANES), lambda b, h: (b, 0, h)),
                  pl.BlockSpec((1, s_len, 2 * V_HEAD_DIM), lambda b, h: (b, 0, h)), _DEP_SPEC],
        out_specs=pl.BlockSpec((1, s_len, 2 * V_HEAD_DIM), lambda b, h: (b, 0, h)),
        out_shape=jax.ShapeDtypeStruct((bn, s_len, MLA_HEADS * V_HEAD_DIM), BF16),
        compiler_params=pltpu.CompilerParams(
            dimension_semantics=("parallel", "parallel"), vmem_limit_bytes=VMEM_LIMIT),
        name="mla_attention",
    )(q, k, v, dep)


def _mem_kv_kernel(mem_ref, wk_ref, wv_ref, k_ref, v_ref):
    mb = mem_ref[0].astype(BF16)
    k_ref[0, 0] = _dot(mb, wk_ref[0]).astype(BF16)
    v_ref[0, 0] = _dot(mb, wv_ref[0]).astype(BF16)


def _mem_kv(mem, w_xk, w_xv):
    bn, m_len, d = mem.shape
    n_l = w_xk.shape[0]
    wspec = pl.BlockSpec((1, d, d), lambda l, b: (l, 0, 0))
    ospec = pl.BlockSpec((1, 1, m_len, d), lambda l, b: (l, b, 0, 0))
    return pl.pallas_call(
        _mem_kv_kernel,
        grid=(n_l, bn),
        in_specs=[pl.BlockSpec((1, m_len, d), lambda l, b: (b, 0, 0)), wspec, wspec],
        out_specs=[ospec, ospec],
        out_shape=[jax.ShapeDtypeStruct((n_l, bn, m_len, d), BF16)] * 2,
        compiler_params=pltpu.CompilerParams(
            dimension_semantics=("arbitrary", "arbitrary"), vmem_limit_bytes=VMEM_LIMIT),
        name="mem_kv",
    )(mem, w_xk, w_xv)


def _first_argmax(vals, idx):
    m = jnp.max(functools.reduce(jnp.maximum, vals), axis=0, keepdims=True)
    big = jnp.int32(1 << 20)
    cand = functools.reduce(jnp.minimum, [jnp.where(v == m, ix, big) for v, ix in zip(vals, idx)])
    first = jnp.min(cand, axis=0, keepdims=True)
    return [ix == first for ix in idx], m, first


def _route(logits_t, bias):
    n = logits_t.shape[1]
    scores = jax.nn.sigmoid(logits_t)
    sel = scores + bias
    sub = lax.broadcasted_iota(jnp.int32, (GROUP_SIZE, n), 0)
    neg = -jnp.inf
    sel_g = [sel[g * GROUP_SIZE:(g + 1) * GROUP_SIZE] for g in range(N_GROUPS)]
    gs = []
    for g in range(N_GROUPS):
        (hot,), m1, _ = _first_argmax([sel_g[g]], [sub])
        m2 = jnp.max(jnp.where(hot, neg, sel_g[g]), axis=0, keepdims=True)
        gs.append(m1 + m2)
    gsv = jnp.concatenate(gs, axis=0)
    gmask = jnp.zeros(gsv.shape, F32)
    for _ in range(TOPK_GROUPS):
        (hot,), _m, _ = _first_argmax([gsv], [sub])
        gmask = jnp.where(hot, 1.0, gmask)
        gsv = jnp.where(hot, neg, gsv)
    msel = [jnp.where(gmask[g:g + 1] > 0.0, sel_g[g], neg) for g in range(N_GROUPS)]
    eidx = [sub + g * GROUP_SIZE for g in range(N_GROUPS)]
    picks, ids = [], []
    for _ in range(TOP_K):
        hots, _m, first = _first_argmax(msel, eidx)
        picks.append(hots)
        ids.append(first)
        msel = [jnp.where(h, neg, v) for h, v in zip(hots, msel)]
    chosen = [functools.reduce(jnp.logical_or, [p[g] for p in picks]) for g in range(N_GROUPS)]
    sc_g = [scores[g * GROUP_SIZE:(g + 1) * GROUP_SIZE] for g in range(N_GROUPS)]
    picked = [jnp.where(c, s, 0.0) for c, s in zip(chosen, sc_g)]
    wsum = jnp.sum(functools.reduce(lambda a, b: a + b, picked), axis=0, keepdims=True)
    gates = [p / wsum * ROUTE_SCALE for p in picked]
    return picks, ids, chosen, gates


def _pick_rows(hots, vals):
    acc = functools.reduce(lambda a, b: a + b, [jnp.where(h, v, 0.0) for h, v in zip(hots, vals)])
    return jnp.sum(acc, axis=0, keepdims=True)


def _post_kernel(x_ref, yag_ref, sgb_ref, o_ref, km_ref, vm_ref, wao_ref, wmo_ref, ln1_ref,
                 wxq_ref, wxo_ref, ln2_ref, wr_ref, wsgu_ref, wsd_ref, dep_ref,
                 x2p_ref, base_ref, logit_ref, *, alpha):
    d = x_ref.shape[2]
    x = x_ref[0]
    y_b = _dot(o_ref[0], wao_ref[0])
    y = yag_ref[0].astype(F32) + sgb_ref[0].astype(F32) * y_b
    mix = _dot(y.astype(BF16), wmo_ref[0])
    x1 = _layer_norm(alpha * x + mix, ln1_ref[0, 0:1], ln1_ref[0, 1:2])

    hd = d // XATTN_HEADS
    xq = (_dot(x1.astype(BF16), wxq_ref[0]) * (hd ** -0.5)).astype(BF16)
    heads = []
    for h in range(XATTN_HEADS):
        sl = slice(h * hd, (h + 1) * hd)
        s = _dot_nt(xq[:, sl], km_ref[0, 0, :, sl])
        m = jnp.max(s, axis=-1, keepdims=True)
        p = jnp.exp(s - m)
        den = jnp.sum(p, axis=-1, keepdims=True)
        heads.append((_dot(p.astype(BF16), vm_ref[0, 0, :, sl]) / den).astype(BF16))
    xat = _dot(jnp.concatenate(heads, axis=-1), wxo_ref[0])
    x2 = _layer_norm(alpha * x1 + xat, ln2_ref[0, 0:1], ln2_ref[0, 1:2])

    x2b = x2.astype(BF16)
    x2l = (x2 - x2b.astype(F32)).astype(BF16)
    wr = wr_ref[0]
    wrh = wr.astype(BF16)
    wrl = (wr - wrh.astype(F32)).astype(BF16)
    logit_ref[0] = _dot_nt(wrh, x2b) + (_dot_nt(wrh, x2l) + _dot_nt(wrl, x2b))

    x2p_ref[0] = _pack_rows(x2)
    sh = _dot(x2b, wsgu_ref[0])
    f = sh.shape[1] // 2
    hid = _silu(sh[:, :f]) * sh[:, f:]
    base_ref[0] = alpha * x2 + _dot(hid.astype(BF16), wsd_ref[0])


def _post(l, bn, x_b0, mem_b0, x, yag, sgb, o, kmem, vmem, wao, wmo, ln1, wxq, wxo, ln2, wr,
          wsgu, wsd, ts, alpha, dep):
    _, s_len, d = x.shape
    m_len = kmem.shape[2]
    tile = lambda w: pl.BlockSpec((1, ts, w), lambda b, i: (b, i, 0))
    lay = lambda a: pl.BlockSpec((1,) + a.shape[1:], lambda b, i: (l,) + (0,) * (a.ndim - 1),
                                 pipeline_mode=pl.Buffered(1))
    memspec = pl.BlockSpec((1, 1, m_len, d), lambda b, i: (l, b + mem_b0, 0, 0))
    return pl.pallas_call(
        functools.partial(_post_kernel, alpha=alpha),
        grid=(bn, s_len // ts),
        in_specs=[pl.BlockSpec((1, ts, d), lambda b, i: (b + x_b0, i, 0)),
                  tile(d), tile(d), tile(o.shape[2]), memspec, memspec,
                  lay(wao), lay(wmo), lay(ln1), lay(wxq), lay(wxo), lay(ln2), lay(wr),
                  lay(wsgu), lay(wsd), _DEP_SPEC],
        out_specs=[tile(d // 2), tile(d),
                   pl.BlockSpec((1, N_EXPERTS, ts), lambda b, i: (b, 0, i))],
        out_shape=[jax.ShapeDtypeStruct((bn, s_len, d // 2), I32),
                   jax.ShapeDtypeStruct((bn, s_len, d), F32),
                   jax.ShapeDtypeStruct((bn, N_EXPERTS, s_len), F32)],
        compiler_params=pltpu.CompilerParams(
            dimension_semantics=("parallel", "arbitrary"), vmem_limit_bytes=VMEM_LIMIT),
        name="post_mixer",
    )(x, yag, sgb, o, kmem, vmem, wao, wmo, ln1, wxq, wxo, ln2, wr, wsgu, wsd, dep)


RANK_BLOCK = 512


def _route_kernel(lg_ref, rb_ref, eid_ref, rank_ref, gate_ref, cnt_ref):
    @pl.when(pl.program_id(0) == 0)
    def _():
        cnt_ref[...] = jnp.zeros_like(cnt_ref)

    n = lg_ref.shape[2]
    blk = min(RANK_BLOCK, n)
    picks, ids, chosen, gates = _route(lg_ref[0], rb_ref[0])

    sel_t = jnp.concatenate([c.astype(F32) for c in chosen], axis=0)
    before = (lax.broadcasted_iota(jnp.int32, (blk, blk), 0)
              < lax.broadcasted_iota(jnp.int32, (blk, blk), 1)).astype(BF16)
    running = cnt_ref[:, 0:1]
    ranks = []
    for j in range(n // blk):
        sb = sel_t[:, j * blk:(j + 1) * blk]
        ranks.append(_dot(sb.astype(BF16), before) + running)
        running = running + jnp.sum(sb, axis=1, keepdims=True)
    cnt_ref[...] = jnp.broadcast_to(running, cnt_ref.shape)
    rank_all = jnp.concatenate(ranks, axis=1)
    rank_g = [rank_all[g * GROUP_SIZE:(g + 1) * GROUP_SIZE] for g in range(N_GROUPS)]
    eid_ref[0] = jnp.concatenate(ids, axis=0)
    rank_ref[0] = jnp.concatenate([_pick_rows(h, rank_g) for h in picks], axis=0).astype(I32)
    gate_ref[0] = jnp.concatenate([_pick_rows(h, gates) for h in picks], axis=0)


def _route_call(l, logits, rb):
    bn, n_e, s_len = logits.shape
    pick = pl.BlockSpec((1, TOP_K, s_len), lambda b: (b, 0, 0))
    return pl.pallas_call(
        _route_kernel,
        grid=(bn,),
        in_specs=[pl.BlockSpec((1, n_e, s_len), lambda b: (b, 0, 0)),
                  pl.BlockSpec((1,) + rb.shape[1:], lambda b: (l, 0, 0))],
        out_specs=[pick, pick, pick, pl.BlockSpec((n_e, LANES), lambda b: (0, 0))],
        out_shape=[jax.ShapeDtypeStruct((bn, TOP_K, s_len), I32),
                   jax.ShapeDtypeStruct((bn, TOP_K, s_len), I32),
                   jax.ShapeDtypeStruct((bn, TOP_K, s_len), F32),
                   jax.ShapeDtypeStruct((n_e, LANES), F32)],
        compiler_params=pltpu.CompilerParams(
            dimension_semantics=("arbitrary",), vmem_limit_bytes=VMEM_LIMIT),
        name="moe_route",
    )(logits, rb)


def _slot_kernel(offs_ref, eid_ref, rank_ref, slot_ref):
    eid = eid_ref[0]
    acc = rank_ref[0]
    for e in range(N_EXPERTS):
        acc = acc + jnp.where(eid == e, offs_ref[e], 0)
    slot_ref[0] = acc


def _slots(offs, eid, rank):
    bn, k, s_len = eid.shape
    spec = pl.BlockSpec((1, k, s_len), lambda b, offs_ref: (b, 0, 0))
    return pl.pallas_call(
        _slot_kernel,
        grid_spec=pltpu.PrefetchScalarGridSpec(
            num_scalar_prefetch=1, grid=(bn,), in_specs=[spec, spec], out_specs=spec),
        out_shape=jax.ShapeDtypeStruct((bn, k, s_len), I32),
        name="moe_slots",
    )(offs, eid, rank)


def _sc_mesh():
    return plsc.VectorSubcoreMesh(core_axis_name="c", subcore_axis_name="s")


def _sc_dispatch(xp, slot_chunks, n_slots, dep):
    t, w = xp.shape
    n_chunks, k, ch = slot_chunks.shape
    per_worker = n_chunks // (SC_CORES * SC_SUBCORES)

    @functools.partial(
        pl.kernel, mesh=_sc_mesh(),
        out_type=jax.ShapeDtypeStruct((n_slots, w), I32),
        scratch_types=[pltpu.VMEM((k, ch), I32), pltpu.VMEM((ch, w), I32), pltpu.SemaphoreType.DMA],
        name="moe_dispatch")
    def body(x_hbm, slot_hbm, dep_hbm, out_hbm, idx_v, rows_v, sem):
        worker = lax.axis_index("s") * SC_CORES + lax.axis_index("c")

        @pl.loop(0, per_worker)
        def _(j):
            c = worker * per_worker + j
            pltpu.sync_copy(slot_hbm.at[c], idx_v)
            pltpu.sync_copy(x_hbm.at[pl.ds(c * ch, ch)], rows_v)
            copies = [pltpu.async_copy(rows_v, out_hbm.at[idx_v.at[kk]], sem) for kk in range(k)]
            for cp in copies:
                cp.wait()

    return body(xp, slot_chunks, dep)


def _sc_combine(ys, slot_chunks, t):
    w = ys.shape[1]
    n_chunks, k, ch = slot_chunks.shape
    per_worker = n_chunks // (SC_CORES * SC_SUBCORES)

    @functools.partial(
        pl.kernel, mesh=_sc_mesh(),
        out_type=jax.ShapeDtypeStruct((k, t, w), I32),
        scratch_types=[pltpu.VMEM((k, ch), I32), pltpu.VMEM((ch, w), I32), pltpu.VMEM((ch, w), I32),
                       pltpu.SemaphoreType.DMA, pltpu.SemaphoreType.DMA,
                       pltpu.SemaphoreType.DMA, pltpu.SemaphoreType.DMA],
        name="moe_combine")
    def body(y_hbm, slot_hbm, out_hbm, idx_v, rows0, rows1, g0, g1, w0, w1):
        worker = lax.axis_index("s") * SC_CORES + lax.axis_index("c")
        bufs, gsem, wsem = (rows0, rows1), (g0, g1), (w0, w1)

        @pl.loop(0, per_worker)
        def _(j):
            c = worker * per_worker + j
            pltpu.sync_copy(slot_hbm.at[c], idx_v)
            gathers = [None] * k
            writes = [None] * k
            gathers[0] = pltpu.async_copy(y_hbm.at[idx_v.at[0]], bufs[0], gsem[0])
            for kk in range(k):
                cur = kk % 2
                if kk + 1 < k:
                    if kk >= 1:
                        writes[kk - 1].wait()
                    gathers[kk + 1] = pltpu.async_copy(
                        y_hbm.at[idx_v.at[kk + 1]], bufs[1 - cur], gsem[1 - cur])
                gathers[kk].wait()
                writes[kk] = pltpu.async_copy(
                    bufs[cur], out_hbm.at[kk, pl.ds(c * ch, ch)], wsem[cur])
            writes[k - 2].wait()
            writes[k - 1].wait()

    return body(ys, slot_chunks)


def _experts_kernel(ti_ref, te_ref, nv_ref, nu_ref, xs_ref, wg_ref, wu_ref, wd_ref, dep_ref, ys_ref,
                    wg_b, wu_b, wd_b):
    j = pl.program_id(0)
    prev = te_ref[jnp.maximum(j - 1, 0)]
    new_expert = jnp.logical_or(j == 0, te_ref[j] != prev)

    @pl.when(new_expert)
    def _():
        wg_b[...] = wg_ref[0, 0].astype(BF16)
        wu_b[...] = wu_ref[0, 0].astype(BF16)
        wd_b[...] = wd_ref[0, 0].astype(BF16)

    def ffn(n_rows):
        rows = pl.ds(0, n_rows)
        xb = _unpack_rows(xs_ref[rows, :]).astype(BF16)
        hid = _silu(_dot(xb, wg_b[...])) * _dot(xb, wu_b[...])
        ys_ref[rows, :] = _pack_rows(_dot(hid.astype(BF16), wd_b[...]))

    used = j < nu_ref[0]
    half = xs_ref.shape[0] // 2
    many = nv_ref[j] > half

    @pl.when(jnp.logical_and(used, many))
    def _():
        ffn(2 * half)

    @pl.when(jnp.logical_and(used, jnp.logical_not(many)))
    def _():
        ffn(half)


def _experts(l, xs, ti, te, nv, nu, w_gate, w_up, w_down, dep):
    n_slots, w = xs.shape
    n_tiles = n_slots // EXPERT_TILE
    d, f = w_gate.shape[2:]
    rows = pl.BlockSpec((EXPERT_TILE, w), lambda j, ti, te, nv, nu: (ti[j], 0))
    wspec = lambda a, b: pl.BlockSpec((1, 1, a, b), lambda j, ti, te, nv, nu: (l, te[j], 0, 0))
    return pl.pallas_call(
        _experts_kernel,
        grid_spec=pltpu.PrefetchScalarGridSpec(
            num_scalar_prefetch=4, grid=(n_tiles,),
            in_specs=[rows, wspec(d, f), wspec(d, f), wspec(f, d), _DEP_SPEC],
            out_specs=rows,
            scratch_shapes=[pltpu.VMEM((d, f), BF16), pltpu.VMEM((d, f), BF16),
                            pltpu.VMEM((f, d), BF16)]),
        out_shape=jax.ShapeDtypeStruct((n_slots, w), I32),
        compiler_params=pltpu.CompilerParams(
            dimension_semantics=("arbitrary",), vmem_limit_bytes=VMEM_LIMIT),
        name="moe_experts",
    )(ti, te, nv, nu, xs, w_gate, w_up, w_down, dep)


def _final_kernel(yg_ref, gate_ref, base_ref, ln_ref, *rest):
    out_ref = rest[-1]
    gates = gate_ref[0].T
    acc = base_ref[0]
    for k in range(yg_ref.shape[0]):
        acc = acc + gates[:, k:k + 1] * _unpack_rows(yg_ref[k, 0])
    out_ref[0] = _layer_norm(acc, ln_ref[0, 0:1], ln_ref[0, 1:2])


def _final(l, yg, gate8, base, ln3, tm, out_rows, out_b0, out_prev, dep):
    bn, s_len, d = base.shape
    k = yg.shape[0]
    in_specs = [pl.BlockSpec((k, 1, tm, d // 2), lambda b, i: (0, b, i, 0)),
                pl.BlockSpec((1, k, tm), lambda b, i: (b, 0, i)),
                pl.BlockSpec((1, tm, d), lambda b, i: (b, i, 0)),
                pl.BlockSpec((1,) + ln3.shape[1:], lambda b, i: (l, 0, 0)), _DEP_SPEC]
    args = [yg, gate8, base, ln3, dep]
    aliases = {}
    if out_prev is not None:
        in_specs.append(pl.BlockSpec(memory_space=pl.ANY))
        args.append(out_prev)
        aliases = {len(args) - 1: 0}
    return pl.pallas_call(
        _final_kernel,
        grid=(bn, s_len // tm),
        in_specs=in_specs,
        out_specs=pl.BlockSpec((1, tm, d), lambda b, i: (b + out_b0, i, 0)),
        out_shape=jax.ShapeDtypeStruct((out_rows, s_len, d), F32),
        input_output_aliases=aliases,
        compiler_params=pltpu.CompilerParams(
            dimension_semantics=("parallel", "parallel"), vmem_limit_bytes=VMEM_LIMIT),
        name="moe_final",
    )(*args)


def _moe_schedule(eid, rank, counts):
    bn, _, s_len = eid.shape
    t = bn * s_len
    n_tiles = (t * TOP_K) // EXPERT_TILE + N_EXPERTS

    cnt = counts[:, 0].astype(I32)
    padded = (cnt + EXPERT_TILE - 1) // EXPERT_TILE * EXPERT_TILE
    ends = jnp.cumsum(padded)
    offs = ends - padded
    n_used = ends[-1] // EXPERT_TILE
    ti = jnp.minimum(jnp.arange(n_tiles, dtype=I32), n_used - 1)
    te = jnp.sum((ends[None, :] <= (ti * EXPERT_TILE)[:, None]).astype(I32), axis=1)
    nv = jnp.take(cnt + offs, te) - ti * EXPERT_TILE

    slots = _slots(offs, eid, rank)
    slot_chunks = slots.reshape(bn, TOP_K, s_len // SC_CHUNK, SC_CHUNK).transpose(0, 2, 1, 3)
    slot_chunks = slot_chunks.reshape(t // SC_CHUNK, TOP_K, SC_CHUNK)
    return slot_chunks, ti, te, nv, n_used.reshape(1), n_tiles * EXPERT_TILE


def _head_chunks(w, widths, n_heads, per_head):
    lead = w.shape[:-1]
    wh = w.reshape(lead + (n_heads, per_head))
    parts = [sign * wh[..., a:b] for a, b, sign in widths]
    used = sum(b - a for a, b, _ in widths)
    parts.append(jnp.zeros(lead + (n_heads, LANES - used), w.dtype))
    return jnp.concatenate(parts, axis=-1).reshape(lead + (n_heads * LANES,))


def kernel(x, mem, positions, w_in, w_conv, w_conv_out, q_norm, w_uq, kv_norm, w_uk, w_uv,
           w_attn_out, w_mix_out, ln1_g, ln1_b, w_xq, w_xk, w_xv, w_xo, ln2_g, ln2_b,
           w_router, router_bias, w_gate, w_up, w_down, ws_gate, ws_up, ws_down, ln3_g, ln3_b):
    bn, s_len, d = x.shape
    depth = w_in.shape[0]
    alpha = (2 * depth) ** 0.25
    ts = min(512, s_len)
    tq = min(512, s_len)
    half = QK_ROPE_DIM // 2
    nope, rope = QK_NOPE_DIM, QK_ROPE_DIM

    inv_freq = ROPE_BASE ** (-jnp.arange(half, dtype=F32) / half)
    ang = positions.astype(F32)[..., None] * inv_freq
    cos, sin = jnp.cos(ang), jnp.sin(ang)
    scale = (nope + rope) ** -0.5
    zeros = lambda w: jnp.zeros((bn, s_len, w), F32)
    tail = LANES - nope - rope
    tabs = jnp.concatenate([
        jnp.ones((bn, s_len, nope), F32), cos, cos, zeros(tail),
        zeros(nope), -sin, zeros(half), zeros(tail),
        zeros(nope), zeros(half), sin, zeros(tail)], axis=-1)

    sizes = (CONV_DIM, CONV_DIM, CONV_DIM, Q_RANK, KV_RANK, rope, d, d)
    offs = [0]
    for sz in sizes:
        offs.append(offs[-1] + sz)
    seg = lambda j: w_in[:, :, offs[j]:offs[j + 1]]
    zpad = lambda w: jnp.zeros(w_in.shape[:2] + (w,), w_in.dtype)
    wx = jnp.concatenate([w_in[:, :, :offs[5]], zpad(nope), seg(5), zpad(tail), w_in[:, :, offs[6]:]],
                         axis=-1).astype(BF16)

    per_q = nope + rope
    wq = _head_chunks(w_uq, [(0, per_q, scale)], MLA_HEADS, per_q).astype(BF16)
    wk = _head_chunks(w_uk, [(0, nope, 1.0)], MLA_HEADS, nope)
    wkv = jnp.concatenate([wk, w_uv], axis=-1).astype(BF16)

    wconv = w_conv[:, :, 0, :]
    wco = w_conv_out.astype(BF16)
    qn = q_norm[:, None, :]
    kvn = kv_norm[:, None, :]
    wao = w_attn_out.astype(BF16)
    wmo = w_mix_out.astype(BF16)
    ln1 = jnp.stack([ln1_g, ln1_b], axis=1)
    ln2 = jnp.stack([ln2_g, ln2_b], axis=1)
    ln3 = jnp.stack([ln3_g, ln3_b], axis=1)
    wxq = w_xq.astype(BF16)
    wxo = w_xo.astype(BF16)
    wr = jnp.swapaxes(w_router, 1, 2)
    rb = router_bias[:, :, None]
    wsgu = jnp.concatenate([ws_gate, ws_up], axis=-1).astype(BF16)
    wsd = ws_down.astype(BF16)

    kmem, vmem = _mem_kv(mem, w_xk.astype(BF16), w_xv.astype(BF16))

    assert bn % 2 == 0
    hb = bn // 2
    t = hb * s_len
    tm = min(512, s_len)

    def mix_attn(l, c, xc, dep):
        x_b0 = c * hb if l == 0 else 0
        yag, sgb, q, k, v = _mixer_in(l, hb, x_b0, c * hb, xc, tabs, wx, wconv, wco, qn, wq, kvn, wkv,
                                      min(1024, s_len), dep)
        return yag, sgb, _attention(q, k, v, tq, yag)

    def post_route(l, c, xc, ma, dep):
        x_b0 = c * hb if l == 0 else 0
        yag, sgb, o = ma
        x2p, base, logits = _post(l, hb, x_b0, c * hb, xc, yag, sgb, o, kmem, vmem, wao, wmo, ln1,
                                  wxq, wxo, ln2, wr, wsgu, wsd, min(1024, s_len), alpha, dep)
        eid, rank, gate8, counts = _route_call(l, logits, rb)
        slot_chunks, ti, te, nv, nu, n_slots = _moe_schedule(eid, rank, counts)
        return dict(x2p=x2p.reshape(t, d // 2), base=base, gate8=gate8, slot_chunks=slot_chunks,
                    ti=ti, te=te, nv=nv, nu=nu, n_slots=n_slots)

    def dispatch(st, dep):
        st["xs"] = _sc_dispatch(st["x2p"], st["slot_chunks"], st["n_slots"], dep)

    def experts(l, st, dep):
        st["ys"] = _experts(l, st["xs"], st["ti"], st["te"], st["nv"], st["nu"], w_gate, w_up, w_down,
                            dep)
        st["yg"] = _sc_combine(st["ys"], st["slot_chunks"], t).reshape(TOP_K, hb, s_len, d // 2)

    def final(l, c, st, dep, out_prev):
        if l == depth - 1:
            return _final(l, st["yg"], st["gate8"], st["base"], ln3, tm, bn, c * hb, out_prev, dep)
        return _final(l, st["yg"], st["gate8"], st["base"], ln3, tm, hb, 0, None, dep)

    xa = xb = x
    ma_a = mix_attn(0, 0, xa, x)
    st_a = post_route(0, 0, xa, ma_a, ma_a[2])
    dispatch(st_a, st_a["gate8"])
    for l in range(depth):
        more = l + 1 < depth
        ma_b = mix_attn(l, 1, xb, st_a["gate8"])
        experts(l, st_a, ma_b[2])
        st_b = post_route(l, 1, xb, ma_b, st_a["ys"])
        xa = final(l, 0, st_a, st_b["gate8"], None)
        dispatch(st_b, xa if more else st_b["gate8"])
        if more:
            ma_a = mix_attn(l + 1, 0, xa, xa)
        experts(l, st_b, ma_a[2] if more else xa)
        if more:
            st_a = post_route(l + 1, 0, xa, ma_a, st_b["ys"])
        xb = final(l, 1, st_b, st_a["gate8"] if more else st_b["ys"], xa if not more else None)
        if more:
            dispatch(st_a, xb)
    return xb
```

```python
import functools

import jax
import jax.numpy as jnp
from jax import lax
from jax.experimental import pallas as pl
from jax.experimental.pallas import tpu as pltpu
from jax.experimental.pallas import tpu_sc as plsc

CONV_DIM = 512
CONV_WIDTH = 3
MLA_HEADS = 8
QK_NOPE_DIM = 64
QK_ROPE_DIM = 32
V_HEAD_DIM = 64
Q_RANK = 384
KV_RANK = 256
ROPE_BASE = 10000.0
XATTN_HEADS = 4
N_EXPERTS = 64
TOP_K = 8
N_GROUPS = 8
TOPK_GROUPS = 4
GROUP_SIZE = N_EXPERTS // N_GROUPS
ROUTE_SCALE = 2.5
LN_EPS = 1e-5
RMS_EPS = 1e-6

LANES = 128
HALO_ROWS = 8
VMEM_LIMIT = 56 * 1024 * 1024
SC_CORES = 2
SC_SUBCORES = 16
SC_CHUNK = 64
EXPERT_TILE = 1024

BF16 = jnp.bfloat16
F32 = jnp.float32
I32 = jnp.int32


_DEP_SPEC = pl.BlockSpec(memory_space=pl.ANY)


def _dot(a, b):
    return jnp.dot(a, b, preferred_element_type=F32)


def _dot_nt(a, b):
    return lax.dot_general(a, b, (((1,), (1,)), ((), ())), preferred_element_type=F32)


def _layer_norm(x, g, b):
    mu = jnp.mean(x, axis=-1, keepdims=True)
    xc = x - mu
    var = jnp.mean(xc * xc, axis=-1, keepdims=True)
    return xc * lax.rsqrt(var + LN_EPS) * g + b


def _rms_norm(x, g):
    ms = jnp.mean(x * x, axis=-1, keepdims=True)
    return x * lax.rsqrt(ms + RMS_EPS) * g


def _silu(x):
    return x * jax.nn.sigmoid(x)


_HI_MASK = -65536


def _pack_rows(y):
    w = y.shape[1] // 2
    lo = lax.bitcast_convert_type(y[:, :w].astype(BF16).astype(F32), I32)
    hi = lax.bitcast_convert_type(y[:, w:].astype(BF16).astype(F32), I32)
    return lax.shift_right_logical(lo, 16) | (hi & _HI_MASK)


def _unpack_rows(p):
    lo = lax.bitcast_convert_type(lax.shift_left(p, 16), F32)
    hi = lax.bitcast_convert_type(p & _HI_MASK, F32)
    return jnp.concatenate([lo, hi], axis=-1)


_C_BCH = 0
_C_CQ = _C_BCH + 3 * CONV_DIM
_C_CKV = _C_CQ + Q_RANK
_C_KR = _C_CKV + KV_RANK
_C_GA = _C_KR + LANES
_C_GB = _C_GA + 1024


def _mixer_in_kernel(x_ref, xh_ref, tab_ref, wx_ref, wconv_ref, wco_ref, qn_ref, wq_ref,
                     kvn_ref, wkv_ref, dep_ref,
                     yag_ref, sgb_ref, q_ref, k_ref, v_ref, *, d_model):
    i = pl.program_id(1)
    d = d_model
    ts = x_ref.shape[1]
    xb = x_ref[0].astype(BF16)

    bch = _dot(xb, wx_ref[:, _C_BCH:_C_BCH + 3 * CONV_DIM])
    b_gate = bch[:, :CONV_DIM]
    u = bch[:, CONV_DIM:2 * CONV_DIM] * bch[:, 2 * CONV_DIM:]
    xh = xh_ref[0].astype(BF16)
    hp = _dot(xh, wx_ref[:, _C_BCH + CONV_DIM:_C_BCH + 3 * CONV_DIM])
    uh = hp[:, :CONV_DIM] * hp[:, CONV_DIM:]
    uh = jnp.where(i == 0, 0.0, uh)
    row = lax.broadcasted_iota(jnp.int32, (ts, CONV_DIM), 0)
    u1 = jnp.where(row == 0, uh[HALO_ROWS - 1:HALO_ROWS], pltpu.roll(u, 1, 0))
    u2 = jnp.where(row == 0, uh[HALO_ROWS - 2:HALO_ROWS - 1],
                   jnp.where(row == 1, uh[HALO_ROWS - 1:HALO_ROWS], pltpu.roll(u, 2, 0)))
    wc = wconv_ref[0]
    z = wc[0:1] * u2 + wc[1:2] * u1 + wc[2:3] * u
    y_a = _dot((b_gate * z).astype(BF16), wco_ref[0])
    g_a = _dot(xb, wx_ref[:, _C_GA:_C_GA + d])
    yag_ref[0] = (jax.nn.sigmoid(g_a) * y_a).astype(BF16)
    g_b = _dot(xb, wx_ref[:, _C_GB:_C_GB + d])
    sgb_ref[0] = jax.nn.sigmoid(g_b).astype(BF16)

    tab = tab_ref[0]
    t_a, t_m, t_p = tab[:, 0:LANES], tab[:, LANES:2 * LANES], tab[:, 2 * LANES:3 * LANES]
    half = QK_ROPE_DIM // 2

    def rotate(c):
        return (c * t_a + pltpu.roll(c, LANES - half, 1) * t_m + pltpu.roll(c, half, 1) * t_p)

    c_q = _dot(xb, wx_ref[:, _C_CQ:_C_CQ + Q_RANK])
    cqn = _rms_norm(c_q, qn_ref[0]).astype(BF16)
    hw = MLA_HEADS * LANES
    q = _dot(cqn, wq_ref[0])
    for h in range(MLA_HEADS):
        sl = slice(h * LANES, (h + 1) * LANES)
        q_ref[0, :, sl] = rotate(q[:, sl]).astype(BF16)

    c_kv = _dot(xb, wx_ref[:, _C_CKV:_C_CKV + KV_RANK])
    ckvn = _rms_norm(c_kv, kvn_ref[0]).astype(BF16)
    kr_rot = rotate(_dot(xb, wx_ref[:, _C_KR:_C_KR + LANES]))
    k_nope = _dot(ckvn, wkv_ref[0, :, :hw])
    for h in range(MLA_HEADS):
        sl = slice(h * LANES, (h + 1) * LANES)
        k_ref[0, :, sl] = (k_nope[:, sl] + kr_rot).astype(BF16)
    v_ref[0] = _dot(ckvn, wkv_ref[0, :, hw:]).astype(BF16)


def _mixer_in(l, bn, x_b0, tab_b0, x, tabs, wx, wconv, wco, qn, wq, kvn, wkv, ts, dep):
    _, s_len, d = x.shape
    n_t = s_len // ts
    hw = MLA_HEADS * LANES
    vw = MLA_HEADS * V_HEAD_DIM
    tile = lambda w: pl.BlockSpec((1, ts, w), lambda b, i: (b, i, 0))
    once = pl.Buffered(1)
    lay = lambda a: pl.BlockSpec((1,) + a.shape[1:], lambda b, i: (l,) + (0,) * (a.ndim - 1),
                                 pipeline_mode=once)
    halo = pl.BlockSpec((1, HALO_ROWS, d),
                        lambda b, i: (b + x_b0, jnp.maximum(i * (ts // HALO_ROWS) - 1, 0), 0))
    return pl.pallas_call(
        functools.partial(_mixer_in_kernel, d_model=d),
        grid=(bn, n_t),
        in_specs=[pl.BlockSpec((1, ts, d), lambda b, i: (b + x_b0, i, 0)), halo,
                  pl.BlockSpec((1, ts, 3 * LANES), lambda b, i: (b + tab_b0, i, 0)),
                  pl.BlockSpec((None,) + wx.shape[1:], lambda b, i: (l, 0, 0), pipeline_mode=once),
                  lay(wconv), lay(wco), lay(qn), lay(wq), lay(kvn), lay(wkv), _DEP_SPEC],
        out_specs=[tile(d), tile(d), tile(hw), tile(hw), tile(vw)],
        out_shape=[jax.ShapeDtypeStruct((bn, s_len, d), BF16),
                   jax.ShapeDtypeStruct((bn, s_len, d), BF16),
                   jax.ShapeDtypeStruct((bn, s_len, hw), BF16),
                   jax.ShapeDtypeStruct((bn, s_len, hw), BF16),
                   jax.ShapeDtypeStruct((bn, s_len, vw), BF16)],
        compiler_params=pltpu.CompilerParams(
            dimension_semantics=("parallel", "arbitrary"), vmem_limit_bytes=VMEM_LIMIT),
        name="mixer_in",
    )(x, x, tabs, wx, wconv, wco, qn, wq, kvn, wkv, dep)


def _attn_kernel(q_ref, k_ref, v_ref, dep_ref, o_ref, *, tq):
    s_len = q_ref.shape[1]
    n_q = s_len // tq
    lane = lax.broadcasted_iota(jnp.int32, (tq, 2 * V_HEAD_DIM), 1)
    for qi in range(n_q):
        kv_len = (qi + 1) * tq
        rows = slice(qi * tq, (qi + 1) * tq)
        v2 = v_ref[0, :kv_len, :]
        r = lax.broadcasted_iota(jnp.int32, (tq, kv_len), 0) + qi * tq
        c = lax.broadcasted_iota(jnp.int32, (tq, kv_len), 1)
        outs = []
        for h in range(2):
            sl = slice(h * LANES, (h + 1) * LANES)
            s = _dot_nt(q_ref[0, rows, sl], k_ref[0, :kv_len, sl])
            s = jnp.where(c <= r, s, -jnp.inf)
            m = jnp.max(s, axis=-1, keepdims=True)
            p = jnp.exp(s - m)
            den = jnp.sum(p, axis=-1, keepdims=True)
            outs.append(_dot(p.astype(BF16), v2) / den)
        o_ref[0, rows, :] = jnp.where(lane < V_HEAD_DIM, outs[0], outs[1]).astype(BF16)


def _attention(q, k, v, tq, dep):
    bn, s_len, _ = q.shape
    n_pairs = MLA_HEADS // 2
    return pl.pallas_call(
        functools.partial(_attn_kernel, tq=tq),
        grid=(bn, n_pairs),
        in_specs=[pl.BlockSpec((1, s_len, 2 * LANES), lambda b, h: (b, 0, h)),
                  pl.BlockSpec((1, s_len, 2 * LANES), lambda b, h: (b, 0, h)),
                  pl.BlockSpec((1, s_len, 2 * V_HEAD_DIM), lambda b, h: (b, 0, h)), _DEP_SPEC],
        out_specs=pl.BlockSpec((1, s_len, 2 * V_HEAD_DIM), lambda b, h: (b, 0, h)),
        out_shape=jax.ShapeDtypeStruct((bn, s_len, MLA_HEADS * V_HEAD_DIM), BF16),
        compiler_params=pltpu.CompilerParams(
            dimension_semantics=("parallel", "parallel"), vmem_limit_bytes=VMEM_LIMIT),
        name="mla_attention",
    )(q, k, v, dep)


def _mem_kv_kernel(mem_ref, wk_ref, wv_ref, k_ref, v_ref):
    mb = mem_ref[0].astype(BF16)
    k_ref[0, 0] = _dot(mb, wk_ref[0]).astype(BF16)
    v_ref[0, 0] = _dot(mb, wv_ref[0]).astype(BF16)


def _mem_kv(mem, w_xk, w_xv):
    bn, m_len, d = mem.shape
    n_l = w_xk.shape[0]
    wspec = pl.BlockSpec((1, d, d), lambda l, b: (l, 0, 0))
    ospec = pl.BlockSpec((1, 1, m_len, d), lambda l, b: (l, b, 0, 0))
    return pl.pallas_call(
        _mem_kv_kernel,
        grid=(n_l, bn),
        in_specs=[pl.BlockSpec((1, m_len, d), lambda l, b: (b, 0, 0)), wspec, wspec],
        out_specs=[ospec, ospec],
        out_shape=[jax.ShapeDtypeStruct((n_l, bn, m_len, d), BF16)] * 2,
        compiler_params=pltpu.CompilerParams(
            dimension_semantics=("arbitrary", "arbitrary"), vmem_limit_bytes=VMEM_LIMIT),
        name="mem_kv",
    )(mem, w_xk, w_xv)


def _first_argmax(vals, idx):
    m = jnp.max(functools.reduce(jnp.maximum, vals), axis=0, keepdims=True)
    big = jnp.int32(1 << 20)
    cand = functools.reduce(jnp.minimum, [jnp.where(v == m, ix, big) for v, ix in zip(vals, idx)])
    first = jnp.min(cand, axis=0, keepdims=True)
    return [ix == first for ix in idx], m, first


def _route(logits_t, bias):
    n = logits_t.shape[1]
    scores = jax.nn.sigmoid(logits_t)
    sel = scores + bias
    sub = lax.broadcasted_iota(jnp.int32, (GROUP_SIZE, n), 0)
    neg = -jnp.inf
    sel_g = [sel[g * GROUP_SIZE:(g + 1) * GROUP_SIZE] for g in range(N_GROUPS)]
    gs = []
    for g in range(N_GROUPS):
        (hot,), m1, _ = _first_argmax([sel_g[g]], [sub])
        m2 = jnp.max(jnp.where(hot, neg, sel_g[g]), axis=0, keepdims=True)
        gs.append(m1 + m2)
    gsv = jnp.concatenate(gs, axis=0)
    gmask = jnp.zeros(gsv.shape, F32)
    for _ in range(TOPK_GROUPS):
        (hot,), _m, _ = _first_argmax([gsv], [sub])
        gmask = jnp.where(hot, 1.0, gmask)
        gsv = jnp.where(hot, neg, gsv)
    msel = [jnp.where(gmask[g:g + 1] > 0.0, sel_g[g], neg) for g in range(N_GROUPS)]
    eidx = [sub + g * GROUP_SIZE for g in range(N_GROUPS)]
    picks, ids = [], []
    for _ in range(TOP_K):
        hots, _m, first = _first_argmax(msel, eidx)
        picks.append(hots)
        ids.append(first)
        msel = [jnp.where(h, neg, v) for h, v in zip(hots, msel)]
    chosen = [functools.reduce(jnp.logical_or, [p[g] for p in picks]) for g in range(N_GROUPS)]
    sc_g = [scores[g * GROUP_SIZE:(g + 1) * GROUP_SIZE] for g in range(N_GROUPS)]
    picked = [jnp.where(c, s, 0.0) for c, s in zip(chosen, sc_g)]
    wsum = jnp.sum(functools.reduce(lambda a, b: a + b, picked), axis=0, keepdims=True)
    gates = [p / wsum * ROUTE_SCALE for p in picked]
    return picks, ids, chosen, gates


def _pick_rows(hots, vals):
    acc = functools.reduce(lambda a, b: a + b, [jnp.where(h, v, 0.0) for h, v in zip(hots, vals)])
    return jnp.sum(acc, axis=0, keepdims=True)


def _post_kernel(x_ref, yag_ref, sgb_ref, o_ref, km_ref, vm_ref, wao_ref, wmo_ref, ln1_ref,
                 wxq_ref, wxo_ref, ln2_ref, wr_ref, wsgu_ref, wsd_ref, dep_ref,
                 x2p_ref, base_ref, logit_ref, *, alpha):
    d = x_ref.shape[2]
    x = x_ref[0]
    y_b = _dot(o_ref[0], wao_ref[0])
    y = yag_ref[0].astype(F32) + sgb_ref[0].astype(F32) * y_b
    mix = _dot(y.astype(BF16), wmo_ref[0])
    x1 = _layer_norm(alpha * x + mix, ln1_ref[0, 0:1], ln1_ref[0, 1:2])

    hd = d // XATTN_HEADS
    xq = (_dot(x1.astype(BF16), wxq_ref[0]) * (hd ** -0.5)).astype(BF16)
    heads = []
    for h in range(XATTN_HEADS):
        sl = slice(h * hd, (h + 1) * hd)
        s = _dot_nt(xq[:, sl], km_ref[0, 0, :, sl])
        m = jnp.max(s, axis=-1, keepdims=True)
        p = jnp.exp(s - m)
        den = jnp.sum(p, axis=-1, keepdims=True)
        heads.append((_dot(p.astype(BF16), vm_ref[0, 0, :, sl]) / den).astype(BF16))
    xat = _dot(jnp.concatenate(heads, axis=-1), wxo_ref[0])
    x2 = _layer_norm(alpha * x1 + xat, ln2_ref[0, 0:1], ln2_ref[0, 1:2])

    x2b = x2.astype(BF16)
    x2l = (x2 - x2b.astype(F32)).astype(BF16)
    wr = wr_ref[0]
    wrh = wr.astype(BF16)
    wrl = (wr - wrh.astype(F32)).astype(BF16)
    logit_ref[0] = _dot_nt(wrh, x2b) + (_dot_nt(wrh, x2l) + _dot_nt(wrl, x2b))

    x2p_ref[0] = _pack_rows(x2)
    sh = _dot(x2b, wsgu_ref[0])
    f = sh.shape[1] // 2
    hid = _silu(sh[:, :f]) * sh[:, f:]
    base_ref[0] = alpha * x2 + _dot(hid.astype(BF16), wsd_ref[0])


def _post(l, bn, x_b0, mem_b0, x, yag, sgb, o, kmem, vmem, wao, wmo, ln1, wxq, wxo, ln2, wr,
          wsgu, wsd, ts, alpha, dep):
    _, s_len, d = x.shape
    m_len = kmem.shape[2]
    tile = lambda w: pl.BlockSpec((1, ts, w), lambda b, i: (b, i, 0))
    lay = lambda a: pl.BlockSpec((1,) + a.shape[1:], lambda b, i: (l,) + (0,) * (a.ndim - 1),
                                 pipeline_mode=pl.Buffered(1))
    memspec = pl.BlockSpec((1, 1, m_len, d), lambda b, i: (l, b + mem_b0, 0, 0))
    return pl.pallas_call(
        functools.partial(_post_kernel, alpha=alpha),
        grid=(bn, s_len // ts),
        in_specs=[pl.BlockSpec((1, ts, d), lambda b, i: (b + x_b0, i, 0)),
                  tile(d), tile(d), tile(o.shape[2]), memspec, memspec,
                  lay(wao), lay(wmo), lay(ln1), lay(wxq), lay(wxo), lay(ln2), lay(wr),
                  lay(wsgu), lay(wsd), _DEP_SPEC],
        out_specs=[tile(d // 2), tile(d),
                   pl.BlockSpec((1, N_EXPERTS, ts), lambda b, i: (b, 0, i))],
        out_shape=[jax.ShapeDtypeStruct((bn, s_len, d // 2), I32),
                   jax.ShapeDtypeStruct((bn, s_len, d), F32),
                   jax.ShapeDtypeStruct((bn, N_EXPERTS, s_len), F32)],
        compiler_params=pltpu.CompilerParams(
            dimension_semantics=("parallel", "arbitrary"), vmem_limit_bytes=VMEM_LIMIT),
        name="post_mixer",
    )(x, yag, sgb, o, kmem, vmem, wao, wmo, ln1, wxq, wxo, ln2, wr, wsgu, wsd, dep)


RANK_BLOCK = 512


def _route_kernel(lg_ref, rb_ref, eid_ref, rank_ref, gate_ref, cnt_ref):
    @pl.when(pl.program_id(0) == 0)
    def _():
        cnt_ref[...] = jnp.zeros_like(cnt_ref)

    n = lg_ref.shape[2]
    blk = min(RANK_BLOCK, n)
    picks, ids, chosen, gates = _route(lg_ref[0], rb_ref[0])

    sel_t = jnp.concatenate([c.astype(F32) for c in chosen], axis=0)
    before = (lax.broadcasted_iota(jnp.int32, (blk, blk), 0)
              < lax.broadcasted_iota(jnp.int32, (blk, blk), 1)).astype(BF16)
    running = cnt_ref[:, 0:1]
    ranks = []
    for j in range(n // blk):
        sb = sel_t[:, j * blk:(j + 1) * blk]
        ranks.append(_dot(sb.astype(BF16), before) + running)
        running = running + jnp.sum(sb, axis=1, keepdims=True)
    cnt_ref[...] = jnp.broadcast_to(running, cnt_ref.shape)
    rank_all = jnp.concatenate(ranks, axis=1)
    rank_g = [rank_all[g * GROUP_SIZE:(g + 1) * GROUP_SIZE] for g in range(N_GROUPS)]
    eid_ref[0] = jnp.concatenate(ids, axis=0)
    rank_ref[0] = jnp.concatenate([_pick_rows(h, rank_g) for h in picks], axis=0).astype(I32)
    gate_ref[0] = jnp.concatenate([_pick_rows(h, gates) for h in picks], axis=0)


def _route_call(l, logits, rb):
    bn, n_e, s_len = logits.shape
    pick = pl.BlockSpec((1, TOP_K, s_len), lambda b: (b, 0, 0))
    return pl.pallas_call(
        _route_kernel,
        grid=(bn,),
        in_specs=[pl.BlockSpec((1, n_e, s_len), lambda b: (b, 0, 0)),
                  pl.BlockSpec((1,) + rb.shape[1:], lambda b: (l, 0, 0))],
        out_specs=[pick, pick, pick, pl.BlockSpec((n_e, LANES), lambda b: (0, 0))],
        out_shape=[jax.ShapeDtypeStruct((bn, TOP_K, s_len), I32),
                   jax.ShapeDtypeStruct((bn, TOP_K, s_len), I32),
                   jax.ShapeDtypeStruct((bn, TOP_K, s_len), F32),
                   jax.ShapeDtypeStruct((n_e, LANES), F32)],
        compiler_params=pltpu.CompilerParams(
            dimension_semantics=("arbitrary",), vmem_limit_bytes=VMEM_LIMIT),
        name="moe_route",
    )(logits, rb)


def _slot_kernel(offs_ref, eid_ref, rank_ref, slot_ref):
    eid = eid_ref[0]
    acc = rank_ref[0]
    for e in range(N_EXPERTS):
        acc = acc + jnp.where(eid == e, offs_ref[e], 0)
    slot_ref[0] = acc


def _slots(offs, eid, rank):
    bn, k, s_len = eid.shape
    spec = pl.BlockSpec((1, k, s_len), lambda b, offs_ref: (b, 0, 0))
    return pl.pallas_call(
        _slot_kernel,
        grid_spec=pltpu.PrefetchScalarGridSpec(
            num_scalar_prefetch=1, grid=(bn,), in_specs=[spec, spec], out_specs=spec),
        out_shape=jax.ShapeDtypeStruct((bn, k, s_len), I32),
        name="moe_slots",
    )(offs, eid, rank)


def _sc_mesh():
    return plsc.VectorSubcoreMesh(core_axis_name="c", subcore_axis_name="s")


def _sc_dispatch(xp, slot_chunks, n_slots, dep):
    t, w = xp.shape
    n_chunks, k, ch = slot_chunks.shape
    per_worker = n_chunks // (SC_CORES * SC_SUBCORES)

    @functools.partial(
        pl.kernel, mesh=_sc_mesh(),
        out_type=jax.ShapeDtypeStruct((n_slots, w), I32),
        scratch_types=[pltpu.VMEM((k, ch), I32), pltpu.VMEM((ch, w), I32), pltpu.SemaphoreType.DMA],
        name="moe_dispatch")
    def body(x_hbm, slot_hbm, dep_hbm, out_hbm, idx_v, rows_v, sem):
        worker = lax.axis_index("s") * SC_CORES + lax.axis_index("c")

        @pl.loop(0, per_worker)
        def _(j):
            c = worker * per_worker + j
            pltpu.sync_copy(slot_hbm.at[c], idx_v)
            pltpu.sync_copy(x_hbm.at[pl.ds(c * ch, ch)], rows_v)
            copies = [pltpu.async_copy(rows_v, out_hbm.at[idx_v.at[kk]], sem) for kk in range(k)]
            for cp in copies:
                cp.wait()

    return body(xp, slot_chunks, dep)


def _sc_combine(ys, slot_chunks, t):
    w = ys.shape[1]
    n_chunks, k, ch = slot_chunks.shape
    per_worker = n_chunks // (SC_CORES * SC_SUBCORES)

    @functools.partial(
        pl.kernel, mesh=_sc_mesh(),
        out_type=jax.ShapeDtypeStruct((k, t, w), I32),
        scratch_types=[pltpu.VMEM((k, ch), I32), pltpu.VMEM((ch, w), I32), pltpu.VMEM((ch, w), I32),
                       pltpu.SemaphoreType.DMA, pltpu.SemaphoreType.DMA,
                       pltpu.SemaphoreType.DMA, pltpu.SemaphoreType.DMA],
        name="moe_combine")
    def body(y_hbm, slot_hbm, out_hbm, idx_v, rows0, rows1, g0, g1, w0, w1):
        worker = lax.axis_index("s") * SC_CORES + lax.axis_index("c")
        bufs, gsem, wsem = (rows0, rows1), (g0, g1), (w0, w1)

        @pl.loop(0, per_worker)
        def _(j):
            c = worker * per_worker + j
            pltpu.sync_copy(slot_hbm.at[c], idx_v)
            gathers = [None] * k
            writes = [None] * k
            gathers[0] = pltpu.async_copy(y_hbm.at[idx_v.at[0]], bufs[0], gsem[0])
            for kk in range(k):
                cur = kk % 2
                if kk + 1 < k:
                    if kk >= 1:
                        writes[kk - 1].wait()
                    gathers[kk + 1] = pltpu.async_copy(
                        y_hbm.at[idx_v.at[kk + 1]], bufs[1 - cur], gsem[1 - cur])
                gathers[kk].wait()
                writes[kk] = pltpu.async_copy(
                    bufs[cur], out_hbm.at[kk, pl.ds(c * ch, ch)], wsem[cur])
            writes[k - 2].wait()
            writes[k - 1].wait()

    return body(ys, slot_chunks)


def _experts_kernel(ti_ref, te_ref, nv_ref, eo_ref, ne_ref, nu_ref, xs_ref, wg_hbm, wu_hbm, wd_hbm,
                    dep_ref, ys_ref, wg_f, wu_f, wd_f, wg_b, wu_b, wd_b, sem, *, layer):
    j = pl.program_id(0)
    expert = te_ref[j]
    slot = eo_ref[j]
    prev = te_ref[jnp.maximum(j - 1, 0)]
    new_expert = jnp.logical_or(j == 0, expert != prev)

    def weight_copies(e, s):
        return [pltpu.make_async_copy(hbm.at[layer, e], buf.at[s], sem.at[s, i])
                for i, (hbm, buf) in enumerate(((wg_hbm, wg_f), (wu_hbm, wu_f), (wd_hbm, wd_f)))]

    @pl.when(j == 0)
    def _():
        for cp in weight_copies(expert, slot):
            cp.start()

    @pl.when(new_expert)
    def _():
        nxt = ne_ref[j]

        @pl.when(nxt >= 0)
        def _():
            for cp in weight_copies(nxt, 1 - slot):
                cp.start()

        for cp in weight_copies(expert, slot):
            cp.wait()
        wg_b[...] = wg_f[slot].astype(BF16)
        wu_b[...] = wu_f[slot].astype(BF16)
        wd_b[...] = wd_f[slot].astype(BF16)

    def ffn(n_rows):
        rows = pl.ds(0, n_rows)
        xb = _unpack_rows(xs_ref[rows, :]).astype(BF16)
        hid = _silu(_dot(xb, wg_b[...])) * _dot(xb, wu_b[...])
        ys_ref[rows, :] = _pack_rows(_dot(hid.astype(BF16), wd_b[...]))

    used = j < nu_ref[0]
    half = xs_ref.shape[0] // 2
    many = nv_ref[j] > half

    @pl.when(jnp.logical_and(used, many))
    def _():
        ffn(2 * half)

    @pl.when(jnp.logical_and(used, jnp.logical_not(many)))
    def _():
        ffn(half)


def _experts(l, xs, sched, w_gate, w_up, w_down, dep):
    n_slots, w = xs.shape
    n_tiles = n_slots // EXPERT_TILE
    d, f = w_gate.shape[2:]
    rows = pl.BlockSpec((EXPERT_TILE, w), lambda j, ti, *_: (ti[j], 0))
    hbm = pl.BlockSpec(memory_space=pl.ANY)
    return pl.pallas_call(
        functools.partial(_experts_kernel, layer=l),
        grid_spec=pltpu.PrefetchScalarGridSpec(
            num_scalar_prefetch=len(sched), grid=(n_tiles,),
            in_specs=[rows, hbm, hbm, hbm, _DEP_SPEC],
            out_specs=rows,
            scratch_shapes=[pltpu.VMEM((2, d, f), F32), pltpu.VMEM((2, d, f), F32),
                            pltpu.VMEM((2, f, d), F32),
                            pltpu.VMEM((d, f), BF16), pltpu.VMEM((d, f), BF16),
                            pltpu.VMEM((f, d), BF16), pltpu.SemaphoreType.DMA((2, 3))]),
        out_shape=jax.ShapeDtypeStruct((n_slots, w), I32),
        compiler_params=pltpu.CompilerParams(
            dimension_semantics=("arbitrary",), vmem_limit_bytes=VMEM_LIMIT),
        name="moe_experts",
    )(*sched, xs, w_gate, w_up, w_down, dep)


def _final_kernel(yg_ref, gate_ref, base_ref, ln_ref, *rest):
    out_ref = rest[-1]
    gates = gate_ref[0].T
    acc = base_ref[0]
    for k in range(yg_ref.shape[0]):
        acc = acc + gates[:, k:k + 1] * _unpack_rows(yg_ref[k, 0])
    out_ref[0] = _layer_norm(acc, ln_ref[0, 0:1], ln_ref[0, 1:2])


def _final(l, yg, gate8, base, ln3, tm, out_rows, out_b0, out_prev, dep):
    bn, s_len, d = base.shape
    k = yg.shape[0]
    in_specs = [pl.BlockSpec((k, 1, tm, d // 2), lambda b, i: (0, b, i, 0)),
                pl.BlockSpec((1, k, tm), lambda b, i: (b, 0, i)),
                pl.BlockSpec((1, tm, d), lambda b, i: (b, i, 0)),
                pl.BlockSpec((1,) + ln3.shape[1:], lambda b, i: (l, 0, 0)), _DEP_SPEC]
    args = [yg, gate8, base, ln3, dep]
    aliases = {}
    if out_prev is not None:
        in_specs.append(pl.BlockSpec(memory_space=pl.ANY))
        args.append(out_prev)
        aliases = {len(args) - 1: 0}
    return pl.pallas_call(
        _final_kernel,
        grid=(bn, s_len // tm),
        in_specs=in_specs,
        out_specs=pl.BlockSpec((1, tm, d), lambda b, i: (b + out_b0, i, 0)),
        out_shape=jax.ShapeDtypeStruct((out_rows, s_len, d), F32),
        input_output_aliases=aliases,
        compiler_params=pltpu.CompilerParams(
            dimension_semantics=("parallel", "parallel"), vmem_limit_bytes=VMEM_LIMIT),
        name="moe_final",
    )(*args)


def _moe_schedule(eid, rank, counts):
    bn, _, s_len = eid.shape
    t = bn * s_len
    n_tiles = (t * TOP_K) // EXPERT_TILE + N_EXPERTS

    cnt = counts[:, 0].astype(I32)
    padded = (cnt + EXPERT_TILE - 1) // EXPERT_TILE * EXPERT_TILE
    ends = jnp.cumsum(padded)
    offs = ends - padded
    n_used = ends[-1] // EXPERT_TILE
    ti = jnp.minimum(jnp.arange(n_tiles, dtype=I32), n_used - 1)
    te = jnp.sum((ends[None, :] <= (ti * EXPERT_TILE)[:, None]).astype(I32), axis=1)
    nv = jnp.take(cnt + offs, te) - ti * EXPERT_TILE
    nonempty = cnt > 0
    idx = jnp.arange(N_EXPERTS, dtype=I32)
    order = jnp.cumsum(nonempty.astype(I32)) - 1
    later = jnp.where(nonempty[None, :] & (idx[None, :] > idx[:, None]), idx[None, :], N_EXPERTS)
    succ = jnp.min(later, axis=1)
    succ = jnp.where(succ == N_EXPERTS, -1, succ).astype(I32)
    eo = jnp.take(order, te) % 2
    ne = jnp.take(succ, te)

    slots = _slots(offs, eid, rank)
    slot_chunks = slots.reshape(bn, TOP_K, s_len // SC_CHUNK, SC_CHUNK).transpose(0, 2, 1, 3)
    slot_chunks = slot_chunks.reshape(t // SC_CHUNK, TOP_K, SC_CHUNK)
    return slot_chunks, (ti, te, nv, eo, ne, n_used.reshape(1)), n_tiles * EXPERT_TILE


def _head_chunks(w, widths, n_heads, per_head):
    lead = w.shape[:-1]
    wh = w.reshape(lead + (n_heads, per_head))
    parts = [sign * wh[..., a:b] for a, b, sign in widths]
    used = sum(b - a for a, b, _ in widths)
    parts.append(jnp.zeros(lead + (n_heads, LANES - used), w.dtype))
    return jnp.concatenate(parts, axis=-1).reshape(lead + (n_heads * LANES,))


def kernel(x, mem, positions, w_in, w_conv, w_conv_out, q_norm, w_uq, kv_norm, w_uk, w_uv,
           w_attn_out, w_mix_out, ln1_g, ln1_b, w_xq, w_xk, w_xv, w_xo, ln2_g, ln2_b,
           w_router, router_bias, w_gate, w_up, w_down, ws_gate, ws_up, ws_down, ln3_g, ln3_b):
    bn, s_len, d = x.shape
    depth = w_in.shape[0]
    alpha = (2 * depth) ** 0.25
    ts = min(512, s_len)
    tq = min(512, s_len)
    half = QK_ROPE_DIM // 2
    nope, rope = QK_NOPE_DIM, QK_ROPE_DIM

    inv_freq = ROPE_BASE ** (-jnp.arange(half, dtype=F32) / half)
    ang = positions.astype(F32)[..., None] * inv_freq
    cos, sin = jnp.cos(ang), jnp.sin(ang)
    scale = (nope + rope) ** -0.5
    zeros = lambda w: jnp.zeros((bn, s_len, w), F32)
    tail = LANES - nope - rope
    tabs = jnp.concatenate([
        jnp.ones((bn, s_len, nope), F32), cos, cos, zeros(tail),
        zeros(nope), -sin, zeros(half), zeros(tail),
        zeros(nope), zeros(half), sin, zeros(tail)], axis=-1)

    sizes = (CONV_DIM, CONV_DIM, CONV_DIM, Q_RANK, KV_RANK, rope, d, d)
    offs = [0]
    for sz in sizes:
        offs.append(offs[-1] + sz)
    seg = lambda j: w_in[:, :, offs[j]:offs[j + 1]]
    zpad = lambda w: jnp.zeros(w_in.shape[:2] + (w,), w_in.dtype)
    wx = jnp.concatenate([w_in[:, :, :offs[5]], zpad(nope), seg(5), zpad(tail), w_in[:, :, offs[6]:]],
                         axis=-1).astype(BF16)

    per_q = nope + rope
    wq = _head_chunks(w_uq, [(0, per_q, scale)], MLA_HEADS, per_q).astype(BF16)
    wk = _head_chunks(w_uk, [(0, nope, 1.0)], MLA_HEADS, nope)
    wkv = jnp.concatenate([wk, w_uv], axis=-1).astype(BF16)

    wconv = w_conv[:, :, 0, :]
    wco = w_conv_out.astype(BF16)
    qn = q_norm[:, None, :]
    kvn = kv_norm[:, None, :]
    wao = w_attn_out.astype(BF16)
    wmo = w_mix_out.astype(BF16)
    ln1 = jnp.stack([ln1_g, ln1_b], axis=1)
    ln2 = jnp.stack([ln2_g, ln2_b], axis=1)
    ln3 = jnp.stack([ln3_g, ln3_b], axis=1)
    wxq = w_xq.astype(BF16)
    wxo = w_xo.astype(BF16)
    wr = jnp.swapaxes(w_router, 1, 2)
    rb = router_bias[:, :, None]
    wsgu = jnp.concatenate([ws_gate, ws_up], axis=-1).astype(BF16)
    wsd = ws_down.astype(BF16)

    kmem, vmem = _mem_kv(mem, w_xk.astype(BF16), w_xv.astype(BF16))

    assert bn % 2 == 0
    hb = bn // 2
    t = hb * s_len
    tm = min(512, s_len)

    def mix_attn(l, c, xc, dep):
        x_b0 = c * hb if l == 0 else 0
        yag, sgb, q, k, v = _mixer_in(l, hb, x_b0, c * hb, xc, tabs, wx, wconv, wco, qn, wq, kvn, wkv,
                                      min(1024, s_len), dep)
        return yag, sgb, _attention(q, k, v, tq, yag)

    def post_route(l, c, xc, ma, dep):
        x_b0 = c * hb if l == 0 else 0
        yag, sgb, o = ma
        x2p, base, logits = _post(l, hb, x_b0, c * hb, xc, yag, sgb, o, kmem, vmem, wao, wmo, ln1,
                                  wxq, wxo, ln2, wr, wsgu, wsd, min(1024, s_len), alpha, dep)
        eid, rank, gate8, counts = _route_call(l, logits, rb)
        slot_chunks, sched, n_slots = _moe_schedule(eid, rank, counts)
        return dict(x2p=x2p.reshape(t, d // 2), base=base, gate8=gate8, slot_chunks=slot_chunks,
                    sched=sched, n_slots=n_slots)

    def dispatch(st, dep):
        st["xs"] = _sc_dispatch(st["x2p"], st["slot_chunks"], st["n_slots"], dep)

    def experts(l, st, dep):
        st["ys"] = _experts(l, st["xs"], st["sched"], w_gate, w_up, w_down, dep)
        st["yg"] = _sc_combine(st["ys"], st["slot_chunks"], t).reshape(TOP_K, hb, s_len, d // 2)

    def final(l, c, st, dep, out_prev):
        if l == depth - 1:
            return _final(l, st["yg"], st["gate8"], st["base"], ln3, tm, bn, c * hb, out_prev, dep)
        return _final(l, st["yg"], st["gate8"], st["base"], ln3, tm, hb, 0, None, dep)

    xa = xb = x
    ma_a = mix_attn(0, 0, xa, x)
    st_a = post_route(0, 0, xa, ma_a, ma_a[2])
    dispatch(st_a, st_a["gate8"])
    for l in range(depth):
        more = l + 1 < depth
        ma_b = mix_attn(l, 1, xb, st_a["gate8"])
        experts(l, st_a, ma_b[2])
        st_b = post_route(l, 1, xb, ma_b, st_a["ys"])
        xa = final(l, 0, st_a, st_b["gate8"], None)
        dispatch(st_b, xa if more else st_b["gate8"])
        if more:
            ma_a = mix_attn(l + 1, 0, xa, xa)
        experts(l, st_b, ma_a[2] if more else xa)
        if more:
            st_a = post_route(l + 1, 0, xa, ma_a, st_b["ys"])
        xb = final(l, 1, st_b, st_a["gate8"] if more else st_b["ys"], xa if not more else None)
        if more:
            dispatch(st_a, xb)
    return xb
```

```python
import functools

import jax
import jax.numpy as jnp
from jax import lax
from jax.experimental import pallas as pl
from jax.experimental.pallas import tpu as pltpu
from jax.experimental.pallas import tpu_sc as plsc

CONV_DIM = 512
CONV_WIDTH = 3
MLA_HEADS = 8
QK_NOPE_DIM = 64
QK_ROPE_DIM = 32
V_HEAD_DIM = 64
Q_RANK = 384
KV_RANK = 256
ROPE_BASE = 10000.0
XATTN_HEADS = 4
N_EXPERTS = 64
TOP_K = 8
N_GROUPS = 8
TOPK_GROUPS = 4
GROUP_SIZE = N_EXPERTS // N_GROUPS
ROUTE_SCALE = 2.5
LN_EPS = 1e-5
RMS_EPS = 1e-6

LANES = 128
HALO_ROWS = 8
VMEM_LIMIT = 56 * 1024 * 1024
SC_CORES = 2
SC_SUBCORES = 16
SC_CHUNK = 64
EXPERT_TILE = 1024

BF16 = jnp.bfloat16
F32 = jnp.float32
I32 = jnp.int32


_DEP_SPEC = pl.BlockSpec(memory_space=pl.ANY)


def _dot(a, b):
    return jnp.dot(a, b, preferred_element_type=F32)


def _dot_nt(a, b):
    return lax.dot_general(a, b, (((1,), (1,)), ((), ())), preferred_element_type=F32)


def _layer_norm(x, g, b):
    mu = jnp.mean(x, axis=-1, keepdims=True)
    xc = x - mu
    var = jnp.mean(xc * xc, axis=-1, keepdims=True)
    return xc * lax.rsqrt(var + LN_EPS) * g + b


def _rms_norm(x, g):
    ms = jnp.mean(x * x, axis=-1, keepdims=True)
    return x * lax.rsqrt(ms + RMS_EPS) * g


def _silu(x):
    return x * jax.nn.sigmoid(x)


_HI_MASK = -65536


def _pack_rows(y):
    w = y.shape[1] // 2
    lo = lax.bitcast_convert_type(y[:, :w].astype(BF16).astype(F32), I32)
    hi = lax.bitcast_convert_type(y[:, w:].astype(BF16).astype(F32), I32)
    return lax.shift_right_logical(lo, 16) | (hi & _HI_MASK)


def _unpack_rows(p):
    lo = lax.bitcast_convert_type(lax.shift_left(p, 16), F32)
    hi = lax.bitcast_convert_type(p & _HI_MASK, F32)
    return jnp.concatenate([lo, hi], axis=-1)


_C_BCH = 0
_C_CQ = _C_BCH + 3 * CONV_DIM
_C_CKV = _C_CQ + Q_RANK
_C_KR = _C_CKV + KV_RANK
_C_GA = _C_KR + LANES
_C_GB = _C_GA + 1024


def _mixer_in_kernel(x_ref, xh_ref, tab_ref, wx_ref, wconv_ref, wco_ref, qn_ref, wq_ref,
                     kvn_ref, wkv_ref, dep_ref,
                     yag_ref, sgb_ref, q_ref, k_ref, v_ref, *, d_model):
    i = pl.program_id(1)
    d = d_model
    ts = x_ref.shape[1]
    xb = x_ref[0].astype(BF16)

    bch = _dot(xb, wx_ref[:, _C_BCH:_C_BCH + 3 * CONV_DIM])
    b_gate = bch[:, :CONV_DIM]
    u = bch[:, CONV_DIM:2 * CONV_DIM] * bch[:, 2 * CONV_DIM:]
    xh = xh_ref[0].astype(BF16)
    hp = _dot(xh, wx_ref[:, _C_BCH + CONV_DIM:_C_BCH + 3 * CONV_DIM])
    uh = hp[:, :CONV_DIM] * hp[:, CONV_DIM:]
    uh = jnp.where(i == 0, 0.0, uh)
    row = lax.broadcasted_iota(jnp.int32, (ts, CONV_DIM), 0)
    u1 = jnp.where(row == 0, uh[HALO_ROWS - 1:HALO_ROWS], pltpu.roll(u, 1, 0))
    u2 = jnp.where(row == 0, uh[HALO_ROWS - 2:HALO_ROWS - 1],
                   jnp.where(row == 1, uh[HALO_ROWS - 1:HALO_ROWS], pltpu.roll(u, 2, 0)))
    wc = wconv_ref[0]
    z = wc[0:1] * u2 + wc[1:2] * u1 + wc[2:3] * u
    y_a = _dot((b_gate * z).astype(BF16), wco_ref[0])
    g_a = _dot(xb, wx_ref[:, _C_GA:_C_GA + d])
    yag_ref[0] = (jax.nn.sigmoid(g_a) * y_a).astype(BF16)
    g_b = _dot(xb, wx_ref[:, _C_GB:_C_GB + d])
    sgb_ref[0] = jax.nn.sigmoid(g_b).astype(BF16)

    tab = tab_ref[0]
    t_a, t_m, t_p = tab[:, 0:LANES], tab[:, LANES:2 * LANES], tab[:, 2 * LANES:3 * LANES]
    half = QK_ROPE_DIM // 2

    def rotate(c):
        return (c * t_a + pltpu.roll(c, LANES - half, 1) * t_m + pltpu.roll(c, half, 1) * t_p)

    c_q = _dot(xb, wx_ref[:, _C_CQ:_C_CQ + Q_RANK])
    cqn = _rms_norm(c_q, qn_ref[0]).astype(BF16)
    hw = MLA_HEADS * LANES
    q = _dot(cqn, wq_ref[0])
    for h in range(MLA_HEADS):
        sl = slice(h * LANES, (h + 1) * LANES)
        q_ref[0, :, sl] = rotate(q[:, sl]).astype(BF16)

    c_kv = _dot(xb, wx_ref[:, _C_CKV:_C_CKV + KV_RANK])
    ckvn = _rms_norm(c_kv, kvn_ref[0]).astype(BF16)
    kr_rot = rotate(_dot(xb, wx_ref[:, _C_KR:_C_KR + LANES]))
    k_nope = _dot(ckvn, wkv_ref[0, :, :hw])
    for h in range(MLA_HEADS):
        sl = slice(h * LANES, (h + 1) * LANES)
        k_ref[0, :, sl] = (k_nope[:, sl] + kr_rot).astype(BF16)
    v_ref[0] = _dot(ckvn, wkv_ref[0, :, hw:]).astype(BF16)


def _mixer_in(l, bn, x_b0, tab_b0, x, tabs, wx, wconv, wco, qn, wq, kvn, wkv, ts, dep):
    _, s_len, d = x.shape
    n_t = s_len // ts
    hw = MLA_HEADS * LANES
    vw = MLA_HEADS * V_HEAD_DIM
    tile = lambda w: pl.BlockSpec((1, ts, w), lambda b, i: (b, i, 0))
    once = pl.Buffered(1)
    lay = lambda a: pl.BlockSpec((1,) + a.shape[1:], lambda b, i: (l,) + (0,) * (a.ndim - 1),
                                 pipeline_mode=once)
    halo = pl.BlockSpec((1, HALO_ROWS, d),
                        lambda b, i: (b + x_b0, jnp.maximum(i * (ts // HALO_ROWS) - 1, 0), 0))
    return pl.pallas_call(
        functools.partial(_mixer_in_kernel, d_model=d),
        grid=(bn, n_t),
        in_specs=[pl.BlockSpec((1, ts, d), lambda b, i: (b + x_b0, i, 0)), halo,
                  pl.BlockSpec((1, ts, 3 * LANES), lambda b, i: (b + tab_b0, i, 0)),
                  pl.BlockSpec((None,) + wx.shape[1:], lambda b, i: (l, 0, 0), pipeline_mode=once),
                  lay(wconv), lay(wco), lay(qn), lay(wq), lay(kvn), lay(wkv), _DEP_SPEC],
        out_specs=[tile(d), tile(d), tile(hw), tile(hw), tile(vw)],
        out_shape=[jax.ShapeDtypeStruct((bn, s_len, d), BF16),
                   jax.ShapeDtypeStruct((bn, s_len, d), BF16),
                   jax.ShapeDtypeStruct((bn, s_len, hw), BF16),
                   jax.ShapeDtypeStruct((bn, s_len, hw), BF16),
                   jax.ShapeDtypeStruct((bn, s_len, vw), BF16)],
        compiler_params=pltpu.CompilerParams(
            dimension_semantics=("parallel", "arbitrary"), vmem_limit_bytes=VMEM_LIMIT),
        name="mixer_in",
    )(x, x, tabs, wx, wconv, wco, qn, wq, kvn, wkv, dep)


def _attn_kernel(q_ref, k_ref, v_ref, dep_ref, o_ref, *, tq):
    s_len = q_ref.shape[1]
    n_q = s_len // tq
    lane = lax.broadcasted_iota(jnp.int32, (tq, 2 * V_HEAD_DIM), 1)
    causal = (lax.broadcasted_iota(jnp.int32, (tq, tq), 1)
              <= lax.broadcasted_iota(jnp.int32, (tq, tq), 0))
    for qi in range(n_q):
        past = qi * tq
        rows = slice(past, past + tq)
        outs = []
        for h in range(2):
            sl = slice(h * LANES, (h + 1) * LANES)
            qh = q_ref[0, rows, sl]
            s_d = jnp.where(causal, _dot_nt(qh, k_ref[0, rows, sl]), -jnp.inf)
            m = jnp.max(s_d, axis=-1, keepdims=True)
            if past:
                s_p = _dot_nt(qh, k_ref[0, :past, sl])
                m = jnp.maximum(m, jnp.max(s_p, axis=-1, keepdims=True))
            p_d = jnp.exp(s_d - m)
            den = jnp.sum(p_d, axis=-1, keepdims=True)
            acc = _dot(p_d.astype(BF16), v_ref[0, rows, :])
            if past:
                p_p = jnp.exp(s_p - m)
                den = den + jnp.sum(p_p, axis=-1, keepdims=True)
                acc = acc + _dot(p_p.astype(BF16), v_ref[0, :past, :])
            outs.append(acc / den)
        o_ref[0, rows, :] = jnp.where(lane < V_HEAD_DIM, outs[0], outs[1]).astype(BF16)


def _attention(q, k, v, tq, dep):
    bn, s_len, _ = q.shape
    n_pairs = MLA_HEADS // 2
    return pl.pallas_call(
        functools.partial(_attn_kernel, tq=tq),
        grid=(bn, n_pairs),
        in_specs=[pl.BlockSpec((1, s_len, 2 * LANES), lambda b, h: (b, 0, h)),
                  pl.BlockSpec((1, s_len, 2 * LANES), lambda b, h: (b, 0, h)),
                  pl.BlockSpec((1, s_len, 2 * V_HEAD_DIM), lambda b, h: (b, 0, h)), _DEP_SPEC],
        out_specs=pl.BlockSpec((1, s_len, 2 * V_HEAD_DIM), lambda b, h: (b, 0, h)),
        out_shape=jax.ShapeDtypeStruct((bn, s_len, MLA_HEADS * V_HEAD_DIM), BF16),
        compiler_params=pltpu.CompilerParams(
            dimension_semantics=("parallel", "parallel"), vmem_limit_bytes=VMEM_LIMIT),
        name="mla_attention",
    )(q, k, v, dep)


def _mem_kv_kernel(mem_ref, wk_ref, wv_ref, k_ref, v_ref):
    mb = mem_ref[0].astype(BF16)
    k_ref[0, 0] = _dot(mb, wk_ref[0]).astype(BF16)
    v_ref[0, 0] = _dot(mb, wv_ref[0]).astype(BF16)


def _mem_kv(mem, w_xk, w_xv):
    bn, m_len, d = mem.shape
    n_l = w_xk.shape[0]
    wspec = pl.BlockSpec((1, d, d), lambda l, b: (l, 0, 0))
    ospec = pl.BlockSpec((1, 1, m_len, d), lambda l, b: (l, b, 0, 0))
    return pl.pallas_call(
        _mem_kv_kernel,
        grid=(n_l, bn),
        in_specs=[pl.BlockSpec((1, m_len, d), lambda l, b: (b, 0, 0)), wspec, wspec],
        out_specs=[ospec, ospec],
        out_shape=[jax.ShapeDtypeStruct((n_l, bn, m_len, d), BF16)] * 2,
        compiler_params=pltpu.CompilerParams(
            dimension_semantics=("arbitrary", "arbitrary"), vmem_limit_bytes=VMEM_LIMIT),
        name="mem_kv",
    )(mem, w_xk, w_xv)


def _first_argmax(vals, idx):
    m = jnp.max(functools.reduce(jnp.maximum, vals), axis=0, keepdims=True)
    big = jnp.int32(1 << 20)
    cand = functools.reduce(jnp.minimum, [jnp.where(v == m, ix, big) for v, ix in zip(vals, idx)])
    first = jnp.min(cand, axis=0, keepdims=True)
    return [ix == first for ix in idx], m, first


def _route(logits_t, bias):
    n = logits_t.shape[1]
    scores = jax.nn.sigmoid(logits_t)
    sel = scores + bias
    sub = lax.broadcasted_iota(jnp.int32, (GROUP_SIZE, n), 0)
    neg = -jnp.inf
    sel_g = [sel[g * GROUP_SIZE:(g + 1) * GROUP_SIZE] for g in range(N_GROUPS)]
    gs = []
    for g in range(N_GROUPS):
        (hot,), m1, _ = _first_argmax([sel_g[g]], [sub])
        m2 = jnp.max(jnp.where(hot, neg, sel_g[g]), axis=0, keepdims=True)
        gs.append(m1 + m2)
    gsv = jnp.concatenate(gs, axis=0)
    gmask = jnp.zeros(gsv.shape, F32)
    for _ in range(TOPK_GROUPS):
        (hot,), _m, _ = _first_argmax([gsv], [sub])
        gmask = jnp.where(hot, 1.0, gmask)
        gsv = jnp.where(hot, neg, gsv)
    msel = [jnp.where(gmask[g:g + 1] > 0.0, sel_g[g], neg) for g in range(N_GROUPS)]
    eidx = [sub + g * GROUP_SIZE for g in range(N_GROUPS)]
    picks, ids = [], []
    for _ in range(TOP_K):
        hots, _m, first = _first_argmax(msel, eidx)
        picks.append(hots)
        ids.append(first)
        msel = [jnp.where(h, neg, v) for h, v in zip(hots, msel)]
    chosen = [functools.reduce(jnp.logical_or, [p[g] for p in picks]) for g in range(N_GROUPS)]
    sc_g = [scores[g * GROUP_SIZE:(g + 1) * GROUP_SIZE] for g in range(N_GROUPS)]
    picked = [jnp.where(c, s, 0.0) for c, s in zip(chosen, sc_g)]
    wsum = jnp.sum(functools.reduce(lambda a, b: a + b, picked), axis=0, keepdims=True)
    gates = [p / wsum * ROUTE_SCALE for p in picked]
    return picks, ids, chosen, gates


def _pick_rows(hots, vals):
    acc = functools.reduce(lambda a, b: a + b, [jnp.where(h, v, 0.0) for h, v in zip(hots, vals)])
    return jnp.sum(acc, axis=0, keepdims=True)


def _post_kernel(x_ref, yag_ref, sgb_ref, o_ref, km_ref, vm_ref, wao_ref, wmo_ref, ln1_ref,
                 wxq_ref, wxo_ref, ln2_ref, wr_ref, wsgu_ref, wsd_ref, dep_ref,
                 x2p_ref, base_ref, logit_ref, *, alpha):
    d = x_ref.shape[2]
    x = x_ref[0]
    y_b = _dot(o_ref[0], wao_ref[0])
    y = yag_ref[0].astype(F32) + sgb_ref[0].astype(F32) * y_b
    mix = _dot(y.astype(BF16), wmo_ref[0])
    x1 = _layer_norm(alpha * x + mix, ln1_ref[0, 0:1], ln1_ref[0, 1:2])

    hd = d // XATTN_HEADS
    xq = (_dot(x1.astype(BF16), wxq_ref[0]) * (hd ** -0.5)).astype(BF16)
    heads = []
    for h in range(XATTN_HEADS):
        sl = slice(h * hd, (h + 1) * hd)
        s = _dot_nt(xq[:, sl], km_ref[0, 0, :, sl])
        m = jnp.max(s, axis=-1, keepdims=True)
        p = jnp.exp(s - m)
        den = jnp.sum(p, axis=-1, keepdims=True)
        heads.append((_dot(p.astype(BF16), vm_ref[0, 0, :, sl]) / den).astype(BF16))
    xat = _dot(jnp.concatenate(heads, axis=-1), wxo_ref[0])
    x2 = _layer_norm(alpha * x1 + xat, ln2_ref[0, 0:1], ln2_ref[0, 1:2])

    x2b = x2.astype(BF16)
    x2l = (x2 - x2b.astype(F32)).astype(BF16)
    wr = wr_ref[0]
    wrh = wr.astype(BF16)
    wrl = (wr - wrh.astype(F32)).astype(BF16)
    logit_ref[0] = _dot_nt(wrh, x2b) + (_dot_nt(wrh, x2l) + _dot_nt(wrl, x2b))

    x2p_ref[0] = _pack_rows(x2)
    sh = _dot(x2b, wsgu_ref[0])
    f = sh.shape[1] // 2
    hid = _silu(sh[:, :f]) * sh[:, f:]
    base_ref[0] = alpha * x2 + _dot(hid.astype(BF16), wsd_ref[0])


def _post(l, bn, x_b0, mem_b0, x, yag, sgb, o, kmem, vmem, wao, wmo, ln1, wxq, wxo, ln2, wr,
          wsgu, wsd, ts, alpha, dep):
    _, s_len, d = x.shape
    m_len = kmem.shape[2]
    tile = lambda w: pl.BlockSpec((1, ts, w), lambda b, i: (b, i, 0))
    lay = lambda a: pl.BlockSpec((1,) + a.shape[1:], lambda b, i: (l,) + (0,) * (a.ndim - 1),
                                 pipeline_mode=pl.Buffered(1))
    memspec = pl.BlockSpec((1, 1, m_len, d), lambda b, i: (l, b + mem_b0, 0, 0))
    return pl.pallas_call(
        functools.partial(_post_kernel, alpha=alpha),
        grid=(bn, s_len // ts),
        in_specs=[pl.BlockSpec((1, ts, d), lambda b, i: (b + x_b0, i, 0)),
                  tile(d), tile(d), tile(o.shape[2]), memspec, memspec,
                  lay(wao), lay(wmo), lay(ln1), lay(wxq), lay(wxo), lay(ln2), lay(wr),
                  lay(wsgu), lay(wsd), _DEP_SPEC],
        out_specs=[tile(d // 2), tile(d),
                   pl.BlockSpec((1, N_EXPERTS, ts), lambda b, i: (b, 0, i))],
        out_shape=[jax.ShapeDtypeStruct((bn, s_len, d // 2), I32),
                   jax.ShapeDtypeStruct((bn, s_len, d), F32),
                   jax.ShapeDtypeStruct((bn, N_EXPERTS, s_len), F32)],
        compiler_params=pltpu.CompilerParams(
            dimension_semantics=("parallel", "arbitrary"), vmem_limit_bytes=VMEM_LIMIT),
        name="post_mixer",
    )(x, yag, sgb, o, kmem, vmem, wao, wmo, ln1, wxq, wxo, ln2, wr, wsgu, wsd, dep)


RANK_BLOCK = 512


def _route_kernel(lg_ref, rb_ref, eid_ref, rank_ref, gate_ref, cnt_ref):
    @pl.when(pl.program_id(0) == 0)
    def _():
        cnt_ref[...] = jnp.zeros_like(cnt_ref)

    n = lg_ref.shape[2]
    blk = min(RANK_BLOCK, n)
    picks, ids, chosen, gates = _route(lg_ref[0], rb_ref[0])

    sel_t = jnp.concatenate([c.astype(F32) for c in chosen], axis=0)
    before = (lax.broadcasted_iota(jnp.int32, (blk, blk), 0)
              < lax.broadcasted_iota(jnp.int32, (blk, blk), 1)).astype(BF16)
    running = cnt_ref[:, 0:1]
    ranks = []
    for j in range(n // blk):
        sb = sel_t[:, j * blk:(j + 1) * blk]
        ranks.append(_dot(sb.astype(BF16), before) + running)
        running = running + jnp.sum(sb, axis=1, keepdims=True)
    cnt_ref[...] = jnp.broadcast_to(running, cnt_ref.shape)
    rank_all = jnp.concatenate(ranks, axis=1)
    rank_g = [rank_all[g * GROUP_SIZE:(g + 1) * GROUP_SIZE] for g in range(N_GROUPS)]
    eid_ref[0] = jnp.concatenate(ids, axis=0)
    rank_ref[0] = jnp.concatenate([_pick_rows(h, rank_g) for h in picks], axis=0).astype(I32)
    gate_ref[0] = jnp.concatenate([_pick_rows(h, gates) for h in picks], axis=0)


def _route_call(l, logits, rb):
    bn, n_e, s_len = logits.shape
    pick = pl.BlockSpec((1, TOP_K, s_len), lambda b: (b, 0, 0))
    return pl.pallas_call(
        _route_kernel,
        grid=(bn,),
        in_specs=[pl.BlockSpec((1, n_e, s_len), lambda b: (b, 0, 0)),
                  pl.BlockSpec((1,) + rb.shape[1:], lambda b: (l, 0, 0))],
        out_specs=[pick, pick, pick, pl.BlockSpec((n_e, LANES), lambda b: (0, 0))],
        out_shape=[jax.ShapeDtypeStruct((bn, TOP_K, s_len), I32),
                   jax.ShapeDtypeStruct((bn, TOP_K, s_len), I32),
                   jax.ShapeDtypeStruct((bn, TOP_K, s_len), F32),
                   jax.ShapeDtypeStruct((n_e, LANES), F32)],
        compiler_params=pltpu.CompilerParams(
            dimension_semantics=("arbitrary",), vmem_limit_bytes=VMEM_LIMIT),
        name="moe_route",
    )(logits, rb)


def _slot_kernel(offs_ref, eid_ref, rank_ref, slot_ref):
    eid = eid_ref[0]
    acc = rank_ref[0]
    for e in range(N_EXPERTS):
        acc = acc + jnp.where(eid == e, offs_ref[e], 0)
    slot_ref[0] = acc


def _slots(offs, eid, rank):
    bn, k, s_len = eid.shape
    spec = pl.BlockSpec((1, k, s_len), lambda b, offs_ref: (b, 0, 0))
    return pl.pallas_call(
        _slot_kernel,
        grid_spec=pltpu.PrefetchScalarGridSpec(
            num_scalar_prefetch=1, grid=(bn,), in_specs=[spec, spec], out_specs=spec),
        out_shape=jax.ShapeDtypeStruct((bn, k, s_len), I32),
        name="moe_slots",
    )(offs, eid, rank)


def _sc_mesh():
    return plsc.VectorSubcoreMesh(core_axis_name="c", subcore_axis_name="s")


def _sc_dispatch(xp, slot_chunks, n_slots, dep):
    t, w = xp.shape
    n_chunks, k, ch = slot_chunks.shape
    per_worker = n_chunks // (SC_CORES * SC_SUBCORES)

    @functools.partial(
        pl.kernel, mesh=_sc_mesh(),
        out_type=jax.ShapeDtypeStruct((n_slots, w), I32),
        scratch_types=[pltpu.VMEM((k, ch), I32), pltpu.VMEM((ch, w), I32), pltpu.SemaphoreType.DMA],
        name="moe_dispatch")
    def body(x_hbm, slot_hbm, dep_hbm, out_hbm, idx_v, rows_v, sem):
        worker = lax.axis_index("s") * SC_CORES + lax.axis_index("c")

        @pl.loop(0, per_worker)
        def _(j):
            c = worker * per_worker + j
            pltpu.sync_copy(slot_hbm.at[c], idx_v)
            pltpu.sync_copy(x_hbm.at[pl.ds(c * ch, ch)], rows_v)
            copies = [pltpu.async_copy(rows_v, out_hbm.at[idx_v.at[kk]], sem) for kk in range(k)]
            for cp in copies:
                cp.wait()

    return body(xp, slot_chunks, dep)


def _sc_combine(ys, slot_chunks, t):
    w = ys.shape[1]
    n_chunks, k, ch = slot_chunks.shape
    per_worker = n_chunks // (SC_CORES * SC_SUBCORES)

    @functools.partial(
        pl.kernel, mesh=_sc_mesh(),
        out_type=jax.ShapeDtypeStruct((k, t, w), I32),
        scratch_types=[pltpu.VMEM((k, ch), I32), pltpu.VMEM((ch, w), I32), pltpu.VMEM((ch, w), I32),
                       pltpu.SemaphoreType.DMA, pltpu.SemaphoreType.DMA,
                       pltpu.SemaphoreType.DMA, pltpu.SemaphoreType.DMA],
        name="moe_combine")
    def body(y_hbm, slot_hbm, out_hbm, idx_v, rows0, rows1, g0, g1, w0, w1):
        worker = lax.axis_index("s") * SC_CORES + lax.axis_index("c")
        bufs, gsem, wsem = (rows0, rows1), (g0, g1), (w0, w1)

        @pl.loop(0, per_worker)
        def _(j):
            c = worker * per_worker + j
            pltpu.sync_copy(slot_hbm.at[c], idx_v)
            gathers = [None] * k
            writes = [None] * k
            gathers[0] = pltpu.async_copy(y_hbm.at[idx_v.at[0]], bufs[0], gsem[0])
            for kk in range(k):
                cur = kk % 2
                if kk + 1 < k:
                    if kk >= 1:
                        writes[kk - 1].wait()
                    gathers[kk + 1] = pltpu.async_copy(
                        y_hbm.at[idx_v.at[kk + 1]], bufs[1 - cur], gsem[1 - cur])
                gathers[kk].wait()
                writes[kk] = pltpu.async_copy(
                    bufs[cur], out_hbm.at[kk, pl.ds(c * ch, ch)], wsem[cur])
            writes[k - 2].wait()
            writes[k - 1].wait()

    return body(ys, slot_chunks)


def _experts_kernel(ti_ref, te_ref, nv_ref, eo_ref, ne_ref, nu_ref, xs_ref, wg_hbm, wu_hbm, wd_hbm,
                    dep_ref, ys_ref, wg_f, wu_f, wd_f, wg_b, wu_b, wd_b, sem, *, layer):
    j = pl.program_id(0)
    expert = te_ref[j]
    slot = eo_ref[j]
    prev = te_ref[jnp.maximum(j - 1, 0)]
    new_expert = jnp.logical_or(j == 0, expert != prev)

    def weight_copies(e, s):
        return [pltpu.make_async_copy(hbm.at[layer, e], buf.at[s], sem.at[s, i])
                for i, (hbm, buf) in enumerate(((wg_hbm, wg_f), (wu_hbm, wu_f), (wd_hbm, wd_f)))]

    @pl.when(j == 0)
    def _():
        for cp in weight_copies(expert, slot):
            cp.start()

    @pl.when(new_expert)
    def _():
        nxt = ne_ref[j]

        @pl.when(nxt >= 0)
        def _():
            for cp in weight_copies(nxt, 1 - slot):
                cp.start()

        for cp in weight_copies(expert, slot):
            cp.wait()
        wg_b[...] = wg_f[slot].astype(BF16)
        wu_b[...] = wu_f[slot].astype(BF16)
        wd_b[...] = wd_f[slot].astype(BF16)

    def ffn(n_rows):
        rows = pl.ds(0, n_rows)
        xb = _unpack_rows(xs_ref[rows, :]).astype(BF16)
        hid = _silu(_dot(xb, wg_b[...])) * _dot(xb, wu_b[...])
        ys_ref[rows, :] = _pack_rows(_dot(hid.astype(BF16), wd_b[...]))

    used = j < nu_ref[0]
    half = xs_ref.shape[0] // 2
    many = nv_ref[j] > half

    @pl.when(jnp.logical_and(used, many))
    def _():
        ffn(2 * half)

    @pl.when(jnp.logical_and(used, jnp.logical_not(many)))
    def _():
        ffn(half)


def _experts(l, xs, sched, w_gate, w_up, w_down, dep):
    n_slots, w = xs.shape
    n_tiles = n_slots // EXPERT_TILE
    d, f = w_gate.shape[2:]
    rows = pl.BlockSpec((EXPERT_TILE, w), lambda j, ti, *_: (ti[j], 0))
    hbm = pl.BlockSpec(memory_space=pl.ANY)
    return pl.pallas_call(
        functools.partial(_experts_kernel, layer=l),
        grid_spec=pltpu.PrefetchScalarGridSpec(
            num_scalar_prefetch=len(sched), grid=(sched[-1][0],),
            in_specs=[rows, hbm, hbm, hbm, _DEP_SPEC],
            out_specs=rows,
            scratch_shapes=[pltpu.VMEM((2, d, f), F32), pltpu.VMEM((2, d, f), F32),
                            pltpu.VMEM((2, f, d), F32),
                            pltpu.VMEM((d, f), BF16), pltpu.VMEM((d, f), BF16),
                            pltpu.VMEM((f, d), BF16), pltpu.SemaphoreType.DMA((2, 3))]),
        out_shape=jax.ShapeDtypeStruct((n_slots, w), I32),
        compiler_params=pltpu.CompilerParams(
            dimension_semantics=("arbitrary",), vmem_limit_bytes=VMEM_LIMIT),
        name="moe_experts",
    )(*sched, xs, w_gate, w_up, w_down, dep)


def _final_kernel(yg_ref, gate_ref, base_ref, ln_ref, *rest):
    out_ref = rest[-1]
    gates = gate_ref[0].T
    acc = base_ref[0]
    for k in range(yg_ref.shape[0]):
        acc = acc + gates[:, k:k + 1] * _unpack_rows(yg_ref[k, 0])
    out_ref[0] = _layer_norm(acc, ln_ref[0, 0:1], ln_ref[0, 1:2])


def _final(l, yg, gate8, base, ln3, tm, out_rows, out_b0, out_prev, dep):
    bn, s_len, d = base.shape
    k = yg.shape[0]
    in_specs = [pl.BlockSpec((k, 1, tm, d // 2), lambda b, i: (0, b, i, 0)),
                pl.BlockSpec((1, k, tm), lambda b, i: (b, 0, i)),
                pl.BlockSpec((1, tm, d), lambda b, i: (b, i, 0)),
                pl.BlockSpec((1,) + ln3.shape[1:], lambda b, i: (l, 0, 0)), _DEP_SPEC]
    args = [yg, gate8, base, ln3, dep]
    aliases = {}
    if out_prev is not None:
        in_specs.append(pl.BlockSpec(memory_space=pl.ANY))
        args.append(out_prev)
        aliases = {len(args) - 1: 0}
    return pl.pallas_call(
        _final_kernel,
        grid=(bn, s_len // tm),
        in_specs=in_specs,
        out_specs=pl.BlockSpec((1, tm, d), lambda b, i: (b + out_b0, i, 0)),
        out_shape=jax.ShapeDtypeStruct((out_rows, s_len, d), F32),
        input_output_aliases=aliases,
        compiler_params=pltpu.CompilerParams(
            dimension_semantics=("parallel", "parallel"), vmem_limit_bytes=VMEM_LIMIT),
        name="moe_final",
    )(*args)


def _moe_schedule(eid, rank, counts):
    bn, _, s_len = eid.shape
    t = bn * s_len
    n_tiles = (t * TOP_K) // EXPERT_TILE + N_EXPERTS

    cnt = counts[:, 0].astype(I32)
    padded = (cnt + EXPERT_TILE - 1) // EXPERT_TILE * EXPERT_TILE
    ends = jnp.cumsum(padded)
    offs = ends - padded
    n_used = ends[-1] // EXPERT_TILE
    ti = jnp.minimum(jnp.arange(n_tiles, dtype=I32), n_used - 1)
    te = jnp.sum((ends[None, :] <= (ti * EXPERT_TILE)[:, None]).astype(I32), axis=1)
    nv = jnp.take(cnt + offs, te) - ti * EXPERT_TILE
    nonempty = cnt > 0
    idx = jnp.arange(N_EXPERTS, dtype=I32)
    order = jnp.cumsum(nonempty.astype(I32)) - 1
    later = jnp.where(nonempty[None, :] & (idx[None, :] > idx[:, None]), idx[None, :], N_EXPERTS)
    succ = jnp.min(later, axis=1)
    succ = jnp.where(succ == N_EXPERTS, -1, succ).astype(I32)
    eo = jnp.take(order, te) % 2
    ne = jnp.take(succ, te)

    slots = _slots(offs, eid, rank)
    slot_chunks = slots.reshape(bn, TOP_K, s_len // SC_CHUNK, SC_CHUNK).transpose(0, 2, 1, 3)
    slot_chunks = slot_chunks.reshape(t // SC_CHUNK, TOP_K, SC_CHUNK)
    return slot_chunks, (ti, te, nv, eo, ne, n_used.reshape(1)), n_tiles * EXPERT_TILE


def _head_chunks(w, widths, n_heads, per_head):
    lead = w.shape[:-1]
    wh = w.reshape(lead + (n_heads, per_head))
    parts = [sign * wh[..., a:b] for a, b, sign in widths]
    used = sum(b - a for a, b, _ in widths)
    parts.append(jnp.zeros(lead + (n_heads, LANES - used), w.dtype))
    return jnp.concatenate(parts, axis=-1).reshape(lead + (n_heads * LANES,))


def kernel(x, mem, positions, w_in, w_conv, w_conv_out, q_norm, w_uq, kv_norm, w_uk, w_uv,
           w_attn_out, w_mix_out, ln1_g, ln1_b, w_xq, w_xk, w_xv, w_xo, ln2_g, ln2_b,
           w_router, router_bias, w_gate, w_up, w_down, ws_gate, ws_up, ws_down, ln3_g, ln3_b):
    bn, s_len, d = x.shape
    depth = w_in.shape[0]
    alpha = (2 * depth) ** 0.25
    ts = min(512, s_len)
    tq = min(512, s_len)
    half = QK_ROPE_DIM // 2
    nope, rope = QK_NOPE_DIM, QK_ROPE_DIM

    inv_freq = ROPE_BASE ** (-jnp.arange(half, dtype=F32) / half)
    ang = positions.astype(F32)[..., None] * inv_freq
    cos, sin = jnp.cos(ang), jnp.sin(ang)
    scale = (nope + rope) ** -0.5
    zeros = lambda w: jnp.zeros((bn, s_len, w), F32)
    tail = LANES - nope - rope
    tabs = jnp.concatenate([
        jnp.ones((bn, s_len, nope), F32), cos, cos, zeros(tail),
        zeros(nope), -sin, zeros(half), zeros(tail),
        zeros(nope), zeros(half), sin, zeros(tail)], axis=-1)

    sizes = (CONV_DIM, CONV_DIM, CONV_DIM, Q_RANK, KV_RANK, rope, d, d)
    offs = [0]
    for sz in sizes:
        offs.append(offs[-1] + sz)
    seg = lambda j: w_in[:, :, offs[j]:offs[j + 1]]
    zpad = lambda w: jnp.zeros(w_in.shape[:2] + (w,), w_in.dtype)
    wx = jnp.concatenate([w_in[:, :, :offs[5]], zpad(nope), seg(5), zpad(tail), w_in[:, :, offs[6]:]],
                         axis=-1).astype(BF16)

    per_q = nope + rope
    wq = _head_chunks(w_uq, [(0, per_q, scale)], MLA_HEADS, per_q).astype(BF16)
    wk = _head_chunks(w_uk, [(0, nope, 1.0)], MLA_HEADS, nope)
    wkv = jnp.concatenate([wk, w_uv], axis=-1).astype(BF16)

    wconv = w_conv[:, :, 0, :]
    wco = w_conv_out.astype(BF16)
    qn = q_norm[:, None, :]
    kvn = kv_norm[:, None, :]
    wao = w_attn_out.astype(BF16)
    wmo = w_mix_out.astype(BF16)
    ln1 = jnp.stack([ln1_g, ln1_b], axis=1)
    ln2 = jnp.stack([ln2_g, ln2_b], axis=1)
    ln3 = jnp.stack([ln3_g, ln3_b], axis=1)
    wxq = w_xq.astype(BF16)
    wxo = w_xo.astype(BF16)
    wr = jnp.swapaxes(w_router, 1, 2)
    rb = router_bias[:, :, None]
    wsgu = jnp.concatenate([ws_gate, ws_up], axis=-1).astype(BF16)
    wsd = ws_down.astype(BF16)

    kmem, vmem = _mem_kv(mem, w_xk.astype(BF16), w_xv.astype(BF16))

    assert bn % 2 == 0
    hb = bn // 2
    t = hb * s_len
    tm = min(512, s_len)

    def mix_attn(l, c, xc, dep):
        x_b0 = c * hb if l == 0 else 0
        yag, sgb, q, k, v = _mixer_in(l, hb, x_b0, c * hb, xc, tabs, wx, wconv, wco, qn, wq, kvn, wkv,
                                      min(1024, s_len), dep)
        return yag, sgb, _attention(q, k, v, tq, yag)

    def post_route(l, c, xc, ma, dep):
        x_b0 = c * hb if l == 0 else 0
        yag, sgb, o = ma
        x2p, base, logits = _post(l, hb, x_b0, c * hb, xc, yag, sgb, o, kmem, vmem, wao, wmo, ln1,
                                  wxq, wxo, ln2, wr, wsgu, wsd, min(1024, s_len), alpha, dep)
        eid, rank, gate8, counts = _route_call(l, logits, rb)
        slot_chunks, sched, n_slots = _moe_schedule(eid, rank, counts)
        return dict(x2p=x2p.reshape(t, d // 2), base=base, gate8=gate8, slot_chunks=slot_chunks,
                    sched=sched, n_slots=n_slots)

    def dispatch(st, dep):
        st["xs"] = _sc_dispatch(st["x2p"], st["slot_chunks"], st["n_slots"], dep)

    def experts(l, st, dep):
        st["ys"] = _experts(l, st["xs"], st["sched"], w_gate, w_up, w_down, dep)
        st["yg"] = _sc_combine(st["ys"], st["slot_chunks"], t).reshape(TOP_K, hb, s_len, d // 2)

    def final(l, c, st, dep, out_prev):
        if l == depth - 1:
            return _final(l, st["yg"], st["gate8"], st["base"], ln3, tm, bn, c * hb, out_prev, dep)
        return _final(l, st["yg"], st["gate8"], st["base"], ln3, tm, hb, 0, None, dep)

    xa = xb = x
    ma_a = mix_attn(0, 0, xa, x)
    st_a = post_route(0, 0, xa, ma_a, ma_a[2])
    dispatch(st_a, st_a["gate8"])
    for l in range(depth):
        more = l + 1 < depth
        ma_b = mix_attn(l, 1, xb, st_a["gate8"])
        experts(l, st_a, ma_b[2])
        st_b = post_route(l, 1, xb, ma_b, st_a["ys"])
        xa = final(l, 0, st_a, st_b["gate8"], None)
        dispatch(st_b, xa if more else st_b["gate8"])
        if more:
            ma_a = mix_attn(l + 1, 0, xa, xa)
        experts(l, st_b, ma_a[2] if more else xa)
        if more:
            st_a = post_route(l + 1, 0, xa, ma_a, st_b["ys"])
        xb = final(l, 1, st_b, st_a["gate8"] if more else st_b["ys"], xa if not more else None)
        if more:
            dispatch(st_a, xb)
    return xb
```

```python
import functools

import jax
import jax.numpy as jnp
from jax import lax
from jax.experimental import pallas as pl
from jax.experimental.pallas import tpu as pltpu
from jax.experimental.pallas import tpu_sc as plsc

CONV_DIM = 512
CONV_WIDTH = 3
MLA_HEADS = 8
QK_NOPE_DIM = 64
QK_ROPE_DIM = 32
V_HEAD_DIM = 64
Q_RANK = 384
KV_RANK = 256
ROPE_BASE = 10000.0
XATTN_HEADS = 4
N_EXPERTS = 64
TOP_K = 8
N_GROUPS = 8
TOPK_GROUPS = 4
GROUP_SIZE = N_EXPERTS // N_GROUPS
ROUTE_SCALE = 2.5
LN_EPS = 1e-5
RMS_EPS = 1e-6

LANES = 128
HALO_ROWS = 8
VMEM_LIMIT = 56 * 1024 * 1024
SC_CORES = 2
SC_SUBCORES = 16
SC_CHUNK = 64
EXPERT_TILE = 1024

BF16 = jnp.bfloat16
F32 = jnp.float32
I32 = jnp.int32


_DEP_SPEC = pl.BlockSpec(memory_space=pl.ANY)


def _dot(a, b):
    return jnp.dot(a, b, preferred_element_type=F32)


def _dot_nt(a, b):
    return lax.dot_general(a, b, (((1,), (1,)), ((), ())), preferred_element_type=F32)


def _layer_norm(x, g, b):
    mu = jnp.mean(x, axis=-1, keepdims=True)
    xc = x - mu
    var = jnp.mean(xc * xc, axis=-1, keepdims=True)
    return xc * lax.rsqrt(var + LN_EPS) * g + b


def _rms_norm(x, g):
    ms = jnp.mean(x * x, axis=-1, keepdims=True)
    return x * lax.rsqrt(ms + RMS_EPS) * g


def _silu(x):
    return x * jax.nn.sigmoid(x)


_HI_MASK = -65536


def _pack_rows(y):
    w = y.shape[1] // 2
    lo = lax.bitcast_convert_type(y[:, :w].astype(BF16).astype(F32), I32)
    hi = lax.bitcast_convert_type(y[:, w:].astype(BF16).astype(F32), I32)
    return lax.shift_right_logical(lo, 16) | (hi & _HI_MASK)


def _unpack_rows(p):
    lo = lax.bitcast_convert_type(lax.shift_left(p, 16), F32)
    hi = lax.bitcast_convert_type(p & _HI_MASK, F32)
    return jnp.concatenate([lo, hi], axis=-1)


_C_BCH = 0
_C_CQ = _C_BCH + 3 * CONV_DIM
_C_CKV = _C_CQ + Q_RANK
_C_KR = _C_CKV + KV_RANK
_C_GA = _C_KR + LANES
_C_GB = _C_GA + 1024


def _mixer_in_kernel(x_ref, xh_ref, tab_ref, wx_ref, wconv_ref, wco_ref, qn_ref, wq_ref,
                     kvn_ref, wkv_ref, dep_ref,
                     yag_ref, sgb_ref, q_ref, k_ref, v_ref, *, d_model):
    i = pl.program_id(1)
    d = d_model
    ts = x_ref.shape[1]
    xb = x_ref[0].astype(BF16)

    bch = _dot(xb, wx_ref[:, _C_BCH:_C_BCH + 3 * CONV_DIM])
    b_gate = bch[:, :CONV_DIM]
    u = bch[:, CONV_DIM:2 * CONV_DIM] * bch[:, 2 * CONV_DIM:]
    xh = xh_ref[0].astype(BF16)
    hp = _dot(xh, wx_ref[:, _C_BCH + CONV_DIM:_C_BCH + 3 * CONV_DIM])
    uh = hp[:, :CONV_DIM] * hp[:, CONV_DIM:]
    uh = jnp.where(i == 0, 0.0, uh)
    row = lax.broadcasted_iota(jnp.int32, (ts, CONV_DIM), 0)
    u1 = jnp.where(row == 0, uh[HALO_ROWS - 1:HALO_ROWS], pltpu.roll(u, 1, 0))
    u2 = jnp.where(row == 0, uh[HALO_ROWS - 2:HALO_ROWS - 1],
                   jnp.where(row == 1, uh[HALO_ROWS - 1:HALO_ROWS], pltpu.roll(u, 2, 0)))
    wc = wconv_ref[0]
    z = wc[0:1] * u2 + wc[1:2] * u1 + wc[2:3] * u
    y_a = _dot((b_gate * z).astype(BF16), wco_ref[0])
    g_a = _dot(xb, wx_ref[:, _C_GA:_C_GA + d])
    yag_ref[0] = (jax.nn.sigmoid(g_a) * y_a).astype(BF16)
    g_b = _dot(xb, wx_ref[:, _C_GB:_C_GB + d])
    sgb_ref[0] = jax.nn.sigmoid(g_b).astype(BF16)

    tab = tab_ref[0]
    t_a, t_m, t_p = tab[:, 0:LANES], tab[:, LANES:2 * LANES], tab[:, 2 * LANES:3 * LANES]
    half = QK_ROPE_DIM // 2

    def rotate(c):
        return (c * t_a + pltpu.roll(c, LANES - half, 1) * t_m + pltpu.roll(c, half, 1) * t_p)

    c_q = _dot(xb, wx_ref[:, _C_CQ:_C_CQ + Q_RANK])
    cqn = _rms_norm(c_q, qn_ref[0]).astype(BF16)
    hw = MLA_HEADS * LANES
    q = _dot(cqn, wq_ref[0])
    for h in range(MLA_HEADS):
        sl = slice(h * LANES, (h + 1) * LANES)
        q_ref[0, :, sl] = rotate(q[:, sl]).astype(BF16)

    c_kv = _dot(xb, wx_ref[:, _C_CKV:_C_CKV + KV_RANK])
    ckvn = _rms_norm(c_kv, kvn_ref[0]).astype(BF16)
    kr_rot = rotate(_dot(xb, wx_ref[:, _C_KR:_C_KR + LANES]))
    k_nope = _dot(ckvn, wkv_ref[0, :, :hw])
    for h in range(MLA_HEADS):
        sl = slice(h * LANES, (h + 1) * LANES)
        k_ref[0, :, sl] = (k_nope[:, sl] + kr_rot).astype(BF16)
    v_ref[0] = _dot(ckvn, wkv_ref[0, :, hw:]).astype(BF16)


def _mixer_in(l, bn, x_b0, tab_b0, x, tabs, wx, wconv, wco, qn, wq, kvn, wkv, ts, dep):
    _, s_len, d = x.shape
    n_t = s_len // ts
    hw = MLA_HEADS * LANES
    vw = MLA_HEADS * V_HEAD_DIM
    tile = lambda w: pl.BlockSpec((1, ts, w), lambda b, i: (b, i, 0))
    once = pl.Buffered(1)
    lay = lambda a: pl.BlockSpec((1,) + a.shape[1:], lambda b, i: (l,) + (0,) * (a.ndim - 1),
                                 pipeline_mode=once)
    halo = pl.BlockSpec((1, HALO_ROWS, d),
                        lambda b, i: (b + x_b0, jnp.maximum(i * (ts // HALO_ROWS) - 1, 0), 0))
    return pl.pallas_call(
        functools.partial(_mixer_in_kernel, d_model=d),
        grid=(bn, n_t),
        in_specs=[pl.BlockSpec((1, ts, d), lambda b, i: (b + x_b0, i, 0)), halo,
                  pl.BlockSpec((1, ts, 3 * LANES), lambda b, i: (b + tab_b0, i, 0)),
                  pl.BlockSpec((None,) + wx.shape[1:], lambda b, i: (l, 0, 0), pipeline_mode=once),
                  lay(wconv), lay(wco), lay(qn), lay(wq), lay(kvn), lay(wkv), _DEP_SPEC],
        out_specs=[tile(d), tile(d), tile(hw), tile(hw), tile(vw)],
        out_shape=[jax.ShapeDtypeStruct((bn, s_len, d), BF16),
                   jax.ShapeDtypeStruct((bn, s_len, d), BF16),
                   jax.ShapeDtypeStruct((bn, s_len, hw), BF16),
                   jax.ShapeDtypeStruct((bn, s_len, hw), BF16),
                   jax.ShapeDtypeStruct((bn, s_len, vw), BF16)],
        compiler_params=pltpu.CompilerParams(
            dimension_semantics=("parallel", "arbitrary"), vmem_limit_bytes=VMEM_LIMIT),
        name="mixer_in",
    )(x, x, tabs, wx, wconv, wco, qn, wq, kvn, wkv, dep)


def _attn_kernel(q_ref, k_ref, v_ref, dep_ref, o_ref, *, tq):
    s_len = q_ref.shape[1]
    n_q = s_len // tq
    lane = lax.broadcasted_iota(jnp.int32, (tq, 2 * V_HEAD_DIM), 1)
    causal = (lax.broadcasted_iota(jnp.int32, (tq, tq), 1)
              <= lax.broadcasted_iota(jnp.int32, (tq, tq), 0))
    for qi in range(n_q):
        past = qi * tq
        rows = slice(past, past + tq)
        outs = []
        for h in range(2):
            sl = slice(h * LANES, (h + 1) * LANES)
            qh = q_ref[0, rows, sl]
            s_d = jnp.where(causal, _dot_nt(qh, k_ref[0, rows, sl]), -jnp.inf)
            m = jnp.max(s_d, axis=-1, keepdims=True)
            if past:
                s_p = _dot_nt(qh, k_ref[0, :past, sl])
                m = jnp.maximum(m, jnp.max(s_p, axis=-1, keepdims=True))
            p_d = jnp.exp(s_d - m)
            den = jnp.sum(p_d, axis=-1, keepdims=True)
            acc = _dot(p_d.astype(BF16), v_ref[0, rows, :])
            if past:
                p_p = jnp.exp(s_p - m)
                den = den + jnp.sum(p_p, axis=-1, keepdims=True)
                acc = acc + _dot(p_p.astype(BF16), v_ref[0, :past, :])
            outs.append(acc / den)
        o_ref[0, rows, :] = jnp.where(lane < V_HEAD_DIM, outs[0], outs[1]).astype(BF16)


def _attention(q, k, v, tq, dep):
    bn, s_len, _ = q.shape
    n_pairs = MLA_HEADS // 2
    return pl.pallas_call(
        functools.partial(_attn_kernel, tq=tq),
        grid=(bn, n_pairs),
        in_specs=[pl.BlockSpec((1, s_len, 2 * LANES), lambda b, h: (b, 0, h)),
                  pl.BlockSpec((1, s_len, 2 * LANES), lambda b, h: (b, 0, h)),
                  pl.BlockSpec((1, s_len, 2 * V_HEAD_DIM), lambda b, h: (b, 0, h)), _DEP_SPEC],
        out_specs=pl.BlockSpec((1, s_len, 2 * V_HEAD_DIM), lambda b, h: (b, 0, h)),
        out_shape=jax.ShapeDtypeStruct((bn, s_len, MLA_HEADS * V_HEAD_DIM), BF16),
        compiler_params=pltpu.CompilerParams(
            dimension_semantics=("parallel", "parallel"), vmem_limit_bytes=VMEM_LIMIT),
        name="mla_attention",
    )(q, k, v, dep)


def _mem_kv_kernel(mem_ref, wk_ref, wv_ref, k_ref, v_ref):
    mb = mem_ref[0].astype(BF16)
    k_ref[0, 0] = _dot(mb, wk_ref[0]).astype(BF16)
    v_ref[0, 0] = _dot(mb, wv_ref[0]).astype(BF16)


def _mem_kv(mem, w_xk, w_xv):
    bn, m_len, d = mem.shape
    n_l = w_xk.shape[0]
    wspec = pl.BlockSpec((1, d, d), lambda l, b: (l, 0, 0))
    ospec = pl.BlockSpec((1, 1, m_len, d), lambda l, b: (l, b, 0, 0))
    return pl.pallas_call(
        _mem_kv_kernel,
        grid=(n_l, bn),
        in_specs=[pl.BlockSpec((1, m_len, d), lambda l, b: (b, 0, 0)), wspec, wspec],
        out_specs=[ospec, ospec],
        out_shape=[jax.ShapeDtypeStruct((n_l, bn, m_len, d), BF16)] * 2,
        compiler_params=pltpu.CompilerParams(
            dimension_semantics=("arbitrary", "arbitrary"), vmem_limit_bytes=VMEM_LIMIT),
        name="mem_kv",
    )(mem, w_xk, w_xv)


def _first_argmax(vals, idx):
    m = jnp.max(functools.reduce(jnp.maximum, vals), axis=0, keepdims=True)
    big = jnp.int32(1 << 20)
    cand = functools.reduce(jnp.minimum, [jnp.where(v == m, ix, big) for v, ix in zip(vals, idx)])
    first = jnp.min(cand, axis=0, keepdims=True)
    return [ix == first for ix in idx], m, first


def _route(logits_t, bias):
    n = logits_t.shape[1]
    scores = jax.nn.sigmoid(logits_t)
    sel = scores + bias
    sub = lax.broadcasted_iota(jnp.int32, (GROUP_SIZE, n), 0)
    neg = -jnp.inf
    sel_g = [sel[g * GROUP_SIZE:(g + 1) * GROUP_SIZE] for g in range(N_GROUPS)]
    gs = []
    for g in range(N_GROUPS):
        (hot,), m1, _ = _first_argmax([sel_g[g]], [sub])
        m2 = jnp.max(jnp.where(hot, neg, sel_g[g]), axis=0, keepdims=True)
        gs.append(m1 + m2)
    gsv = jnp.concatenate(gs, axis=0)
    gmask = jnp.zeros(gsv.shape, F32)
    for _ in range(TOPK_GROUPS):
        (hot,), _m, _ = _first_argmax([gsv], [sub])
        gmask = jnp.where(hot, 1.0, gmask)
        gsv = jnp.where(hot, neg, gsv)
    msel = [jnp.where(gmask[g:g + 1] > 0.0, sel_g[g], neg) for g in range(N_GROUPS)]
    eidx = [sub + g * GROUP_SIZE for g in range(N_GROUPS)]
    picks, ids = [], []
    for _ in range(TOP_K):
        hots, _m, first = _first_argmax(msel, eidx)
        picks.append(hots)
        ids.append(first)
        msel = [jnp.where(h, neg, v) for h, v in zip(hots, msel)]
    chosen = [functools.reduce(jnp.logical_or, [p[g] for p in picks]) for g in range(N_GROUPS)]
    sc_g = [scores[g * GROUP_SIZE:(g + 1) * GROUP_SIZE] for g in range(N_GROUPS)]
    picked = [jnp.where(c, s, 0.0) for c, s in zip(chosen, sc_g)]
    wsum = jnp.sum(functools.reduce(lambda a, b: a + b, picked), axis=0, keepdims=True)
    gates = [p / wsum * ROUTE_SCALE for p in picked]
    return picks, ids, chosen, gates


def _pick_rows(hots, vals):
    acc = functools.reduce(lambda a, b: a + b, [jnp.where(h, v, 0.0) for h, v in zip(hots, vals)])
    return jnp.sum(acc, axis=0, keepdims=True)


def _post_kernel(x_ref, yag_ref, sgb_ref, o_ref, km_ref, vm_ref, wao_ref, wmo_ref, ln1_ref,
                 wxq_ref, wxo_ref, ln2_ref, wr_ref, dep_ref,
                 x2p_ref, x2_ref, logit_ref, *, alpha):
    d = x_ref.shape[2]
    x = x_ref[0]
    y_b = _dot(o_ref[0], wao_ref[0])
    y = yag_ref[0].astype(F32) + sgb_ref[0].astype(F32) * y_b
    mix = _dot(y.astype(BF16), wmo_ref[0])
    x1 = _layer_norm(alpha * x + mix, ln1_ref[0, 0:1], ln1_ref[0, 1:2])

    hd = d // XATTN_HEADS
    xq = (_dot(x1.astype(BF16), wxq_ref[0]) * (hd ** -0.5)).astype(BF16)
    heads = []
    for h in range(XATTN_HEADS):
        sl = slice(h * hd, (h + 1) * hd)
        s = _dot_nt(xq[:, sl], km_ref[0, 0, :, sl])
        m = jnp.max(s, axis=-1, keepdims=True)
        p = jnp.exp(s - m)
        den = jnp.sum(p, axis=-1, keepdims=True)
        heads.append((_dot(p.astype(BF16), vm_ref[0, 0, :, sl]) / den).astype(BF16))
    xat = _dot(jnp.concatenate(heads, axis=-1), wxo_ref[0])
    x2 = _layer_norm(alpha * x1 + xat, ln2_ref[0, 0:1], ln2_ref[0, 1:2])

    x2b = x2.astype(BF16)
    x2l = (x2 - x2b.astype(F32)).astype(BF16)
    wr = wr_ref[0]
    wrh = wr.astype(BF16)
    wrl = (wr - wrh.astype(F32)).astype(BF16)
    logit_ref[0] = _dot_nt(wrh, x2b) + (_dot_nt(wrh, x2l) + _dot_nt(wrl, x2b))

    x2p_ref[0] = _pack_rows(x2)
    x2_ref[0] = x2


def _post(l, bn, x_b0, mem_b0, x, yag, sgb, o, kmem, vmem, wao, wmo, ln1, wxq, wxo, ln2, wr,
          ts, alpha, dep):
    _, s_len, d = x.shape
    m_len = kmem.shape[2]
    tile = lambda w: pl.BlockSpec((1, ts, w), lambda b, i: (b, i, 0))
    lay = lambda a: pl.BlockSpec((1,) + a.shape[1:], lambda b, i: (l,) + (0,) * (a.ndim - 1),
                                 pipeline_mode=pl.Buffered(1))
    memspec = pl.BlockSpec((1, 1, m_len, d), lambda b, i: (l, b + mem_b0, 0, 0))
    return pl.pallas_call(
        functools.partial(_post_kernel, alpha=alpha),
        grid=(bn, s_len // ts),
        in_specs=[pl.BlockSpec((1, ts, d), lambda b, i: (b + x_b0, i, 0)),
                  tile(d), tile(d), tile(o.shape[2]), memspec, memspec,
                  lay(wao), lay(wmo), lay(ln1), lay(wxq), lay(wxo), lay(ln2), lay(wr), _DEP_SPEC],
        out_specs=[tile(d // 2), tile(d),
                   pl.BlockSpec((1, N_EXPERTS, ts), lambda b, i: (b, 0, i))],
        out_shape=[jax.ShapeDtypeStruct((bn, s_len, d // 2), I32),
                   jax.ShapeDtypeStruct((bn, s_len, d), F32),
                   jax.ShapeDtypeStruct((bn, N_EXPERTS, s_len), F32)],
        compiler_params=pltpu.CompilerParams(
            dimension_semantics=("parallel", "arbitrary"), vmem_limit_bytes=VMEM_LIMIT),
        name="post_mixer",
    )(x, yag, sgb, o, kmem, vmem, wao, wmo, ln1, wxq, wxo, ln2, wr, dep)


RANK_BLOCK = 512


def _route_kernel(lg_ref, rb_ref, eid_ref, rank_ref, gate_ref, cnt_ref):
    @pl.when(pl.program_id(0) == 0)
    def _():
        cnt_ref[...] = jnp.zeros_like(cnt_ref)

    n = lg_ref.shape[2]
    blk = min(RANK_BLOCK, n)
    picks, ids, chosen, gates = _route(lg_ref[0], rb_ref[0])

    sel_t = jnp.concatenate([c.astype(F32) for c in chosen], axis=0)
    before = (lax.broadcasted_iota(jnp.int32, (blk, blk), 0)
              < lax.broadcasted_iota(jnp.int32, (blk, blk), 1)).astype(BF16)
    running = cnt_ref[:, 0:1]
    ranks = []
    for j in range(n // blk):
        sb = sel_t[:, j * blk:(j + 1) * blk]
        ranks.append(_dot(sb.astype(BF16), before) + running)
        running = running + jnp.sum(sb, axis=1, keepdims=True)
    cnt_ref[...] = jnp.broadcast_to(running, cnt_ref.shape)
    rank_all = jnp.concatenate(ranks, axis=1)
    rank_g = [rank_all[g * GROUP_SIZE:(g + 1) * GROUP_SIZE] for g in range(N_GROUPS)]
    eid_ref[0] = jnp.concatenate(ids, axis=0)
    rank_ref[0] = jnp.concatenate([_pick_rows(h, rank_g) for h in picks], axis=0).astype(I32)
    gate_ref[0] = jnp.concatenate([_pick_rows(h, gates) for h in picks], axis=0)


def _route_call(l, logits, rb):
    bn, n_e, s_len = logits.shape
    pick = pl.BlockSpec((1, TOP_K, s_len), lambda b: (b, 0, 0))
    return pl.pallas_call(
        _route_kernel,
        grid=(bn,),
        in_specs=[pl.BlockSpec((1, n_e, s_len), lambda b: (b, 0, 0)),
                  pl.BlockSpec((1,) + rb.shape[1:], lambda b: (l, 0, 0))],
        out_specs=[pick, pick, pick, pl.BlockSpec((n_e, LANES), lambda b: (0, 0))],
        out_shape=[jax.ShapeDtypeStruct((bn, TOP_K, s_len), I32),
                   jax.ShapeDtypeStruct((bn, TOP_K, s_len), I32),
                   jax.ShapeDtypeStruct((bn, TOP_K, s_len), F32),
                   jax.ShapeDtypeStruct((n_e, LANES), F32)],
        compiler_params=pltpu.CompilerParams(
            dimension_semantics=("arbitrary",), vmem_limit_bytes=VMEM_LIMIT),
        name="moe_route",
    )(logits, rb)


def _slot_kernel(offs_ref, eid_ref, rank_ref, slot_ref):
    eid = eid_ref[0]
    acc = rank_ref[0]
    for e in range(N_EXPERTS):
        acc = acc + jnp.where(eid == e, offs_ref[e], 0)
    slot_ref[0] = acc


def _slots(offs, eid, rank):
    bn, k, s_len = eid.shape
    spec = pl.BlockSpec((1, k, s_len), lambda b, offs_ref: (b, 0, 0))
    return pl.pallas_call(
        _slot_kernel,
        grid_spec=pltpu.PrefetchScalarGridSpec(
            num_scalar_prefetch=1, grid=(bn,), in_specs=[spec, spec], out_specs=spec),
        out_shape=jax.ShapeDtypeStruct((bn, k, s_len), I32),
        name="moe_slots",
    )(offs, eid, rank)


def _sc_mesh():
    return plsc.VectorSubcoreMesh(core_axis_name="c", subcore_axis_name="s")


def _sc_dispatch(xp, slot_chunks, n_slots, dep):
    t, w = xp.shape
    n_chunks, k, ch = slot_chunks.shape
    per_worker = n_chunks // (SC_CORES * SC_SUBCORES)

    @functools.partial(
        pl.kernel, mesh=_sc_mesh(),
        out_type=jax.ShapeDtypeStruct((n_slots, w), I32),
        scratch_types=[pltpu.VMEM((k, ch), I32), pltpu.VMEM((ch, w), I32), pltpu.SemaphoreType.DMA],
        name="moe_dispatch")
    def body(x_hbm, slot_hbm, dep_hbm, out_hbm, idx_v, rows_v, sem):
        worker = lax.axis_index("s") * SC_CORES + lax.axis_index("c")

        @pl.loop(0, per_worker)
        def _(j):
            c = worker * per_worker + j
            pltpu.sync_copy(slot_hbm.at[c], idx_v)
            pltpu.sync_copy(x_hbm.at[pl.ds(c * ch, ch)], rows_v)
            copies = [pltpu.async_copy(rows_v, out_hbm.at[idx_v.at[kk]], sem) for kk in range(k)]
            for cp in copies:
                cp.wait()

    return body(xp, slot_chunks, dep)


def _sc_combine(ys, slot_chunks, t):
    w = ys.shape[1]
    n_chunks, k, ch = slot_chunks.shape
    per_worker = n_chunks // (SC_CORES * SC_SUBCORES)

    @functools.partial(
        pl.kernel, mesh=_sc_mesh(),
        out_type=jax.ShapeDtypeStruct((k, t, w), I32),
        scratch_types=[pltpu.VMEM((k, ch), I32), pltpu.VMEM((ch, w), I32), pltpu.VMEM((ch, w), I32),
                       pltpu.SemaphoreType.DMA, pltpu.SemaphoreType.DMA,
                       pltpu.SemaphoreType.DMA, pltpu.SemaphoreType.DMA],
        name="moe_combine")
    def body(y_hbm, slot_hbm, out_hbm, idx_v, rows0, rows1, g0, g1, w0, w1):
        worker = lax.axis_index("s") * SC_CORES + lax.axis_index("c")
        bufs, gsem, wsem = (rows0, rows1), (g0, g1), (w0, w1)

        @pl.loop(0, per_worker)
        def _(j):
            c = worker * per_worker + j
            pltpu.sync_copy(slot_hbm.at[c], idx_v)
            gathers = [None] * k
            writes = [None] * k
            gathers[0] = pltpu.async_copy(y_hbm.at[idx_v.at[0]], bufs[0], gsem[0])
            for kk in range(k):
                cur = kk % 2
                if kk + 1 < k:
                    if kk >= 1:
                        writes[kk - 1].wait()
                    gathers[kk + 1] = pltpu.async_copy(
                        y_hbm.at[idx_v.at[kk + 1]], bufs[1 - cur], gsem[1 - cur])
                gathers[kk].wait()
                writes[kk] = pltpu.async_copy(
                    bufs[cur], out_hbm.at[kk, pl.ds(c * ch, ch)], wsem[cur])
            writes[k - 2].wait()
            writes[k - 1].wait()

    return body(ys, slot_chunks)


def _experts_kernel(ti_ref, te_ref, nv_ref, eo_ref, ne_ref, nu_ref, xs_ref, wg_hbm, wu_hbm, wd_hbm,
                    dep_ref, ys_ref, wg_f, wu_f, wd_f, wg_b, wu_b, wd_b, sem, *, layer):
    j = pl.program_id(0)
    expert = te_ref[j]
    slot = eo_ref[j]
    prev = te_ref[jnp.maximum(j - 1, 0)]
    new_expert = jnp.logical_or(j == 0, expert != prev)

    def weight_copies(e, s):
        return [pltpu.make_async_copy(hbm.at[layer, e], buf.at[s], sem.at[s, i])
                for i, (hbm, buf) in enumerate(((wg_hbm, wg_f), (wu_hbm, wu_f), (wd_hbm, wd_f)))]

    @pl.when(j == 0)
    def _():
        for cp in weight_copies(expert, slot):
            cp.start()

    @pl.when(new_expert)
    def _():
        nxt = ne_ref[j]

        @pl.when(nxt >= 0)
        def _():
            for cp in weight_copies(nxt, 1 - slot):
                cp.start()

        for cp in weight_copies(expert, slot):
            cp.wait()
        wg_b[...] = wg_f[slot].astype(BF16)
        wu_b[...] = wu_f[slot].astype(BF16)
        wd_b[...] = wd_f[slot].astype(BF16)

    def ffn(n_rows):
        rows = pl.ds(0, n_rows)
        xb = _unpack_rows(xs_ref[rows, :]).astype(BF16)
        hid = _silu(_dot(xb, wg_b[...])) * _dot(xb, wu_b[...])
        ys_ref[rows, :] = _pack_rows(_dot(hid.astype(BF16), wd_b[...]))

    used = j < nu_ref[0]
    half = xs_ref.shape[0] // 2
    many = nv_ref[j] > half

    @pl.when(jnp.logical_and(used, many))
    def _():
        ffn(2 * half)

    @pl.when(jnp.logical_and(used, jnp.logical_not(many)))
    def _():
        ffn(half)


def _experts(l, xs, sched, w_gate, w_up, w_down, dep):
    n_slots, w = xs.shape
    n_tiles = n_slots // EXPERT_TILE
    d, f = w_gate.shape[2:]
    rows = pl.BlockSpec((EXPERT_TILE, w), lambda j, ti, *_: (ti[j], 0))
    hbm = pl.BlockSpec(memory_space=pl.ANY)
    return pl.pallas_call(
        functools.partial(_experts_kernel, layer=l),
        grid_spec=pltpu.PrefetchScalarGridSpec(
            num_scalar_prefetch=len(sched), grid=(sched[-1][0],),
            in_specs=[rows, hbm, hbm, hbm, _DEP_SPEC],
            out_specs=rows,
            scratch_shapes=[pltpu.VMEM((2, d, f), F32), pltpu.VMEM((2, d, f), F32),
                            pltpu.VMEM((2, f, d), F32),
                            pltpu.VMEM((d, f), BF16), pltpu.VMEM((d, f), BF16),
                            pltpu.VMEM((f, d), BF16), pltpu.SemaphoreType.DMA((2, 3))]),
        out_shape=jax.ShapeDtypeStruct((n_slots, w), I32),
        compiler_params=pltpu.CompilerParams(
            dimension_semantics=("arbitrary",), vmem_limit_bytes=VMEM_LIMIT),
        name="moe_experts",
    )(*sched, xs, w_gate, w_up, w_down, dep)


def _final_kernel(yg_ref, gate_ref, x2_ref, ln_ref, wsgu_ref, wsd_ref, *rest, alpha):
    out_ref = rest[-1]
    x2 = x2_ref[0]
    sh = _dot(x2.astype(BF16), wsgu_ref[0])
    f = sh.shape[1] // 2
    hid = _silu(sh[:, :f]) * sh[:, f:]
    acc = alpha * x2 + _dot(hid.astype(BF16), wsd_ref[0])
    gates = gate_ref[0].T
    for k in range(yg_ref.shape[0]):
        acc = acc + gates[:, k:k + 1] * _unpack_rows(yg_ref[k, 0])
    out_ref[0] = _layer_norm(acc, ln_ref[0, 0:1], ln_ref[0, 1:2])


def _final(l, yg, gate8, x2, ln3, wsgu, wsd, tm, alpha, out_rows, out_b0, out_prev, dep):
    bn, s_len, d = x2.shape
    k = yg.shape[0]
    lay = lambda a: pl.BlockSpec((1,) + a.shape[1:], lambda b, i: (l, 0, 0),
                                 pipeline_mode=pl.Buffered(1))
    in_specs = [pl.BlockSpec((k, 1, tm, d // 2), lambda b, i: (0, b, i, 0)),
                pl.BlockSpec((1, k, tm), lambda b, i: (b, 0, i)),
                pl.BlockSpec((1, tm, d), lambda b, i: (b, i, 0)),
                lay(ln3), lay(wsgu), lay(wsd), _DEP_SPEC]
    args = [yg, gate8, x2, ln3, wsgu, wsd, dep]
    aliases = {}
    if out_prev is not None:
        in_specs.append(pl.BlockSpec(memory_space=pl.ANY))
        args.append(out_prev)
        aliases = {len(args) - 1: 0}
    return pl.pallas_call(
        functools.partial(_final_kernel, alpha=alpha),
        grid=(bn, s_len // tm),
        in_specs=in_specs,
        out_specs=pl.BlockSpec((1, tm, d), lambda b, i: (b + out_b0, i, 0)),
        out_shape=jax.ShapeDtypeStruct((out_rows, s_len, d), F32),
        input_output_aliases=aliases,
        compiler_params=pltpu.CompilerParams(
            dimension_semantics=("parallel", "parallel"), vmem_limit_bytes=VMEM_LIMIT),
        name="moe_final",
    )(*args)


def _moe_schedule(eid, rank, counts):
    bn, _, s_len = eid.shape
    t = bn * s_len
    n_tiles = (t * TOP_K) // EXPERT_TILE + N_EXPERTS

    cnt = counts[:, 0].astype(I32)
    padded = (cnt + EXPERT_TILE - 1) // EXPERT_TILE * EXPERT_TILE
    ends = jnp.cumsum(padded)
    offs = ends - padded
    n_used = ends[-1] // EXPERT_TILE
    ti = jnp.minimum(jnp.arange(n_tiles, dtype=I32), n_used - 1)
    te = jnp.sum((ends[None, :] <= (ti * EXPERT_TILE)[:, None]).astype(I32), axis=1)
    nv = jnp.take(cnt + offs, te) - ti * EXPERT_TILE
    nonempty = cnt > 0
    idx = jnp.arange(N_EXPERTS, dtype=I32)
    order = jnp.cumsum(nonempty.astype(I32)) - 1
    later = jnp.where(nonempty[None, :] & (idx[None, :] > idx[:, None]), idx[None, :], N_EXPERTS)
    succ = jnp.min(later, axis=1)
    succ = jnp.where(succ == N_EXPERTS, -1, succ).astype(I32)
    eo = jnp.take(order, te) % 2
    ne = jnp.take(succ, te)

    slots = _slots(offs, eid, rank)
    slot_chunks = slots.reshape(bn, TOP_K, s_len // SC_CHUNK, SC_CHUNK).transpose(0, 2, 1, 3)
    slot_chunks = slot_chunks.reshape(t // SC_CHUNK, TOP_K, SC_CHUNK)
    return slot_chunks, (ti, te, nv, eo, ne, n_used.reshape(1)), n_tiles * EXPERT_TILE


def _head_chunks(w, widths, n_heads, per_head):
    lead = w.shape[:-1]
    wh = w.reshape(lead + (n_heads, per_head))
    parts = [sign * wh[..., a:b] for a, b, sign in widths]
    used = sum(b - a for a, b, _ in widths)
    parts.append(jnp.zeros(lead + (n_heads, LANES - used), w.dtype))
    return jnp.concatenate(parts, axis=-1).reshape(lead + (n_heads * LANES,))


def kernel(x, mem, positions, w_in, w_conv, w_conv_out, q_norm, w_uq, kv_norm, w_uk, w_uv,
           w_attn_out, w_mix_out, ln1_g, ln1_b, w_xq, w_xk, w_xv, w_xo, ln2_g, ln2_b,
           w_router, router_bias, w_gate, w_up, w_down, ws_gate, ws_up, ws_down, ln3_g, ln3_b):
    bn, s_len, d = x.shape
    depth = w_in.shape[0]
    alpha = (2 * depth) ** 0.25
    ts = min(512, s_len)
    tq = min(512, s_len)
    half = QK_ROPE_DIM // 2
    nope, rope = QK_NOPE_DIM, QK_ROPE_DIM

    inv_freq = ROPE_BASE ** (-jnp.arange(half, dtype=F32) / half)
    ang = positions.astype(F32)[..., None] * inv_freq
    cos, sin = jnp.cos(ang), jnp.sin(ang)
    scale = (nope + rope) ** -0.5
    zeros = lambda w: jnp.zeros((bn, s_len, w), F32)
    tail = LANES - nope - rope
    tabs = jnp.concatenate([
        jnp.ones((bn, s_len, nope), F32), cos, cos, zeros(tail),
        zeros(nope), -sin, zeros(half), zeros(tail),
        zeros(nope), zeros(half), sin, zeros(tail)], axis=-1)

    sizes = (CONV_DIM, CONV_DIM, CONV_DIM, Q_RANK, KV_RANK, rope, d, d)
    offs = [0]
    for sz in sizes:
        offs.append(offs[-1] + sz)
    seg = lambda j: w_in[:, :, offs[j]:offs[j + 1]]
    zpad = lambda w: jnp.zeros(w_in.shape[:2] + (w,), w_in.dtype)
    wx = jnp.concatenate([w_in[:, :, :offs[5]], zpad(nope), seg(5), zpad(tail), w_in[:, :, offs[6]:]],
                         axis=-1).astype(BF16)

    per_q = nope + rope
    wq = _head_chunks(w_uq, [(0, per_q, scale)], MLA_HEADS, per_q).astype(BF16)
    wk = _head_chunks(w_uk, [(0, nope, 1.0)], MLA_HEADS, nope)
    wkv = jnp.concatenate([wk, w_uv], axis=-1).astype(BF16)

    wconv = w_conv[:, :, 0, :]
    wco = w_conv_out.astype(BF16)
    qn = q_norm[:, None, :]
    kvn = kv_norm[:, None, :]
    wao = w_attn_out.astype(BF16)
    wmo = w_mix_out.astype(BF16)
    ln1 = jnp.stack([ln1_g, ln1_b], axis=1)
    ln2 = jnp.stack([ln2_g, ln2_b], axis=1)
    ln3 = jnp.stack([ln3_g, ln3_b], axis=1)
    wxq = w_xq.astype(BF16)
    wxo = w_xo.astype(BF16)
    wr = jnp.swapaxes(w_router, 1, 2)
    rb = router_bias[:, :, None]
    wsgu = jnp.concatenate([ws_gate, ws_up], axis=-1).astype(BF16)
    wsd = ws_down.astype(BF16)

    kmem, vmem = _mem_kv(mem, w_xk.astype(BF16), w_xv.astype(BF16))

    assert bn % 2 == 0
    hb = bn // 2
    t = hb * s_len
    tm = min(512, s_len)

    def mix_attn(l, c, xc, dep):
        x_b0 = c * hb if l == 0 else 0
        yag, sgb, q, k, v = _mixer_in(l, hb, x_b0, c * hb, xc, tabs, wx, wconv, wco, qn, wq, kvn, wkv,
                                      min(1024, s_len), dep)
        return yag, sgb, _attention(q, k, v, tq, yag)

    def post_route(l, c, xc, ma, dep):
        x_b0 = c * hb if l == 0 else 0
        yag, sgb, o = ma
        x2p, x2, logits = _post(l, hb, x_b0, c * hb, xc, yag, sgb, o, kmem, vmem, wao, wmo, ln1,
                                wxq, wxo, ln2, wr, min(1024, s_len), alpha, dep)
        eid, rank, gate8, counts = _route_call(l, logits, rb)
        slot_chunks, sched, n_slots = _moe_schedule(eid, rank, counts)
        return dict(x2p=x2p.reshape(t, d // 2), x2=x2, gate8=gate8, slot_chunks=slot_chunks,
                    sched=sched, n_slots=n_slots)

    def dispatch(st, dep):
        st["xs"] = _sc_dispatch(st["x2p"], st["slot_chunks"], st["n_slots"], dep)

    def experts(l, st, dep):
        st["ys"] = _experts(l, st["xs"], st["sched"], w_gate, w_up, w_down, dep)
        st["yg"] = _sc_combine(st["ys"], st["slot_chunks"], t).reshape(TOP_K, hb, s_len, d // 2)

    def final(l, c, st, dep, out_prev):
        if l == depth - 1:
            return _final(l, st["yg"], st["gate8"], st["x2"], ln3, wsgu, wsd, tm, alpha, bn, c * hb,
                          out_prev, dep)
        return _final(l, st["yg"], st["gate8"], st["x2"], ln3, wsgu, wsd, tm, alpha, hb, 0, None, dep)

    xa = xb = x
    ma_a = mix_attn(0, 0, xa, x)
    st_a = post_route(0, 0, xa, ma_a, ma_a[2])
    dispatch(st_a, st_a["gate8"])
    for l in range(depth):
        more = l + 1 < depth
        ma_b = mix_attn(l, 1, xb, st_a["gate8"])
        experts(l, st_a, ma_b[2])
        st_b = post_route(l, 1, xb, ma_b, st_a["ys"])
        xa = final(l, 0, st_a, st_b["gate8"], None)
        dispatch(st_b, xa if more else st_b["gate8"])
        if more:
            ma_a = mix_attn(l + 1, 0, xa, xa)
        experts(l, st_b, ma_a[2] if more else xa)
        if more:
            st_a = post_route(l + 1, 0, xa, ma_a, st_b["ys"])
        xb = final(l, 1, st_b, st_a["gate8"] if more else st_b["ys"], xa if not more else None)
        if more:
            dispatch(st_a, xb)
    return xb
```

```python
import functools

import jax
import jax.numpy as jnp
from jax import lax
from jax.experimental import pallas as pl
from jax.experimental.pallas import tpu as pltpu
from jax.experimental.pallas import tpu_sc as plsc

CONV_DIM = 512
CONV_WIDTH = 3
MLA_HEADS = 8
QK_NOPE_DIM = 64
QK_ROPE_DIM = 32
V_HEAD_DIM = 64
Q_RANK = 384
KV_RANK = 256
ROPE_BASE = 10000.0
XATTN_HEADS = 4
N_EXPERTS = 64
TOP_K = 8
N_GROUPS = 8
TOPK_GROUPS = 4
GROUP_SIZE = N_EXPERTS // N_GROUPS
ROUTE_SCALE = 2.5
LN_EPS = 1e-5
RMS_EPS = 1e-6

LANES = 128
HALO_ROWS = 8
VMEM_LIMIT = 56 * 1024 * 1024
SC_CORES = 2
SC_SUBCORES = 16
SC_CHUNK = 64
EXPERT_TILE = 1024

BF16 = jnp.bfloat16
F32 = jnp.float32
I32 = jnp.int32


_DEP_SPEC = pl.BlockSpec(memory_space=pl.ANY)


def _dot(a, b):
    return jnp.dot(a, b, preferred_element_type=F32)


def _dot_nt(a, b):
    return lax.dot_general(a, b, (((1,), (1,)), ((), ())), preferred_element_type=F32)


def _layer_norm(x, g, b):
    mu = jnp.mean(x, axis=-1, keepdims=True)
    xc = x - mu
    var = jnp.mean(xc * xc, axis=-1, keepdims=True)
    return xc * lax.rsqrt(var + LN_EPS) * g + b


def _rms_norm(x, g):
    ms = jnp.mean(x * x, axis=-1, keepdims=True)
    return x * lax.rsqrt(ms + RMS_EPS) * g


def _silu(x):
    return x * jax.nn.sigmoid(x)


_HI_MASK = -65536


def _pack_rows(y):
    w = y.shape[1] // 2
    lo = lax.bitcast_convert_type(y[:, :w].astype(BF16).astype(F32), I32)
    hi = lax.bitcast_convert_type(y[:, w:].astype(BF16).astype(F32), I32)
    return lax.shift_right_logical(lo, 16) | (hi & _HI_MASK)


def _unpack_rows(p):
    lo = lax.bitcast_convert_type(lax.shift_left(p, 16), F32)
    hi = lax.bitcast_convert_type(p & _HI_MASK, F32)
    return jnp.concatenate([lo, hi], axis=-1)


_C_BCH = 0
_C_CQ = _C_BCH + 3 * CONV_DIM
_C_CKV = _C_CQ + Q_RANK


def _mixer_in_kernel(x_ref, xh_ref, tab_ref, wx_ref, wkr_ref, wg_ref, wconv_ref, wco_ref, qn_ref, wq_ref,
                     kvn_ref, wkv_ref, dep_ref,
                     yag_ref, sgb_ref, q_ref, k_ref, v_ref, *, d_model):
    i = pl.program_id(1)
    d = d_model
    ts = x_ref.shape[1]
    xb = x_ref[0].astype(BF16)

    bch = _dot(xb, wx_ref[:, _C_BCH:_C_BCH + 3 * CONV_DIM])
    b_gate = bch[:, :CONV_DIM]
    u = bch[:, CONV_DIM:2 * CONV_DIM] * bch[:, 2 * CONV_DIM:]
    xh = xh_ref[0].astype(BF16)
    hp = _dot(xh, wx_ref[:, _C_BCH + CONV_DIM:_C_BCH + 3 * CONV_DIM])
    uh = hp[:, :CONV_DIM] * hp[:, CONV_DIM:]
    uh = jnp.where(i == 0, 0.0, uh)
    row = lax.broadcasted_iota(jnp.int32, (ts, CONV_DIM), 0)
    u1 = jnp.where(row == 0, uh[HALO_ROWS - 1:HALO_ROWS], pltpu.roll(u, 1, 0))
    u2 = jnp.where(row == 0, uh[HALO_ROWS - 2:HALO_ROWS - 1],
                   jnp.where(row == 1, uh[HALO_ROWS - 1:HALO_ROWS], pltpu.roll(u, 2, 0)))
    wc = wconv_ref[0]
    z = wc[0:1] * u2 + wc[1:2] * u1 + wc[2:3] * u
    y_a = _dot((b_gate * z).astype(BF16), wco_ref[0])
    g_a = _dot(xb, wg_ref[:, :d])
    yag_ref[0] = (jax.nn.sigmoid(g_a) * y_a).astype(BF16)
    g_b = _dot(xb, wg_ref[:, d:])
    sgb_ref[0] = jax.nn.sigmoid(g_b).astype(BF16)

    tab = tab_ref[0]
    t_a, t_m, t_p = tab[:, 0:LANES], tab[:, LANES:2 * LANES], tab[:, 2 * LANES:3 * LANES]
    half = QK_ROPE_DIM // 2

    def rotate(c):
        return (c * t_a + pltpu.roll(c, LANES - half, 1) * t_m + pltpu.roll(c, half, 1) * t_p)

    c_q = _dot(xb, wx_ref[:, _C_CQ:_C_CQ + Q_RANK])
    cqn = _rms_norm(c_q, qn_ref[0]).astype(BF16)
    hw = MLA_HEADS * LANES
    q = _dot(cqn, wq_ref[0])
    for h in range(MLA_HEADS):
        sl = slice(h * LANES, (h + 1) * LANES)
        q_ref[0, :, sl] = rotate(q[:, sl]).astype(BF16)

    c_kv = _dot(xb, wx_ref[:, _C_CKV:_C_CKV + KV_RANK])
    ckvn = _rms_norm(c_kv, kvn_ref[0]).astype(BF16)
    kr_rot = rotate(_dot(xb, wkr_ref[...]))
    k_nope = _dot(ckvn, wkv_ref[0, :, :hw])
    for h in range(MLA_HEADS):
        sl = slice(h * LANES, (h + 1) * LANES)
        k_ref[0, :, sl] = (k_nope[:, sl] + kr_rot).astype(BF16)
    v_ref[0] = _dot(ckvn, wkv_ref[0, :, hw:]).astype(BF16)


def _mixer_in(l, bn, x_b0, tab_b0, x, tabs, wx, wkr, wg, wconv, wco, qn, wq, kvn, wkv, ts, dep):
    _, s_len, d = x.shape
    n_t = s_len // ts
    hw = MLA_HEADS * LANES
    vw = MLA_HEADS * V_HEAD_DIM
    tile = lambda w: pl.BlockSpec((1, ts, w), lambda b, i: (b, i, 0))
    once = pl.Buffered(1)
    lay = lambda a: pl.BlockSpec((1,) + a.shape[1:], lambda b, i: (l,) + (0,) * (a.ndim - 1),
                                 pipeline_mode=once)
    flat = lambda a: pl.BlockSpec((None,) + a.shape[1:], lambda b, i: (l, 0, 0), pipeline_mode=once)
    halo = pl.BlockSpec((1, HALO_ROWS, d),
                        lambda b, i: (b + x_b0, jnp.maximum(i * (ts // HALO_ROWS) - 1, 0), 0))
    return pl.pallas_call(
        functools.partial(_mixer_in_kernel, d_model=d),
        grid=(bn, n_t),
        in_specs=[pl.BlockSpec((1, ts, d), lambda b, i: (b + x_b0, i, 0)), halo,
                  pl.BlockSpec((1, ts, 3 * LANES), lambda b, i: (b + tab_b0, i, 0)),
                  flat(wx), flat(wkr), flat(wg),
                  lay(wconv), lay(wco), lay(qn), lay(wq), lay(kvn), lay(wkv), _DEP_SPEC],
        out_specs=[tile(d), tile(d), tile(hw), tile(hw), tile(vw)],
        out_shape=[jax.ShapeDtypeStruct((bn, s_len, d), BF16),
                   jax.ShapeDtypeStruct((bn, s_len, d), BF16),
                   jax.ShapeDtypeStruct((bn, s_len, hw), BF16),
                   jax.ShapeDtypeStruct((bn, s_len, hw), BF16),
                   jax.ShapeDtypeStruct((bn, s_len, vw), BF16)],
        compiler_params=pltpu.CompilerParams(
            dimension_semantics=("parallel", "arbitrary"), vmem_limit_bytes=VMEM_LIMIT),
        name="mixer_in",
    )(x, x, tabs, wx, wkr, wg, wconv, wco, qn, wq, kvn, wkv, dep)


def _attn_kernel(q_ref, k_ref, v_ref, dep_ref, o_ref, *, tq):
    s_len = q_ref.shape[1]
    n_q = s_len // tq
    lane = lax.broadcasted_iota(jnp.int32, (tq, 2 * V_HEAD_DIM), 1)
    causal = (lax.broadcasted_iota(jnp.int32, (tq, tq), 1)
              <= lax.broadcasted_iota(jnp.int32, (tq, tq), 0))
    for qi in range(n_q):
        past = qi * tq
        rows = slice(past, past + tq)
        outs = []
        for h in range(2):
            sl = slice(h * LANES, (h + 1) * LANES)
            qh = q_ref[0, rows, sl]
            s_d = jnp.where(causal, _dot_nt(qh, k_ref[0, rows, sl]), -jnp.inf)
            m = jnp.max(s_d, axis=-1, keepdims=True)
            if past:
                s_p = _dot_nt(qh, k_ref[0, :past, sl])
                m = jnp.maximum(m, jnp.max(s_p, axis=-1, keepdims=True))
            p_d = jnp.exp(s_d - m)
            den = jnp.sum(p_d, axis=-1, keepdims=True)
            acc = _dot(p_d.astype(BF16), v_ref[0, rows, :])
            if past:
                p_p = jnp.exp(s_p - m)
                den = den + jnp.sum(p_p, axis=-1, keepdims=True)
                acc = acc + _dot(p_p.astype(BF16), v_ref[0, :past, :])
            outs.append(acc / den)
        o_ref[0, rows, :] = jnp.where(lane < V_HEAD_DIM, outs[0], outs[1]).astype(BF16)


def _attention(q, k, v, tq, dep):
    bn, s_len, _ = q.shape
    n_pairs = MLA_HEADS // 2
    return pl.pallas_call(
        functools.partial(_attn_kernel, tq=tq),
        grid=(bn, n_pairs),
        in_specs=[pl.BlockSpec((1, s_len, 2 * LANES), lambda b, h: (b, 0, h)),
                  pl.BlockSpec((1, s_len, 2 * LANES), lambda b, h: (b, 0, h)),
                  pl.BlockSpec((1, s_len, 2 * V_HEAD_DIM), lambda b, h: (b, 0, h)), _DEP_SPEC],
        out_specs=pl.BlockSpec((1, s_len, 2 * V_HEAD_DIM), lambda b, h: (b, 0, h)),
        out_shape=jax.ShapeDtypeStruct((bn, s_len, MLA_HEADS * V_HEAD_DIM), BF16),
        compiler_params=pltpu.CompilerParams(
            dimension_semantics=("parallel", "parallel"), vmem_limit_bytes=VMEM_LIMIT),
        name="mla_attention",
    )(q, k, v, dep)


def _mem_kv_kernel(mem_ref, wk_ref, wv_ref, k_ref, v_ref):
    mb = mem_ref[0].astype(BF16)
    k_ref[0, 0] = _dot(mb, wk_ref[0]).astype(BF16)
    v_ref[0, 0] = _dot(mb, wv_ref[0]).astype(BF16)


def _mem_kv(mem, w_xk, w_xv):
    bn, m_len, d = mem.shape
    n_l = w_xk.shape[0]
    wspec = pl.BlockSpec((1, d, d), lambda l, b: (l, 0, 0))
    ospec = pl.BlockSpec((1, 1, m_len, d), lambda l, b: (l, b, 0, 0))
    return pl.pallas_call(
        _mem_kv_kernel,
        grid=(n_l, bn),
        in_specs=[pl.BlockSpec((1, m_len, d), lambda l, b: (b, 0, 0)), wspec, wspec],
        out_specs=[ospec, ospec],
        out_shape=[jax.ShapeDtypeStruct((n_l, bn, m_len, d), BF16)] * 2,
        compiler_params=pltpu.CompilerParams(
            dimension_semantics=("arbitrary", "arbitrary"), vmem_limit_bytes=VMEM_LIMIT),
        name="mem_kv",
    )(mem, w_xk, w_xv)


def _first_argmax(vals, idx):
    m = jnp.max(functools.reduce(jnp.maximum, vals), axis=0, keepdims=True)
    big = jnp.int32(1 << 20)
    cand = functools.reduce(jnp.minimum, [jnp.where(v == m, ix, big) for v, ix in zip(vals, idx)])
    first = jnp.min(cand, axis=0, keepdims=True)
    return [ix == first for ix in idx], m, first


def _route(logits_t, bias):
    n = logits_t.shape[1]
    scores = jax.nn.sigmoid(logits_t)
    sel = scores + bias
    sub = lax.broadcasted_iota(jnp.int32, (GROUP_SIZE, n), 0)
    neg = -jnp.inf
    sel_g = [sel[g * GROUP_SIZE:(g + 1) * GROUP_SIZE] for g in range(N_GROUPS)]
    gs = []
    for g in range(N_GROUPS):
        (hot,), m1, _ = _first_argmax([sel_g[g]], [sub])
        m2 = jnp.max(jnp.where(hot, neg, sel_g[g]), axis=0, keepdims=True)
        gs.append(m1 + m2)
    gsv = jnp.concatenate(gs, axis=0)
    gmask = jnp.zeros(gsv.shape, F32)
    for _ in range(TOPK_GROUPS):
        (hot,), _m, _ = _first_argmax([gsv], [sub])
        gmask = jnp.where(hot, 1.0, gmask)
        gsv = jnp.where(hot, neg, gsv)
    msel = [jnp.where(gmask[g:g + 1] > 0.0, sel_g[g], neg) for g in range(N_GROUPS)]
    eidx = [sub + g * GROUP_SIZE for g in range(N_GROUPS)]
    picks, ids = [], []
    for _ in range(TOP_K):
        hots, _m, first = _first_argmax(msel, eidx)
        picks.append(hots)
        ids.append(first)
        msel = [jnp.where(h, neg, v) for h, v in zip(hots, msel)]
    chosen = [functools.reduce(jnp.logical_or, [p[g] for p in picks]) for g in range(N_GROUPS)]
    sc_g = [scores[g * GROUP_SIZE:(g + 1) * GROUP_SIZE] for g in range(N_GROUPS)]
    picked = [jnp.where(c, s, 0.0) for c, s in zip(chosen, sc_g)]
    wsum = jnp.sum(functools.reduce(lambda a, b: a + b, picked), axis=0, keepdims=True)
    gates = [p / wsum * ROUTE_SCALE for p in picked]
    return picks, ids, chosen, gates


def _pick_rows(hots, vals):
    acc = functools.reduce(lambda a, b: a + b, [jnp.where(h, v, 0.0) for h, v in zip(hots, vals)])
    return jnp.sum(acc, axis=0, keepdims=True)


def _post_kernel(x_ref, yag_ref, sgb_ref, o_ref, km_ref, vm_ref, wao_ref, wmo_ref, ln1_ref,
                 wxq_ref, wxo_ref, ln2_ref, wr_ref, dep_ref,
                 x2p_ref, x2_ref, logit_ref, *, alpha):
    d = x_ref.shape[2]
    x = x_ref[0]
    y_b = _dot(o_ref[0], wao_ref[0])
    y = yag_ref[0].astype(F32) + sgb_ref[0].astype(F32) * y_b
    mix = _dot(y.astype(BF16), wmo_ref[0])
    x1 = _layer_norm(alpha * x + mix, ln1_ref[0, 0:1], ln1_ref[0, 1:2])

    hd = d // XATTN_HEADS
    xq = (_dot(x1.astype(BF16), wxq_ref[0]) * (hd ** -0.5)).astype(BF16)
    heads = []
    for h in range(XATTN_HEADS):
        sl = slice(h * hd, (h + 1) * hd)
        s = _dot_nt(xq[:, sl], km_ref[0, 0, :, sl])
        m = jnp.max(s, axis=-1, keepdims=True)
        p = jnp.exp(s - m)
        den = jnp.sum(p, axis=-1, keepdims=True)
        heads.append((_dot(p.astype(BF16), vm_ref[0, 0, :, sl]) / den).astype(BF16))
    xat = _dot(jnp.concatenate(heads, axis=-1), wxo_ref[0])
    x2 = _layer_norm(alpha * x1 + xat, ln2_ref[0, 0:1], ln2_ref[0, 1:2])

    x2b = x2.astype(BF16)
    x2l = (x2 - x2b.astype(F32)).astype(BF16)
    wr = wr_ref[0]
    wrh = wr.astype(BF16)
    wrl = (wr - wrh.astype(F32)).astype(BF16)
    logit_ref[0] = _dot_nt(wrh, x2b) + (_dot_nt(wrh, x2l) + _dot_nt(wrl, x2b))

    x2p_ref[0] = _pack_rows(x2)
    x2_ref[0] = x2


def _post(l, bn, x_b0, mem_b0, x, yag, sgb, o, kmem, vmem, wao, wmo, ln1, wxq, wxo, ln2, wr,
          ts, alpha, dep):
    _, s_len, d = x.shape
    m_len = kmem.shape[2]
    tile = lambda w: pl.BlockSpec((1, ts, w), lambda b, i: (b, i, 0))
    lay = lambda a: pl.BlockSpec((1,) + a.shape[1:], lambda b, i: (l,) + (0,) * (a.ndim - 1),
                                 pipeline_mode=pl.Buffered(1))
    memspec = pl.BlockSpec((1, 1, m_len, d), lambda b, i: (l, b + mem_b0, 0, 0))
    return pl.pallas_call(
        functools.partial(_post_kernel, alpha=alpha),
        grid=(bn, s_len // ts),
        in_specs=[pl.BlockSpec((1, ts, d), lambda b, i: (b + x_b0, i, 0)),
                  tile(d), tile(d), tile(o.shape[2]), memspec, memspec,
                  lay(wao), lay(wmo), lay(ln1), lay(wxq), lay(wxo), lay(ln2), lay(wr), _DEP_SPEC],
        out_specs=[tile(d // 2), tile(d),
                   pl.BlockSpec((1, N_EXPERTS, ts), lambda b, i: (b, 0, i))],
        out_shape=[jax.ShapeDtypeStruct((bn, s_len, d // 2), I32),
                   jax.ShapeDtypeStruct((bn, s_len, d), F32),
                   jax.ShapeDtypeStruct((bn, N_EXPERTS, s_len), F32)],
        compiler_params=pltpu.CompilerParams(
            dimension_semantics=("parallel", "arbitrary"), vmem_limit_bytes=VMEM_LIMIT),
        name="post_mixer",
    )(x, yag, sgb, o, kmem, vmem, wao, wmo, ln1, wxq, wxo, ln2, wr, dep)


RANK_BLOCK = 512


def _route_kernel(lg_ref, rb_ref, eid_ref, rank_ref, gate_ref, cnt_ref):
    @pl.when(pl.program_id(0) == 0)
    def _():
        cnt_ref[...] = jnp.zeros_like(cnt_ref)

    n = lg_ref.shape[2]
    blk = min(RANK_BLOCK, n)
    picks, ids, chosen, gates = _route(lg_ref[0], rb_ref[0])

    sel_t = jnp.concatenate([c.astype(F32) for c in chosen], axis=0)
    before = (lax.broadcasted_iota(jnp.int32, (blk, blk), 0)
              < lax.broadcasted_iota(jnp.int32, (blk, blk), 1)).astype(BF16)
    running = cnt_ref[:, 0:1]
    ranks = []
    for j in range(n // blk):
        sb = sel_t[:, j * blk:(j + 1) * blk]
        ranks.append(_dot(sb.astype(BF16), before) + running)
        running = running + jnp.sum(sb, axis=1, keepdims=True)
    cnt_ref[...] = jnp.broadcast_to(running, cnt_ref.shape)
    rank_all = jnp.concatenate(ranks, axis=1)
    rank_g = [rank_all[g * GROUP_SIZE:(g + 1) * GROUP_SIZE] for g in range(N_GROUPS)]
    eid_ref[0] = jnp.concatenate(ids, axis=0)
    rank_ref[0] = jnp.concatenate([_pick_rows(h, rank_g) for h in picks], axis=0).astype(I32)
    gate_ref[0] = jnp.concatenate([_pick_rows(h, gates) for h in picks], axis=0)


def _route_call(l, logits, rb):
    bn, n_e, s_len = logits.shape
    pick = pl.BlockSpec((1, TOP_K, s_len), lambda b: (b, 0, 0))
    return pl.pallas_call(
        _route_kernel,
        grid=(bn,),
        in_specs=[pl.BlockSpec((1, n_e, s_len), lambda b: (b, 0, 0)),
                  pl.BlockSpec((1,) + rb.shape[1:], lambda b: (l, 0, 0))],
        out_specs=[pick, pick, pick, pl.BlockSpec((n_e, LANES), lambda b: (0, 0))],
        out_shape=[jax.ShapeDtypeStruct((bn, TOP_K, s_len), I32),
                   jax.ShapeDtypeStruct((bn, TOP_K, s_len), I32),
                   jax.ShapeDtypeStruct((bn, TOP_K, s_len), F32),
                   jax.ShapeDtypeStruct((n_e, LANES), F32)],
        compiler_params=pltpu.CompilerParams(
            dimension_semantics=("arbitrary",), vmem_limit_bytes=VMEM_LIMIT),
        name="moe_route",
    )(logits, rb)


def _slot_kernel(offs_ref, eid_ref, rank_ref, slot_ref):
    eid = eid_ref[0]
    acc = rank_ref[0]
    for e in range(N_EXPERTS):
        acc = acc + jnp.where(eid == e, offs_ref[e], 0)
    slot_ref[0] = acc


def _slots(offs, eid, rank):
    bn, k, s_len = eid.shape
    spec = pl.BlockSpec((1, k, s_len), lambda b, offs_ref: (b, 0, 0))
    return pl.pallas_call(
        _slot_kernel,
        grid_spec=pltpu.PrefetchScalarGridSpec(
            num_scalar_prefetch=1, grid=(bn,), in_specs=[spec, spec], out_specs=spec),
        out_shape=jax.ShapeDtypeStruct((bn, k, s_len), I32),
        name="moe_slots",
    )(offs, eid, rank)


def _sc_mesh():
    return plsc.VectorSubcoreMesh(core_axis_name="c", subcore_axis_name="s")


def _sc_dispatch(xp, slot_chunks, n_slots, dep):
    t, w = xp.shape
    n_chunks, k, ch = slot_chunks.shape
    per_worker = n_chunks // (SC_CORES * SC_SUBCORES)

    @functools.partial(
        pl.kernel, mesh=_sc_mesh(),
        out_type=jax.ShapeDtypeStruct((n_slots, w), I32),
        scratch_types=[pltpu.VMEM((k, ch), I32), pltpu.VMEM((ch, w), I32), pltpu.SemaphoreType.DMA],
        name="moe_dispatch")
    def body(x_hbm, slot_hbm, dep_hbm, out_hbm, idx_v, rows_v, sem):
        worker = lax.axis_index("s") * SC_CORES + lax.axis_index("c")

        @pl.loop(0, per_worker)
        def _(j):
            c = worker * per_worker + j
            pltpu.sync_copy(slot_hbm.at[c], idx_v)
            pltpu.sync_copy(x_hbm.at[pl.ds(c * ch, ch)], rows_v)
            copies = [pltpu.async_copy(rows_v, out_hbm.at[idx_v.at[kk]], sem) for kk in range(k)]
            for cp in copies:
                cp.wait()

    return body(xp, slot_chunks, dep)


def _sc_combine(ys, slot_chunks, t):
    w = ys.shape[1]
    n_chunks, k, ch = slot_chunks.shape
    per_worker = n_chunks // (SC_CORES * SC_SUBCORES)

    @functools.partial(
        pl.kernel, mesh=_sc_mesh(),
        out_type=jax.ShapeDtypeStruct((k, t, w), I32),
        scratch_types=[pltpu.VMEM((k, ch), I32), pltpu.VMEM((ch, w), I32), pltpu.VMEM((ch, w), I32),
                       pltpu.SemaphoreType.DMA, pltpu.SemaphoreType.DMA,
                       pltpu.SemaphoreType.DMA, pltpu.SemaphoreType.DMA],
        name="moe_combine")
    def body(y_hbm, slot_hbm, out_hbm, idx_v, rows0, rows1, g0, g1, w0, w1):
        worker = lax.axis_index("s") * SC_CORES + lax.axis_index("c")
        bufs, gsem, wsem = (rows0, rows1), (g0, g1), (w0, w1)

        @pl.loop(0, per_worker)
        def _(j):
            c = worker * per_worker + j
            pltpu.sync_copy(slot_hbm.at[c], idx_v)
            gathers = [None] * k
            writes = [None] * k
            gathers[0] = pltpu.async_copy(y_hbm.at[idx_v.at[0]], bufs[0], gsem[0])
            for kk in range(k):
                cur = kk % 2
                if kk + 1 < k:
                    if kk >= 1:
                        writes[kk - 1].wait()
                    gathers[kk + 1] = pltpu.async_copy(
                        y_hbm.at[idx_v.at[kk + 1]], bufs[1 - cur], gsem[1 - cur])
                gathers[kk].wait()
                writes[kk] = pltpu.async_copy(
                    bufs[cur], out_hbm.at[kk, pl.ds(c * ch, ch)], wsem[cur])
            writes[k - 2].wait()
            writes[k - 1].wait()

    return body(ys, slot_chunks)


def _experts_kernel(ti_ref, te_ref, nv_ref, eo_ref, ne_ref, nu_ref, xs_ref, wg_hbm, wu_hbm, wd_hbm,
                    dep_ref, ys_ref, wg_f, wu_f, wd_f, wg_b, wu_b, wd_b, sem, *, layer):
    j = pl.program_id(0)
    expert = te_ref[j]
    slot = eo_ref[j]
    prev = te_ref[jnp.maximum(j - 1, 0)]
    new_expert = jnp.logical_or(j == 0, expert != prev)

    def weight_copies(e, s):
        return [pltpu.make_async_copy(hbm.at[layer, e], buf.at[s], sem.at[s, i])
                for i, (hbm, buf) in enumerate(((wg_hbm, wg_f), (wu_hbm, wu_f), (wd_hbm, wd_f)))]

    @pl.when(j == 0)
    def _():
        for cp in weight_copies(expert, slot):
            cp.start()

    @pl.when(new_expert)
    def _():
        nxt = ne_ref[j]

        @pl.when(nxt >= 0)
        def _():
            for cp in weight_copies(nxt, 1 - slot):
                cp.start()

        for cp in weight_copies(expert, slot):
            cp.wait()
        wg_b[...] = wg_f[slot].astype(BF16)
        wu_b[...] = wu_f[slot].astype(BF16)
        wd_b[...] = wd_f[slot].astype(BF16)

    def ffn(n_rows):
        rows = pl.ds(0, n_rows)
        xb = _unpack_rows(xs_ref[rows, :]).astype(BF16)
        hid = _silu(_dot(xb, wg_b[...])) * _dot(xb, wu_b[...])
        ys_ref[rows, :] = _pack_rows(_dot(hid.astype(BF16), wd_b[...]))

    used = j < nu_ref[0]
    half = xs_ref.shape[0] // 2
    many = nv_ref[j] > half

    @pl.when(jnp.logical_and(used, many))
    def _():
        ffn(2 * half)

    @pl.when(jnp.logical_and(used, jnp.logical_not(many)))
    def _():
        ffn(half)


def _experts(l, xs, sched, w_gate, w_up, w_down, dep):
    n_slots, w = xs.shape
    n_tiles = n_slots // EXPERT_TILE
    d, f = w_gate.shape[2:]
    rows = pl.BlockSpec((EXPERT_TILE, w), lambda j, ti, *_: (ti[j], 0))
    hbm = pl.BlockSpec(memory_space=pl.ANY)
    return pl.pallas_call(
        functools.partial(_experts_kernel, layer=l),
        grid_spec=pltpu.PrefetchScalarGridSpec(
            num_scalar_prefetch=len(sched), grid=(sched[-1][0],),
            in_specs=[rows, hbm, hbm, hbm, _DEP_SPEC],
            out_specs=rows,
            scratch_shapes=[pltpu.VMEM((2, d, f), F32), pltpu.VMEM((2, d, f), F32),
                            pltpu.VMEM((2, f, d), F32),
                            pltpu.VMEM((d, f), BF16), pltpu.VMEM((d, f), BF16),
                            pltpu.VMEM((f, d), BF16), pltpu.SemaphoreType.DMA((2, 3))]),
        out_shape=jax.ShapeDtypeStruct((n_slots, w), I32),
        compiler_params=pltpu.CompilerParams(
            dimension_semantics=("arbitrary",), vmem_limit_bytes=VMEM_LIMIT),
        name="moe_experts",
    )(*sched, xs, w_gate, w_up, w_down, dep)


def _final_kernel(yg_ref, gate_ref, x2_ref, ln_ref, wsgu_ref, wsd_ref, *rest, alpha):
    out_ref = rest[-1]
    x2 = x2_ref[0]
    sh = _dot(x2.astype(BF16), wsgu_ref[0])
    f = sh.shape[1] // 2
    hid = _silu(sh[:, :f]) * sh[:, f:]
    acc = alpha * x2 + _dot(hid.astype(BF16), wsd_ref[0])
    gates = gate_ref[0].T
    for k in range(yg_ref.shape[0]):
        acc = acc + gates[:, k:k + 1] * _unpack_rows(yg_ref[k, 0])
    out_ref[0] = _layer_norm(acc, ln_ref[0, 0:1], ln_ref[0, 1:2])


def _final(l, yg, gate8, x2, ln3, wsgu, wsd, tm, alpha, out_rows, out_b0, out_prev, dep):
    bn, s_len, d = x2.shape
    k = yg.shape[0]
    lay = lambda a: pl.BlockSpec((1,) + a.shape[1:], lambda b, i: (l, 0, 0),
                                 pipeline_mode=pl.Buffered(1))
    in_specs = [pl.BlockSpec((k, 1, tm, d // 2), lambda b, i: (0, b, i, 0)),
                pl.BlockSpec((1, k, tm), lambda b, i: (b, 0, i)),
                pl.BlockSpec((1, tm, d), lambda b, i: (b, i, 0)),
                lay(ln3), lay(wsgu), lay(wsd), _DEP_SPEC]
    args = [yg, gate8, x2, ln3, wsgu, wsd, dep]
    aliases = {}
    if out_prev is not None:
        in_specs.append(pl.BlockSpec(memory_space=pl.ANY))
        args.append(out_prev)
        aliases = {len(args) - 1: 0}
    return pl.pallas_call(
        functools.partial(_final_kernel, alpha=alpha),
        grid=(bn, s_len // tm),
        in_specs=in_specs,
        out_specs=pl.BlockSpec((1, tm, d), lambda b, i: (b + out_b0, i, 0)),
        out_shape=jax.ShapeDtypeStruct((out_rows, s_len, d), F32),
        input_output_aliases=aliases,
        compiler_params=pltpu.CompilerParams(
            dimension_semantics=("parallel", "parallel"), vmem_limit_bytes=VMEM_LIMIT),
        name="moe_final",
    )(*args)


def _moe_schedule(eid, rank, counts):
    bn, _, s_len = eid.shape
    t = bn * s_len
    n_tiles = (t * TOP_K) // EXPERT_TILE + N_EXPERTS

    cnt = counts[:, 0].astype(I32)
    padded = (cnt + EXPERT_TILE - 1) // EXPERT_TILE * EXPERT_TILE
    ends = jnp.cumsum(padded)
    offs = ends - padded
    n_used = ends[-1] // EXPERT_TILE
    ti = jnp.minimum(jnp.arange(n_tiles, dtype=I32), n_used - 1)
    te = jnp.sum((ends[None, :] <= (ti * EXPERT_TILE)[:, None]).astype(I32), axis=1)
    nv = jnp.take(cnt + offs, te) - ti * EXPERT_TILE
    nonempty = cnt > 0
    idx = jnp.arange(N_EXPERTS, dtype=I32)
    order = jnp.cumsum(nonempty.astype(I32)) - 1
    later = jnp.where(nonempty[None, :] & (idx[None, :] > idx[:, None]), idx[None, :], N_EXPERTS)
    succ = jnp.min(later, axis=1)
    succ = jnp.where(succ == N_EXPERTS, -1, succ).astype(I32)
    eo = jnp.take(order, te) % 2
    ne = jnp.take(succ, te)

    slots = _slots(offs, eid, rank)
    slot_chunks = slots.reshape(bn, TOP_K, s_len // SC_CHUNK, SC_CHUNK).transpose(0, 2, 1, 3)
    slot_chunks = slot_chunks.reshape(t // SC_CHUNK, TOP_K, SC_CHUNK)
    return slot_chunks, (ti, te, nv, eo, ne, n_used.reshape(1)), n_tiles * EXPERT_TILE


def _head_chunks(w, widths, n_heads, per_head):
    lead = w.shape[:-1]
    wh = w.reshape(lead + (n_heads, per_head))
    parts = [sign * wh[..., a:b] for a, b, sign in widths]
    used = sum(b - a for a, b, _ in widths)
    parts.append(jnp.zeros(lead + (n_heads, LANES - used), w.dtype))
    return jnp.concatenate(parts, axis=-1).reshape(lead + (n_heads * LANES,))


def kernel(x, mem, positions, w_in, w_conv, w_conv_out, q_norm, w_uq, kv_norm, w_uk, w_uv,
           w_attn_out, w_mix_out, ln1_g, ln1_b, w_xq, w_xk, w_xv, w_xo, ln2_g, ln2_b,
           w_router, router_bias, w_gate, w_up, w_down, ws_gate, ws_up, ws_down, ln3_g, ln3_b):
    bn, s_len, d = x.shape
    depth = w_in.shape[0]
    alpha = (2 * depth) ** 0.25
    ts = min(512, s_len)
    tq = min(512, s_len)
    half = QK_ROPE_DIM // 2
    nope, rope = QK_NOPE_DIM, QK_ROPE_DIM

    inv_freq = ROPE_BASE ** (-jnp.arange(half, dtype=F32) / half)
    ang = positions.astype(F32)[..., None] * inv_freq
    cos, sin = jnp.cos(ang), jnp.sin(ang)
    scale = (nope + rope) ** -0.5
    zeros = lambda w: jnp.zeros((bn, s_len, w), F32)
    tail = LANES - nope - rope
    tabs = jnp.concatenate([
        jnp.ones((bn, s_len, nope), F32), cos, cos, zeros(tail),
        zeros(nope), -sin, zeros(half), zeros(tail),
        zeros(nope), zeros(half), sin, zeros(tail)], axis=-1)

    sizes = (CONV_DIM, CONV_DIM, CONV_DIM, Q_RANK, KV_RANK, rope, d, d)
    offs = [0]
    for sz in sizes:
        offs.append(offs[-1] + sz)
    zpad = lambda w: jnp.zeros(w_in.shape[:2] + (w,), w_in.dtype)
    wx = w_in[:, :, :offs[5]].astype(BF16)
    wkr = jnp.concatenate([zpad(nope), w_in[:, :, offs[5]:offs[6]], zpad(tail)], axis=-1).astype(BF16)
    wg = w_in[:, :, offs[6]:].astype(BF16)

    per_q = nope + rope
    wq = _head_chunks(w_uq, [(0, per_q, scale)], MLA_HEADS, per_q).astype(BF16)
    wk = _head_chunks(w_uk, [(0, nope, 1.0)], MLA_HEADS, nope)
    wkv = jnp.concatenate([wk, w_uv], axis=-1).astype(BF16)

    wconv = w_conv[:, :, 0, :]
    wco = w_conv_out.astype(BF16)
    qn = q_norm[:, None, :]
    kvn = kv_norm[:, None, :]
    wao = w_attn_out.astype(BF16)
    wmo = w_mix_out.astype(BF16)
    ln1 = jnp.stack([ln1_g, ln1_b], axis=1)
    ln2 = jnp.stack([ln2_g, ln2_b], axis=1)
    ln3 = jnp.stack([ln3_g, ln3_b], axis=1)
    wxq = w_xq.astype(BF16)
    wxo = w_xo.astype(BF16)
    wr = jnp.swapaxes(w_router, 1, 2)
    rb = router_bias[:, :, None]
    wsgu = jnp.concatenate([ws_gate, ws_up], axis=-1).astype(BF16)
    wsd = ws_down.astype(BF16)

    kmem, vmem = _mem_kv(mem, w_xk.astype(BF16), w_xv.astype(BF16))

    assert bn % 2 == 0
    hb = bn // 2
    t = hb * s_len
    tm = min(512, s_len)

    def mix(l, c, xc, dep):
        x_b0 = c * hb if l == 0 else 0
        return _mixer_in(l, hb, x_b0, c * hb, xc, tabs, wx, wkr, wg, wconv, wco, qn, wq, kvn, wkv,
                         min(1024, s_len), dep)

    def attn(mx, dep):
        yag, sgb, q, k, v = mx
        return yag, sgb, _attention(q, k, v, tq, dep)

    def post_route(l, c, xc, ma, dep):
        x_b0 = c * hb if l == 0 else 0
        yag, sgb, o = ma
        x2p, x2, logits = _post(l, hb, x_b0, c * hb, xc, yag, sgb, o, kmem, vmem, wao, wmo, ln1,
                                wxq, wxo, ln2, wr, min(1024, s_len), alpha, dep)
        eid, rank, gate8, counts = _route_call(l, logits, rb)
        slot_chunks, sched, n_slots = _moe_schedule(eid, rank, counts)
        return dict(x2p=x2p.reshape(t, d // 2), x2=x2, gate8=gate8, slot_chunks=slot_chunks,
                    sched=sched, n_slots=n_slots)

    def dispatch(st, dep):
        st["xs"] = _sc_dispatch(st["x2p"], st["slot_chunks"], st["n_slots"], dep)

    def experts(l, st, dep):
        st["ys"] = _experts(l, st["xs"], st["sched"], w_gate, w_up, w_down, dep)
        st["yg"] = _sc_combine(st["ys"], st["slot_chunks"], t).reshape(TOP_K, hb, s_len, d // 2)

    def final(l, c, st, dep, out_prev):
        if l == depth - 1:
            return _final(l, st["yg"], st["gate8"], st["x2"], ln3, wsgu, wsd, tm, alpha, bn, c * hb,
                          out_prev, dep)
        return _final(l, st["yg"], st["gate8"], st["x2"], ln3, wsgu, wsd, tm, alpha, hb, 0, None, dep)

    xa = xb = x
    mx_a = mix(0, 0, xa, x)
    st_a = post_route(0, 0, xa, attn(mx_a, mx_a[0]), mx_a[0])
    dispatch(st_a, st_a["gate8"])
    for l in range(depth):
        more = l + 1 < depth
        mx_b = mix(l, 1, xb, st_a["gate8"])
        experts(l, st_a, mx_b[0])
        st_b = post_route(l, 1, xb, attn(mx_b, st_a["ys"]), st_a["ys"])
        xa = final(l, 0, st_a, st_b["gate8"], None)
        dispatch(st_b, xa if more else st_b["gate8"])
        if more:
            mx_a = mix(l + 1, 0, xa, xa)
        experts(l, st_b, mx_a[0] if more else xa)
        if more:
            st_a = post_route(l + 1, 0, xa, attn(mx_a, st_b["ys"]), st_b["ys"])
        xb = final(l, 1, st_b, st_a["gate8"] if more else st_b["ys"], xa if not more else None)
        if more:
            dispatch(st_a, xb)
    return xb
```

```python
import functools

import jax
import jax.numpy as jnp
from jax import lax
from jax.experimental import pallas as pl
from jax.experimental.pallas import tpu as pltpu
from jax.experimental.pallas import tpu_sc as plsc

CONV_DIM = 512
CONV_WIDTH = 3
MLA_HEADS = 8
QK_NOPE_DIM = 64
QK_ROPE_DIM = 32
V_HEAD_DIM = 64
Q_RANK = 384
KV_RANK = 256
ROPE_BASE = 10000.0
XATTN_HEADS = 4
N_EXPERTS = 64
TOP_K = 8
N_GROUPS = 8
TOPK_GROUPS = 4
GROUP_SIZE = N_EXPERTS // N_GROUPS
ROUTE_SCALE = 2.5
LN_EPS = 1e-5
RMS_EPS = 1e-6

LANES = 128
HALO_ROWS = 8
VMEM_LIMIT = 56 * 1024 * 1024
SC_CORES = 2
SC_SUBCORES = 16
SC_CHUNK = 64
EXPERT_TILE = 1024

BF16 = jnp.bfloat16
F32 = jnp.float32
I32 = jnp.int32


_DEP_SPEC = pl.BlockSpec(memory_space=pl.ANY)


def _dot(a, b):
    return jnp.dot(a, b, preferred_element_type=F32)


def _dot_nt(a, b):
    return lax.dot_general(a, b, (((1,), (1,)), ((), ())), preferred_element_type=F32)


def _layer_norm(x, g, b):
    mu = jnp.mean(x, axis=-1, keepdims=True)
    xc = x - mu
    var = jnp.mean(xc * xc, axis=-1, keepdims=True)
    return xc * lax.rsqrt(var + LN_EPS) * g + b


def _rms_norm(x, g):
    ms = jnp.mean(x * x, axis=-1, keepdims=True)
    return x * lax.rsqrt(ms + RMS_EPS) * g


def _silu(x):
    return x * jax.nn.sigmoid(x)


_HI_MASK = -65536


def _pack_rows(y):
    w = y.shape[1] // 2
    lo = lax.bitcast_convert_type(y[:, :w].astype(BF16).astype(F32), I32)
    hi = lax.bitcast_convert_type(y[:, w:].astype(BF16).astype(F32), I32)
    return lax.shift_right_logical(lo, 16) | (hi & _HI_MASK)


def _unpack_rows(p):
    lo = lax.bitcast_convert_type(lax.shift_left(p, 16), F32)
    hi = lax.bitcast_convert_type(p & _HI_MASK, F32)
    return jnp.concatenate([lo, hi], axis=-1)


_C_BCH = 0
_C_CQ = _C_BCH + 3 * CONV_DIM
_C_CKV = _C_CQ + Q_RANK


def _mixer_in_kernel(x_ref, xh_ref, tab_ref, wx_ref, wkr_ref, wg_ref, wconv_ref, wco_ref, qn_ref, wq_ref,
                     kvn_ref, wkv_ref, dep_ref,
                     yag_ref, sgb_ref, q_ref, k_ref, v_ref, *, d_model):
    i = pl.program_id(1)
    d = d_model
    ts = x_ref.shape[1]
    xb = x_ref[0].astype(BF16)

    bch = _dot(xb, wx_ref[:, _C_BCH:_C_BCH + 3 * CONV_DIM])
    b_gate = bch[:, :CONV_DIM]
    u = bch[:, CONV_DIM:2 * CONV_DIM] * bch[:, 2 * CONV_DIM:]
    xh = xh_ref[0].astype(BF16)
    hp = _dot(xh, wx_ref[:, _C_BCH + CONV_DIM:_C_BCH + 3 * CONV_DIM])
    uh = hp[:, :CONV_DIM] * hp[:, CONV_DIM:]
    uh = jnp.where(i == 0, 0.0, uh)
    row = lax.broadcasted_iota(jnp.int32, (ts, CONV_DIM), 0)
    u1 = jnp.where(row == 0, uh[HALO_ROWS - 1:HALO_ROWS], pltpu.roll(u, 1, 0))
    u2 = jnp.where(row == 0, uh[HALO_ROWS - 2:HALO_ROWS - 1],
                   jnp.where(row == 1, uh[HALO_ROWS - 1:HALO_ROWS], pltpu.roll(u, 2, 0)))
    wc = wconv_ref[0]
    z = wc[0:1] * u2 + wc[1:2] * u1 + wc[2:3] * u
    y_a = _dot((b_gate * z).astype(BF16), wco_ref[0])
    g_a = _dot(xb, wg_ref[:, :d])
    yag_ref[0] = (jax.nn.sigmoid(g_a) * y_a).astype(BF16)
    g_b = _dot(xb, wg_ref[:, d:])
    sgb_ref[0] = jax.nn.sigmoid(g_b).astype(BF16)

    tab = tab_ref[0]
    t_a, t_m, t_p = tab[:, 0:LANES], tab[:, LANES:2 * LANES], tab[:, 2 * LANES:3 * LANES]
    half = QK_ROPE_DIM // 2

    def rotate(c):
        return (c * t_a + pltpu.roll(c, LANES - half, 1) * t_m + pltpu.roll(c, half, 1) * t_p)

    c_q = _dot(xb, wx_ref[:, _C_CQ:_C_CQ + Q_RANK])
    cqn = _rms_norm(c_q, qn_ref[0]).astype(BF16)
    hw = MLA_HEADS * LANES
    q = _dot(cqn, wq_ref[0])
    for h in range(MLA_HEADS):
        sl = slice(h * LANES, (h + 1) * LANES)
        q_ref[0, :, sl] = rotate(q[:, sl]).astype(BF16)

    c_kv = _dot(xb, wx_ref[:, _C_CKV:_C_CKV + KV_RANK])
    ckvn = _rms_norm(c_kv, kvn_ref[0]).astype(BF16)
    kr_rot = rotate(_dot(xb, wkr_ref[...]))
    k_nope = _dot(ckvn, wkv_ref[0, :, :hw])
    for h in range(MLA_HEADS):
        sl = slice(h * LANES, (h + 1) * LANES)
        k_ref[0, :, sl] = (k_nope[:, sl] + kr_rot).astype(BF16)
    v_ref[0] = _dot(ckvn, wkv_ref[0, :, hw:]).astype(BF16)


def _mixer_in(l, bn, x_b0, tab_b0, x, tabs, wx, wkr, wg, wconv, wco, qn, wq, kvn, wkv, ts, dep):
    _, s_len, d = x.shape
    n_t = s_len // ts
    hw = MLA_HEADS * LANES
    vw = MLA_HEADS * V_HEAD_DIM
    tile = lambda w: pl.BlockSpec((1, ts, w), lambda b, i: (b, i, 0))
    once = pl.Buffered(1)
    lay = lambda a: pl.BlockSpec((1,) + a.shape[1:], lambda b, i: (l,) + (0,) * (a.ndim - 1),
                                 pipeline_mode=once)
    flat = lambda a: pl.BlockSpec((None,) + a.shape[1:], lambda b, i: (l, 0, 0), pipeline_mode=once)
    halo = pl.BlockSpec((1, HALO_ROWS, d),
                        lambda b, i: (b + x_b0, jnp.maximum(i * (ts // HALO_ROWS) - 1, 0), 0))
    return pl.pallas_call(
        functools.partial(_mixer_in_kernel, d_model=d),
        grid=(bn, n_t),
        in_specs=[pl.BlockSpec((1, ts, d), lambda b, i: (b + x_b0, i, 0)), halo,
                  pl.BlockSpec((1, ts, 3 * LANES), lambda b, i: (b + tab_b0, i, 0)),
                  flat(wx), flat(wkr), flat(wg),
                  lay(wconv), lay(wco), lay(qn), lay(wq), lay(kvn), lay(wkv), _DEP_SPEC],
        out_specs=[tile(d), tile(d), tile(hw), tile(hw), tile(vw)],
        out_shape=[jax.ShapeDtypeStruct((bn, s_len, d), BF16),
                   jax.ShapeDtypeStruct((bn, s_len, d), BF16),
                   jax.ShapeDtypeStruct((bn, s_len, hw), BF16),
                   jax.ShapeDtypeStruct((bn, s_len, hw), BF16),
                   jax.ShapeDtypeStruct((bn, s_len, vw), BF16)],
        compiler_params=pltpu.CompilerParams(
            dimension_semantics=("parallel", "arbitrary"), vmem_limit_bytes=VMEM_LIMIT),
        name="mixer_in",
    )(x, x, tabs, wx, wkr, wg, wconv, wco, qn, wq, kvn, wkv, dep)


def _rope_tables_kernel(cs_ref, place_ref, const_ref, tab_ref):
    c = cs_ref[0]
    hi = c.astype(BF16)
    r1 = c - hi.astype(F32)
    mid = r1.astype(BF16)
    lo = (r1 - mid.astype(F32)).astype(BF16)
    p = place_ref[...]
    tab_ref[0] = (_dot(hi, p) + _dot(mid, p)) + _dot(lo, p) + const_ref[...]


def _rope_tables(cs):
    bn, s_len, w = cs.shape
    half = w // 2
    nope = QK_NOPE_DIM
    src = jnp.arange(w)[:, None]
    dst = jnp.arange(3 * LANES)[None, :]
    lane, tab = dst % LANES, dst // LANES
    is_cos, f = src < half, src % half
    place = (jnp.where((tab == 0) & is_cos & ((lane == nope + f) | (lane == nope + half + f)), 1.0, 0.0)
             + jnp.where((tab == 1) & ~is_cos & (lane == nope + f), -1.0, 0.0)
             + jnp.where((tab == 2) & ~is_cos & (lane == nope + half + f), 1.0, 0.0)).astype(BF16)
    const = jnp.where((dst < nope), 1.0, 0.0).astype(F32)
    ts = min(2048, s_len)
    return pl.pallas_call(
        _rope_tables_kernel,
        grid=(bn, s_len // ts),
        in_specs=[pl.BlockSpec((1, ts, w), lambda b, i: (b, i, 0)),
                  pl.BlockSpec(place.shape, lambda b, i: (0, 0)),
                  pl.BlockSpec(const.shape, lambda b, i: (0, 0))],
        out_specs=pl.BlockSpec((1, ts, 3 * LANES), lambda b, i: (b, i, 0)),
        out_shape=jax.ShapeDtypeStruct((bn, s_len, 3 * LANES), F32),
        compiler_params=pltpu.CompilerParams(
            dimension_semantics=("parallel", "parallel"), vmem_limit_bytes=VMEM_LIMIT),
        name="rope_tables",
    )(cs, place, const)


def _attn_kernel(q_ref, k_ref, v_ref, dep_ref, o_ref, *, tq):
    s_len = q_ref.shape[1]
    n_q = s_len // tq
    lane = lax.broadcasted_iota(jnp.int32, (tq, 2 * V_HEAD_DIM), 1)
    causal = (lax.broadcasted_iota(jnp.int32, (tq, tq), 1)
              <= lax.broadcasted_iota(jnp.int32, (tq, tq), 0))
    for qi in range(n_q):
        past = qi * tq
        rows = slice(past, past + tq)
        outs = []
        for h in range(2):
            sl = slice(h * LANES, (h + 1) * LANES)
            qh = q_ref[0, rows, sl]
            s_d = jnp.where(causal, _dot_nt(qh, k_ref[0, rows, sl]), -jnp.inf)
            m = jnp.max(s_d, axis=-1, keepdims=True)
            if past:
                s_p = _dot_nt(qh, k_ref[0, :past, sl])
                m = jnp.maximum(m, jnp.max(s_p, axis=-1, keepdims=True))
            p_d = jnp.exp(s_d - m)
            den = jnp.sum(p_d, axis=-1, keepdims=True)
            acc = _dot(p_d.astype(BF16), v_ref[0, rows, :])
            if past:
                p_p = jnp.exp(s_p - m)
                den = den + jnp.sum(p_p, axis=-1, keepdims=True)
                acc = acc + _dot(p_p.astype(BF16), v_ref[0, :past, :])
            outs.append(acc / den)
        o_ref[0, rows, :] = jnp.where(lane < V_HEAD_DIM, outs[0], outs[1]).astype(BF16)


def _attention(q, k, v, tq, dep):
    bn, s_len, _ = q.shape
    n_pairs = MLA_HEADS // 2
    return pl.pallas_call(
        functools.partial(_attn_kernel, tq=tq),
        grid=(bn, n_pairs),
        in_specs=[pl.BlockSpec((1, s_len, 2 * LANES), lambda b, h: (b, 0, h)),
                  pl.BlockSpec((1, s_len, 2 * LANES), lambda b, h: (b, 0, h)),
                  pl.BlockSpec((1, s_len, 2 * V_HEAD_DIM), lambda b, h: (b, 0, h)), _DEP_SPEC],
        out_specs=pl.BlockSpec((1, s_len, 2 * V_HEAD_DIM), lambda b, h: (b, 0, h)),
        out_shape=jax.ShapeDtypeStruct((bn, s_len, MLA_HEADS * V_HEAD_DIM), BF16),
        compiler_params=pltpu.CompilerParams(
            dimension_semantics=("parallel", "parallel"), vmem_limit_bytes=VMEM_LIMIT),
        name="mla_attention",
    )(q, k, v, dep)


def _mem_kv_kernel(mem_ref, wk_ref, wv_ref, k_ref, v_ref):
    mb = mem_ref[0].astype(BF16)
    k_ref[0, 0] = _dot(mb, wk_ref[0]).astype(BF16)
    v_ref[0, 0] = _dot(mb, wv_ref[0]).astype(BF16)


def _mem_kv(mem, w_xk, w_xv):
    bn, m_len, d = mem.shape
    n_l = w_xk.shape[0]
    wspec = pl.BlockSpec((1, d, d), lambda l, b: (l, 0, 0))
    ospec = pl.BlockSpec((1, 1, m_len, d), lambda l, b: (l, b, 0, 0))
    return pl.pallas_call(
        _mem_kv_kernel,
        grid=(n_l, bn),
        in_specs=[pl.BlockSpec((1, m_len, d), lambda l, b: (b, 0, 0)), wspec, wspec],
        out_specs=[ospec, ospec],
        out_shape=[jax.ShapeDtypeStruct((n_l, bn, m_len, d), BF16)] * 2,
        compiler_params=pltpu.CompilerParams(
            dimension_semantics=("arbitrary", "arbitrary"), vmem_limit_bytes=VMEM_LIMIT),
        name="mem_kv",
    )(mem, w_xk, w_xv)


def _first_argmax(vals, idx):
    m = jnp.max(functools.reduce(jnp.maximum, vals), axis=0, keepdims=True)
    big = jnp.int32(1 << 20)
    cand = functools.reduce(jnp.minimum, [jnp.where(v == m, ix, big) for v, ix in zip(vals, idx)])
    first = jnp.min(cand, axis=0, keepdims=True)
    return [ix == first for ix in idx], m, first


def _route(logits_t, bias):
    n = logits_t.shape[1]
    scores = jax.nn.sigmoid(logits_t)
    sel = scores + bias
    sub = lax.broadcasted_iota(jnp.int32, (GROUP_SIZE, n), 0)
    neg = -jnp.inf
    sel_g = [sel[g * GROUP_SIZE:(g + 1) * GROUP_SIZE] for g in range(N_GROUPS)]
    gs = []
    for g in range(N_GROUPS):
        (hot,), m1, _ = _first_argmax([sel_g[g]], [sub])
        m2 = jnp.max(jnp.where(hot, neg, sel_g[g]), axis=0, keepdims=True)
        gs.append(m1 + m2)
    gsv = jnp.concatenate(gs, axis=0)
    gmask = jnp.zeros(gsv.shape, F32)
    for _ in range(TOPK_GROUPS):
        (hot,), _m, _ = _first_argmax([gsv], [sub])
        gmask = jnp.where(hot, 1.0, gmask)
        gsv = jnp.where(hot, neg, gsv)
    msel = [jnp.where(gmask[g:g + 1] > 0.0, sel_g[g], neg) for g in range(N_GROUPS)]
    eidx = [sub + g * GROUP_SIZE for g in range(N_GROUPS)]
    picks, ids = [], []
    for _ in range(TOP_K):
        hots, _m, first = _first_argmax(msel, eidx)
        picks.append(hots)
        ids.append(first)
        msel = [jnp.where(h, neg, v) for h, v in zip(hots, msel)]
    chosen = [functools.reduce(jnp.logical_or, [p[g] for p in picks]) for g in range(N_GROUPS)]
    sc_g = [scores[g * GROUP_SIZE:(g + 1) * GROUP_SIZE] for g in range(N_GROUPS)]
    picked = [jnp.where(c, s, 0.0) for c, s in zip(chosen, sc_g)]
    wsum = jnp.sum(functools.reduce(lambda a, b: a + b, picked), axis=0, keepdims=True)
    gates = [p / wsum * ROUTE_SCALE for p in picked]
    return picks, ids, chosen, gates


def _pick_rows(hots, vals):
    acc = functools.reduce(lambda a, b: a + b, [jnp.where(h, v, 0.0) for h, v in zip(hots, vals)])
    return jnp.sum(acc, axis=0, keepdims=True)


def _post_kernel(x_ref, yag_ref, sgb_ref, o_ref, km_ref, vm_ref, wao_ref, wmo_ref, ln1_ref,
                 wxq_ref, wxo_ref, ln2_ref, wr_ref, dep_ref,
                 x2p_ref, x2_ref, logit_ref, *, alpha):
    d = x_ref.shape[2]
    x = x_ref[0]
    y_b = _dot(o_ref[0], wao_ref[0])
    y = yag_ref[0].astype(F32) + sgb_ref[0].astype(F32) * y_b
    mix = _dot(y.astype(BF16), wmo_ref[0])
    x1 = _layer_norm(alpha * x + mix, ln1_ref[0, 0:1], ln1_ref[0, 1:2])

    hd = d // XATTN_HEADS
    xq = (_dot(x1.astype(BF16), wxq_ref[0]) * (hd ** -0.5)).astype(BF16)
    heads = []
    for h in range(XATTN_HEADS):
        sl = slice(h * hd, (h + 1) * hd)
        s = _dot_nt(xq[:, sl], km_ref[0, 0, :, sl])
        m = jnp.max(s, axis=-1, keepdims=True)
        p = jnp.exp(s - m)
        den = jnp.sum(p, axis=-1, keepdims=True)
        heads.append((_dot(p.astype(BF16), vm_ref[0, 0, :, sl]) / den).astype(BF16))
    xat = _dot(jnp.concatenate(heads, axis=-1), wxo_ref[0])
    x2 = _layer_norm(alpha * x1 + xat, ln2_ref[0, 0:1], ln2_ref[0, 1:2])

    x2b = x2.astype(BF16)
    x2l = (x2 - x2b.astype(F32)).astype(BF16)
    wr = wr_ref[0]
    wrh = wr.astype(BF16)
    wrl = (wr - wrh.astype(F32)).astype(BF16)
    logit_ref[0] = _dot_nt(wrh, x2b) + (_dot_nt(wrh, x2l) + _dot_nt(wrl, x2b))

    x2p_ref[0] = _pack_rows(x2)
    x2_ref[0] = x2


def _post(l, bn, x_b0, mem_b0, x, yag, sgb, o, kmem, vmem, wao, wmo, ln1, wxq, wxo, ln2, wr,
          ts, alpha, dep):
    _, s_len, d = x.shape
    m_len = kmem.shape[2]
    tile = lambda w: pl.BlockSpec((1, ts, w), lambda b, i: (b, i, 0))
    lay = lambda a: pl.BlockSpec((1,) + a.shape[1:], lambda b, i: (l,) + (0,) * (a.ndim - 1),
                                 pipeline_mode=pl.Buffered(1))
    memspec = pl.BlockSpec((1, 1, m_len, d), lambda b, i: (l, b + mem_b0, 0, 0))
    return pl.pallas_call(
        functools.partial(_post_kernel, alpha=alpha),
        grid=(bn, s_len // ts),
        in_specs=[pl.BlockSpec((1, ts, d), lambda b, i: (b + x_b0, i, 0)),
                  tile(d), tile(d), tile(o.shape[2]), memspec, memspec,
                  lay(wao), lay(wmo), lay(ln1), lay(wxq), lay(wxo), lay(ln2), lay(wr), _DEP_SPEC],
        out_specs=[tile(d // 2), tile(d),
                   pl.BlockSpec((1, N_EXPERTS, ts), lambda b, i: (b, 0, i))],
        out_shape=[jax.ShapeDtypeStruct((bn, s_len, d // 2), I32),
                   jax.ShapeDtypeStruct((bn, s_len, d), F32),
                   jax.ShapeDtypeStruct((bn, N_EXPERTS, s_len), F32)],
        compiler_params=pltpu.CompilerParams(
            dimension_semantics=("parallel", "arbitrary"), vmem_limit_bytes=VMEM_LIMIT),
        name="post_mixer",
    )(x, yag, sgb, o, kmem, vmem, wao, wmo, ln1, wxq, wxo, ln2, wr, dep)


RANK_BLOCK = 512


def _route_kernel(lg_ref, rb_ref, eid_ref, rank_ref, gate_ref, cnt_ref):
    @pl.when(pl.program_id(0) == 0)
    def _():
        cnt_ref[...] = jnp.zeros_like(cnt_ref)

    n = lg_ref.shape[2]
    blk = min(RANK_BLOCK, n)
    picks, ids, chosen, gates = _route(lg_ref[0], rb_ref[0])

    sel_t = jnp.concatenate([c.astype(F32) for c in chosen], axis=0)
    before = (lax.broadcasted_iota(jnp.int32, (blk, blk), 0)
              < lax.broadcasted_iota(jnp.int32, (blk, blk), 1)).astype(BF16)
    running = cnt_ref[:, 0:1]
    ranks = []
    for j in range(n // blk):
        sb = sel_t[:, j * blk:(j + 1) * blk]
        ranks.append(_dot(sb.astype(BF16), before) + running)
        running = running + jnp.sum(sb, axis=1, keepdims=True)
    cnt_ref[...] = jnp.broadcast_to(running, cnt_ref.shape)
    rank_all = jnp.concatenate(ranks, axis=1)
    rank_g = [rank_all[g * GROUP_SIZE:(g + 1) * GROUP_SIZE] for g in range(N_GROUPS)]
    eid_ref[0] = jnp.concatenate(ids, axis=0)
    rank_ref[0] = jnp.concatenate([_pick_rows(h, rank_g) for h in picks], axis=0).astype(I32)
    gate_ref[0] = jnp.concatenate([_pick_rows(h, gates) for h in picks], axis=0)


def _route_call(l, logits, rb):
    bn, n_e, s_len = logits.shape
    pick = pl.BlockSpec((1, TOP_K, s_len), lambda b: (b, 0, 0))
    return pl.pallas_call(
        _route_kernel,
        grid=(bn,),
        in_specs=[pl.BlockSpec((1, n_e, s_len), lambda b: (b, 0, 0)),
                  pl.BlockSpec((1,) + rb.shape[1:], lambda b: (l, 0, 0))],
        out_specs=[pick, pick, pick, pl.BlockSpec((n_e, LANES), lambda b: (0, 0))],
        out_shape=[jax.ShapeDtypeStruct((bn, TOP_K, s_len), I32),
                   jax.ShapeDtypeStruct((bn, TOP_K, s_len), I32),
                   jax.ShapeDtypeStruct((bn, TOP_K, s_len), F32),
                   jax.ShapeDtypeStruct((n_e, LANES), F32)],
        compiler_params=pltpu.CompilerParams(
            dimension_semantics=("arbitrary",), vmem_limit_bytes=VMEM_LIMIT),
        name="moe_route",
    )(logits, rb)


def _slot_kernel(offs_ref, eid_ref, rank_ref, slot_ref):
    eid = eid_ref[0]
    acc = rank_ref[0]
    for e in range(N_EXPERTS):
        acc = acc + jnp.where(eid == e, offs_ref[e], 0)
    slot_ref[0] = acc


def _slots(offs, eid, rank):
    bn, k, s_len = eid.shape
    spec = pl.BlockSpec((1, k, s_len), lambda b, offs_ref: (b, 0, 0))
    return pl.pallas_call(
        _slot_kernel,
        grid_spec=pltpu.PrefetchScalarGridSpec(
            num_scalar_prefetch=1, grid=(bn,), in_specs=[spec, spec], out_specs=spec),
        out_shape=jax.ShapeDtypeStruct((bn, k, s_len), I32),
        name="moe_slots",
    )(offs, eid, rank)


def _sc_mesh():
    return plsc.VectorSubcoreMesh(core_axis_name="c", subcore_axis_name="s")


def _sc_dispatch(xp, slot_chunks, n_slots, dep):
    t, w = xp.shape
    n_chunks, k, ch = slot_chunks.shape
    per_worker = n_chunks // (SC_CORES * SC_SUBCORES)

    @functools.partial(
        pl.kernel, mesh=_sc_mesh(),
        out_type=jax.ShapeDtypeStruct((n_slots, w), I32),
        scratch_types=[pltpu.VMEM((k, ch), I32), pltpu.VMEM((ch, w), I32), pltpu.SemaphoreType.DMA],
        name="moe_dispatch")
    def body(x_hbm, slot_hbm, dep_hbm, out_hbm, idx_v, rows_v, sem):
        worker = lax.axis_index("s") * SC_CORES + lax.axis_index("c")

        @pl.loop(0, per_worker)
        def _(j):
            c = worker * per_worker + j
            pltpu.sync_copy(slot_hbm.at[c], idx_v)
            pltpu.sync_copy(x_hbm.at[pl.ds(c * ch, ch)], rows_v)
            copies = [pltpu.async_copy(rows_v, out_hbm.at[idx_v.at[kk]], sem) for kk in range(k)]
            for cp in copies:
                cp.wait()

    return body(xp, slot_chunks, dep)


def _sc_combine(ys, slot_chunks, t):
    w = ys.shape[1]
    n_chunks, k, ch = slot_chunks.shape
    per_worker = n_chunks // (SC_CORES * SC_SUBCORES)

    @functools.partial(
        pl.kernel, mesh=_sc_mesh(),
        out_type=jax.ShapeDtypeStruct((k, t, w), I32),
        scratch_types=[pltpu.VMEM((k, ch), I32), pltpu.VMEM((ch, w), I32), pltpu.VMEM((ch, w), I32),
                       pltpu.SemaphoreType.DMA, pltpu.SemaphoreType.DMA,
                       pltpu.SemaphoreType.DMA, pltpu.SemaphoreType.DMA],
        name="moe_combine")
    def body(y_hbm, slot_hbm, out_hbm, idx_v, rows0, rows1, g0, g1, w0, w1):
        worker = lax.axis_index("s") * SC_CORES + lax.axis_index("c")
        bufs, gsem, wsem = (rows0, rows1), (g0, g1), (w0, w1)

        @pl.loop(0, per_worker)
        def _(j):
            c = worker * per_worker + j
            pltpu.sync_copy(slot_hbm.at[c], idx_v)
            gathers = [None] * k
            writes = [None] * k
            gathers[0] = pltpu.async_copy(y_hbm.at[idx_v.at[0]], bufs[0], gsem[0])
            for kk in range(k):
                cur = kk % 2
                if kk + 1 < k:
                    if kk >= 1:
                        writes[kk - 1].wait()
                    gathers[kk + 1] = pltpu.async_copy(
                        y_hbm.at[idx_v.at[kk + 1]], bufs[1 - cur], gsem[1 - cur])
                gathers[kk].wait()
                writes[kk] = pltpu.async_copy(
                    bufs[cur], out_hbm.at[kk, pl.ds(c * ch, ch)], wsem[cur])
            writes[k - 2].wait()
            writes[k - 1].wait()

    return body(ys, slot_chunks)


def _experts_kernel(ti_ref, te_ref, nv_ref, eo_ref, ne_ref, nu_ref, xs_ref, wg_hbm, wu_hbm, wd_hbm,
                    dep_ref, ys_ref, wg_f, wu_f, wd_f, wg_b, wu_b, wd_b, sem, *, layer):
    j = pl.program_id(0)
    expert = te_ref[j]
    slot = eo_ref[j]
    prev = te_ref[jnp.maximum(j - 1, 0)]
    new_expert = jnp.logical_or(j == 0, expert != prev)

    def weight_copies(e, s):
        return [pltpu.make_async_copy(hbm.at[layer, e], buf.at[s], sem.at[s, i])
                for i, (hbm, buf) in enumerate(((wg_hbm, wg_f), (wu_hbm, wu_f), (wd_hbm, wd_f)))]

    @pl.when(j == 0)
    def _():
        for cp in weight_copies(expert, slot):
            cp.start()

    @pl.when(new_expert)
    def _():
        nxt = ne_ref[j]

        @pl.when(nxt >= 0)
        def _():
            for cp in weight_copies(nxt, 1 - slot):
                cp.start()

        for cp in weight_copies(expert, slot):
            cp.wait()
        wg_b[...] = wg_f[slot].astype(BF16)
        wu_b[...] = wu_f[slot].astype(BF16)
        wd_b[...] = wd_f[slot].astype(BF16)

    def ffn(n_rows):
        rows = pl.ds(0, n_rows)
        xb = _unpack_rows(xs_ref[rows, :]).astype(BF16)
        hid = _silu(_dot(xb, wg_b[...])) * _dot(xb, wu_b[...])
        ys_ref[rows, :] = _pack_rows(_dot(hid.astype(BF16), wd_b[...]))

    used = j < nu_ref[0]
    half = xs_ref.shape[0] // 2
    many = nv_ref[j] > half

    @pl.when(jnp.logical_and(used, many))
    def _():
        ffn(2 * half)

    @pl.when(jnp.logical_and(used, jnp.logical_not(many)))
    def _():
        ffn(half)


def _experts(l, xs, sched, w_gate, w_up, w_down, dep):
    n_slots, w = xs.shape
    n_tiles = n_slots // EXPERT_TILE
    d, f = w_gate.shape[2:]
    rows = pl.BlockSpec((EXPERT_TILE, w), lambda j, ti, *_: (ti[j], 0))
    hbm = pl.BlockSpec(memory_space=pl.ANY)
    return pl.pallas_call(
        functools.partial(_experts_kernel, layer=l),
        grid_spec=pltpu.PrefetchScalarGridSpec(
            num_scalar_prefetch=len(sched), grid=(sched[-1][0],),
            in_specs=[rows, hbm, hbm, hbm, _DEP_SPEC],
            out_specs=rows,
            scratch_shapes=[pltpu.VMEM((2, d, f), F32), pltpu.VMEM((2, d, f), F32),
                            pltpu.VMEM((2, f, d), F32),
                            pltpu.VMEM((d, f), BF16), pltpu.VMEM((d, f), BF16),
                            pltpu.VMEM((f, d), BF16), pltpu.SemaphoreType.DMA((2, 3))]),
        out_shape=jax.ShapeDtypeStruct((n_slots, w), I32),
        compiler_params=pltpu.CompilerParams(
            dimension_semantics=("arbitrary",), vmem_limit_bytes=VMEM_LIMIT),
        name="moe_experts",
    )(*sched, xs, w_gate, w_up, w_down, dep)


def _final_kernel(yg_ref, gate_ref, x2_ref, ln_ref, wsgu_ref, wsd_ref, *rest, alpha):
    out_ref = rest[-1]
    x2 = x2_ref[0]
    sh = _dot(x2.astype(BF16), wsgu_ref[0])
    f = sh.shape[1] // 2
    hid = _silu(sh[:, :f]) * sh[:, f:]
    acc = alpha * x2 + _dot(hid.astype(BF16), wsd_ref[0])
    gates = gate_ref[0].T
    for k in range(yg_ref.shape[0]):
        acc = acc + gates[:, k:k + 1] * _unpack_rows(yg_ref[k, 0])
    out_ref[0] = _layer_norm(acc, ln_ref[0, 0:1], ln_ref[0, 1:2])


def _final(l, yg, gate8, x2, ln3, wsgu, wsd, tm, alpha, out_rows, out_b0, out_prev, dep):
    bn, s_len, d = x2.shape
    k = yg.shape[0]
    lay = lambda a: pl.BlockSpec((1,) + a.shape[1:], lambda b, i: (l, 0, 0),
                                 pipeline_mode=pl.Buffered(1))
    in_specs = [pl.BlockSpec((k, 1, tm, d // 2), lambda b, i: (0, b, i, 0)),
                pl.BlockSpec((1, k, tm), lambda b, i: (b, 0, i)),
                pl.BlockSpec((1, tm, d), lambda b, i: (b, i, 0)),
                lay(ln3), lay(wsgu), lay(wsd), _DEP_SPEC]
    args = [yg, gate8, x2, ln3, wsgu, wsd, dep]
    aliases = {}
    if out_prev is not None:
        in_specs.append(pl.BlockSpec(memory_space=pl.ANY))
        args.append(out_prev)
        aliases = {len(args) - 1: 0}
    return pl.pallas_call(
        functools.partial(_final_kernel, alpha=alpha),
        grid=(bn, s_len // tm),
        in_specs=in_specs,
        out_specs=pl.BlockSpec((1, tm, d), lambda b, i: (b + out_b0, i, 0)),
        out_shape=jax.ShapeDtypeStruct((out_rows, s_len, d), F32),
        input_output_aliases=aliases,
        compiler_params=pltpu.CompilerParams(
            dimension_semantics=("parallel", "parallel"), vmem_limit_bytes=VMEM_LIMIT),
        name="moe_final",
    )(*args)


def _moe_schedule(eid, rank, counts):
    bn, _, s_len = eid.shape
    t = bn * s_len
    n_tiles = (t * TOP_K) // EXPERT_TILE + N_EXPERTS

    cnt = counts[:, 0].astype(I32)
    padded = (cnt + EXPERT_TILE - 1) // EXPERT_TILE * EXPERT_TILE
    ends = jnp.cumsum(padded)
    offs = ends - padded
    n_used = ends[-1] // EXPERT_TILE
    ti = jnp.minimum(jnp.arange(n_tiles, dtype=I32), n_used - 1)
    te = jnp.sum((ends[None, :] <= (ti * EXPERT_TILE)[:, None]).astype(I32), axis=1)
    nv = jnp.take(cnt + offs, te) - ti * EXPERT_TILE
    nonempty = cnt > 0
    idx = jnp.arange(N_EXPERTS, dtype=I32)
    order = jnp.cumsum(nonempty.astype(I32)) - 1
    later = jnp.where(nonempty[None, :] & (idx[None, :] > idx[:, None]), idx[None, :], N_EXPERTS)
    succ = jnp.min(later, axis=1)
    succ = jnp.where(succ == N_EXPERTS, -1, succ).astype(I32)
    eo = jnp.take(order, te) % 2
    ne = jnp.take(succ, te)

    slots = _slots(offs, eid, rank)
    slot_chunks = slots.reshape(bn, TOP_K, s_len // SC_CHUNK, SC_CHUNK).transpose(0, 2, 1, 3)
    slot_chunks = slot_chunks.reshape(t // SC_CHUNK, TOP_K, SC_CHUNK)
    return slot_chunks, (ti, te, nv, eo, ne, n_used.reshape(1)), n_tiles * EXPERT_TILE


def _head_chunks(w, widths, n_heads, per_head):
    lead = w.shape[:-1]
    wh = w.reshape(lead + (n_heads, per_head))
    parts = [sign * wh[..., a:b] for a, b, sign in widths]
    used = sum(b - a for a, b, _ in widths)
    parts.append(jnp.zeros(lead + (n_heads, LANES - used), w.dtype))
    return jnp.concatenate(parts, axis=-1).reshape(lead + (n_heads * LANES,))


def kernel(x, mem, positions, w_in, w_conv, w_conv_out, q_norm, w_uq, kv_norm, w_uk, w_uv,
           w_attn_out, w_mix_out, ln1_g, ln1_b, w_xq, w_xk, w_xv, w_xo, ln2_g, ln2_b,
           w_router, router_bias, w_gate, w_up, w_down, ws_gate, ws_up, ws_down, ln3_g, ln3_b):
    bn, s_len, d = x.shape
    depth = w_in.shape[0]
    alpha = (2 * depth) ** 0.25
    ts = min(512, s_len)
    tq = min(512, s_len)
    half = QK_ROPE_DIM // 2
    nope, rope = QK_NOPE_DIM, QK_ROPE_DIM

    inv_freq = ROPE_BASE ** (-jnp.arange(half, dtype=F32) / half)
    ang = positions.astype(F32)[..., None] * inv_freq
    cos, sin = jnp.cos(ang), jnp.sin(ang)
    scale = (nope + rope) ** -0.5
    tail = LANES - nope - rope
    tabs = _rope_tables(jnp.concatenate([cos, sin], axis=-1))

    sizes = (CONV_DIM, CONV_DIM, CONV_DIM, Q_RANK, KV_RANK, rope, d, d)
    offs = [0]
    for sz in sizes:
        offs.append(offs[-1] + sz)
    zpad = lambda w: jnp.zeros(w_in.shape[:2] + (w,), w_in.dtype)
    wx = w_in[:, :, :offs[5]].astype(BF16)
    wkr = jnp.concatenate([zpad(nope), w_in[:, :, offs[5]:offs[6]], zpad(tail)], axis=-1).astype(BF16)
    wg = w_in[:, :, offs[6]:].astype(BF16)

    per_q = nope + rope
    wq = _head_chunks(w_uq, [(0, per_q, scale)], MLA_HEADS, per_q).astype(BF16)
    wk = _head_chunks(w_uk, [(0, nope, 1.0)], MLA_HEADS, nope)
    wkv = jnp.concatenate([wk, w_uv], axis=-1).astype(BF16)

    wconv = w_conv[:, :, 0, :]
    wco = w_conv_out.astype(BF16)
    qn = q_norm[:, None, :]
    kvn = kv_norm[:, None, :]
    wao = w_attn_out.astype(BF16)
    wmo = w_mix_out.astype(BF16)
    ln1 = jnp.stack([ln1_g, ln1_b], axis=1)
    ln2 = jnp.stack([ln2_g, ln2_b], axis=1)
    ln3 = jnp.stack([ln3_g, ln3_b], axis=1)
    wxq = w_xq.astype(BF16)
    wxo = w_xo.astype(BF16)
    wr = jnp.swapaxes(w_router, 1, 2)
    rb = router_bias[:, :, None]
    wsgu = jnp.concatenate([ws_gate, ws_up], axis=-1).astype(BF16)
    wsd = ws_down.astype(BF16)

    kmem, vmem = _mem_kv(mem, w_xk.astype(BF16), w_xv.astype(BF16))

    assert bn % 2 == 0
    hb = bn // 2
    t = hb * s_len
    tm = min(512, s_len)

    def mix(l, c, xc, dep):
        x_b0 = c * hb if l == 0 else 0
        return _mixer_in(l, hb, x_b0, c * hb, xc, tabs, wx, wkr, wg, wconv, wco, qn, wq, kvn, wkv,
                         min(1024, s_len), dep)

    def attn(mx, dep):
        yag, sgb, q, k, v = mx
        return yag, sgb, _attention(q, k, v, tq, dep)

    def post_route(l, c, xc, ma, dep):
        x_b0 = c * hb if l == 0 else 0
        yag, sgb, o = ma
        x2p, x2, logits = _post(l, hb, x_b0, c * hb, xc, yag, sgb, o, kmem, vmem, wao, wmo, ln1,
                                wxq, wxo, ln2, wr, min(1024, s_len), alpha, dep)
        eid, rank, gate8, counts = _route_call(l, logits, rb)
        slot_chunks, sched, n_slots = _moe_schedule(eid, rank, counts)
        return dict(x2p=x2p.reshape(t, d // 2), x2=x2, gate8=gate8, slot_chunks=slot_chunks,
                    sched=sched, n_slots=n_slots)

    def dispatch(st, dep):
        st["xs"] = _sc_dispatch(st["x2p"], st["slot_chunks"], st["n_slots"], dep)

    def experts(l, st, dep):
        st["ys"] = _experts(l, st["xs"], st["sched"], w_gate, w_up, w_down, dep)
        st["yg"] = _sc_combine(st["ys"], st["slot_chunks"], t).reshape(TOP_K, hb, s_len, d // 2)

    def final(l, c, st, dep, out_prev):
        if l == depth - 1:
            return _final(l, st["yg"], st["gate8"], st["x2"], ln3, wsgu, wsd, tm, alpha, bn, c * hb,
                          out_prev, dep)
        return _final(l, st["yg"], st["gate8"], st["x2"], ln3, wsgu, wsd, tm, alpha, hb, 0, None, dep)

    xa = xb = x
    mx_a = mix(0, 0, xa, x)
    st_a = post_route(0, 0, xa, attn(mx_a, mx_a[0]), mx_a[0])
    dispatch(st_a, st_a["gate8"])
    for l in range(depth):
        more = l + 1 < depth
        mx_b = mix(l, 1, xb, st_a["gate8"])
        experts(l, st_a, mx_b[0])
        st_b = post_route(l, 1, xb, attn(mx_b, st_a["ys"]), st_a["ys"])
        xa = final(l, 0, st_a, st_b["gate8"], None)
        dispatch(st_b, xa if more else st_b["gate8"])
        if more:
            mx_a = mix(l + 1, 0, xa, xa)
        experts(l, st_b, mx_a[0] if more else xa)
        if more:
            st_a = post_route(l + 1, 0, xa, attn(mx_a, st_b["ys"]), st_b["ys"])
        xb = final(l, 1, st_b, st_a["gate8"] if more else st_b["ys"], xa if not more else None)
        if more:
            dispatch(st_a, xb)
    return xb
```

```python
import functools

import jax
import jax.numpy as jnp
from jax import lax
from jax.experimental import pallas as pl
from jax.experimental.pallas import tpu as pltpu
from jax.experimental.pallas import tpu_sc as plsc

CONV_DIM = 512
CONV_WIDTH = 3
MLA_HEADS = 8
QK_NOPE_DIM = 64
QK_ROPE_DIM = 32
V_HEAD_DIM = 64
Q_RANK = 384
KV_RANK = 256
ROPE_BASE = 10000.0
XATTN_HEADS = 4
N_EXPERTS = 64
TOP_K = 8
N_GROUPS = 8
TOPK_GROUPS = 4
GROUP_SIZE = N_EXPERTS // N_GROUPS
ROUTE_SCALE = 2.5
LN_EPS = 1e-5
RMS_EPS = 1e-6

LANES = 128
HALO_ROWS = 8
VMEM_LIMIT = 56 * 1024 * 1024
SC_CORES = 2
SC_SUBCORES = 16
SC_CHUNK = 64
ROW_TILE = 1024
ATTN_Q_TILE = 512
FINAL_TILE = 512
TABLE_TILE = 2048
EXPERT_TILE = 2048
EXPERT_ROWS = 512

BF16 = jnp.bfloat16
F32 = jnp.float32
I32 = jnp.int32


_DEP_SPEC = pl.BlockSpec(memory_space=pl.ANY)


def _dot(a, b):
    return jnp.dot(a, b, preferred_element_type=F32)


def _dot_nt(a, b):
    return lax.dot_general(a, b, (((1,), (1,)), ((), ())), preferred_element_type=F32)


def _layer_norm(x, g, b):
    mu = jnp.mean(x, axis=-1, keepdims=True)
    xc = x - mu
    var = jnp.mean(xc * xc, axis=-1, keepdims=True)
    return xc * lax.rsqrt(var + LN_EPS) * g + b


def _rms_norm(x, g):
    ms = jnp.mean(x * x, axis=-1, keepdims=True)
    return x * lax.rsqrt(ms + RMS_EPS) * g


def _silu(x):
    return x * jax.nn.sigmoid(x)


_HI_MASK = -65536


def _pack_rows(y):
    w = y.shape[1] // 2
    lo = lax.bitcast_convert_type(y[:, :w].astype(BF16).astype(F32), I32)
    hi = lax.bitcast_convert_type(y[:, w:].astype(BF16).astype(F32), I32)
    return lax.shift_right_logical(lo, 16) | (hi & _HI_MASK)


def _unpack_rows(p):
    lo = lax.bitcast_convert_type(lax.shift_left(p, 16), F32)
    hi = lax.bitcast_convert_type(p & _HI_MASK, F32)
    return jnp.concatenate([lo, hi], axis=-1)


_C_BCH = 0
_C_CQ = _C_BCH + 3 * CONV_DIM
_C_CKV = _C_CQ + Q_RANK


def _mixer_in_kernel(x_ref, xh_ref, tab_ref, wx_ref, wkr_ref, wg_ref, wconv_ref, wco_ref, qn_ref, wq_ref,
                     kvn_ref, wkv_ref, dep_ref,
                     yag_ref, sgb_ref, q_ref, k_ref, v_ref, *, d_model):
    i = pl.program_id(1)
    d = d_model
    ts = x_ref.shape[1]
    xb = x_ref[0].astype(BF16)

    bch = _dot(xb, wx_ref[:, _C_BCH:_C_BCH + 3 * CONV_DIM])
    b_gate = bch[:, :CONV_DIM]
    u = bch[:, CONV_DIM:2 * CONV_DIM] * bch[:, 2 * CONV_DIM:]
    xh = xh_ref[0].astype(BF16)
    hp = _dot(xh, wx_ref[:, _C_BCH + CONV_DIM:_C_BCH + 3 * CONV_DIM])
    uh = hp[:, :CONV_DIM] * hp[:, CONV_DIM:]
    uh = jnp.where(i == 0, 0.0, uh)
    row = lax.broadcasted_iota(jnp.int32, (ts, CONV_DIM), 0)
    u1 = jnp.where(row == 0, uh[HALO_ROWS - 1:HALO_ROWS], pltpu.roll(u, 1, 0))
    u2 = jnp.where(row == 0, uh[HALO_ROWS - 2:HALO_ROWS - 1],
                   jnp.where(row == 1, uh[HALO_ROWS - 1:HALO_ROWS], pltpu.roll(u, 2, 0)))
    wc = wconv_ref[0]
    z = wc[0:1] * u2 + wc[1:2] * u1 + wc[2:3] * u
    y_a = _dot((b_gate * z).astype(BF16), wco_ref[0])
    g_a = _dot(xb, wg_ref[:, :d])
    yag_ref[0] = (jax.nn.sigmoid(g_a) * y_a).astype(BF16)
    g_b = _dot(xb, wg_ref[:, d:])
    sgb_ref[0] = jax.nn.sigmoid(g_b).astype(BF16)

    tab = tab_ref[0]
    t_a, t_m, t_p = tab[:, 0:LANES], tab[:, LANES:2 * LANES], tab[:, 2 * LANES:3 * LANES]
    half = QK_ROPE_DIM // 2

    def rotate(c):
        return (c * t_a + pltpu.roll(c, LANES - half, 1) * t_m + pltpu.roll(c, half, 1) * t_p)

    c_q = _dot(xb, wx_ref[:, _C_CQ:_C_CQ + Q_RANK])
    cqn = _rms_norm(c_q, qn_ref[0]).astype(BF16)
    hw = MLA_HEADS * LANES
    q = _dot(cqn, wq_ref[0])
    for h in range(MLA_HEADS):
        sl = slice(h * LANES, (h + 1) * LANES)
        q_ref[0, :, sl] = rotate(q[:, sl]).astype(BF16)

    c_kv = _dot(xb, wx_ref[:, _C_CKV:_C_CKV + KV_RANK])
    ckvn = _rms_norm(c_kv, kvn_ref[0]).astype(BF16)
    kr_rot = rotate(_dot(xb, wkr_ref[...]))
    k_nope = _dot(ckvn, wkv_ref[0, :, :hw])
    for h in range(MLA_HEADS):
        sl = slice(h * LANES, (h + 1) * LANES)
        k_ref[0, :, sl] = (k_nope[:, sl] + kr_rot).astype(BF16)
    v_ref[0] = _dot(ckvn, wkv_ref[0, :, hw:]).astype(BF16)


def _mixer_in(l, bn, x_b0, tab_b0, x, tabs, wx, wkr, wg, wconv, wco, qn, wq, kvn, wkv, ts, dep):
    _, s_len, d = x.shape
    n_t = s_len // ts
    hw = MLA_HEADS * LANES
    vw = MLA_HEADS * V_HEAD_DIM
    tile = lambda w: pl.BlockSpec((1, ts, w), lambda b, i: (b, i, 0))
    once = pl.Buffered(1)
    lay = lambda a: pl.BlockSpec((1,) + a.shape[1:], lambda b, i: (l,) + (0,) * (a.ndim - 1),
                                 pipeline_mode=once)
    flat = lambda a: pl.BlockSpec((None,) + a.shape[1:], lambda b, i: (l, 0, 0), pipeline_mode=once)
    halo = pl.BlockSpec((1, HALO_ROWS, d),
                        lambda b, i: (b + x_b0, jnp.maximum(i * (ts // HALO_ROWS) - 1, 0), 0))
    return pl.pallas_call(
        functools.partial(_mixer_in_kernel, d_model=d),
        grid=(bn, n_t),
        in_specs=[pl.BlockSpec((1, ts, d), lambda b, i: (b + x_b0, i, 0)), halo,
                  pl.BlockSpec((1, ts, 3 * LANES), lambda b, i: (b + tab_b0, i, 0)),
                  flat(wx), flat(wkr), flat(wg),
                  lay(wconv), lay(wco), lay(qn), lay(wq), lay(kvn), lay(wkv), _DEP_SPEC],
        out_specs=[tile(d), tile(d), tile(hw), tile(hw), tile(vw)],
        out_shape=[jax.ShapeDtypeStruct((bn, s_len, d), BF16),
                   jax.ShapeDtypeStruct((bn, s_len, d), BF16),
                   jax.ShapeDtypeStruct((bn, s_len, hw), BF16),
                   jax.ShapeDtypeStruct((bn, s_len, hw), BF16),
                   jax.ShapeDtypeStruct((bn, s_len, vw), BF16)],
        compiler_params=pltpu.CompilerParams(
            dimension_semantics=("parallel", "arbitrary"), vmem_limit_bytes=VMEM_LIMIT),
        name="mixer_in",
    )(x, x, tabs, wx, wkr, wg, wconv, wco, qn, wq, kvn, wkv, dep)


def _rope_tables_kernel(cs_ref, place_ref, const_ref, tab_ref):
    c = cs_ref[0]
    hi = c.astype(BF16)
    r1 = c - hi.astype(F32)
    mid = r1.astype(BF16)
    lo = (r1 - mid.astype(F32)).astype(BF16)
    p = place_ref[...]
    tab_ref[0] = (_dot(hi, p) + _dot(mid, p)) + _dot(lo, p) + const_ref[...]


def _rope_tables(cs):
    bn, s_len, w = cs.shape
    half = w // 2
    nope = QK_NOPE_DIM
    src = jnp.arange(w)[:, None]
    dst = jnp.arange(3 * LANES)[None, :]
    lane, tab = dst % LANES, dst // LANES
    is_cos, f = src < half, src % half
    place = (jnp.where((tab == 0) & is_cos & ((lane == nope + f) | (lane == nope + half + f)), 1.0, 0.0)
             + jnp.where((tab == 1) & ~is_cos & (lane == nope + f), -1.0, 0.0)
             + jnp.where((tab == 2) & ~is_cos & (lane == nope + half + f), 1.0, 0.0)).astype(BF16)
    const = jnp.where((dst < nope), 1.0, 0.0).astype(F32)
    ts = min(TABLE_TILE, s_len)
    return pl.pallas_call(
        _rope_tables_kernel,
        grid=(bn, s_len // ts),
        in_specs=[pl.BlockSpec((1, ts, w), lambda b, i: (b, i, 0)),
                  pl.BlockSpec(place.shape, lambda b, i: (0, 0)),
                  pl.BlockSpec(const.shape, lambda b, i: (0, 0))],
        out_specs=pl.BlockSpec((1, ts, 3 * LANES), lambda b, i: (b, i, 0)),
        out_shape=jax.ShapeDtypeStruct((bn, s_len, 3 * LANES), F32),
        compiler_params=pltpu.CompilerParams(
            dimension_semantics=("parallel", "parallel"), vmem_limit_bytes=VMEM_LIMIT),
        name="rope_tables",
    )(cs, place, const)


def _attn_kernel(q_ref, k_ref, v_ref, dep_ref, o_ref, *, tq):
    s_len = q_ref.shape[1]
    n_q = s_len // tq
    lane = lax.broadcasted_iota(jnp.int32, (tq, 2 * V_HEAD_DIM), 1)
    causal = (lax.broadcasted_iota(jnp.int32, (tq, tq), 1)
              <= lax.broadcasted_iota(jnp.int32, (tq, tq), 0))
    for qi in range(n_q):
        past = qi * tq
        rows = slice(past, past + tq)
        outs = []
        for h in range(2):
            sl = slice(h * LANES, (h + 1) * LANES)
            qh = q_ref[0, rows, sl]
            s_d = jnp.where(causal, _dot_nt(qh, k_ref[0, rows, sl]), -jnp.inf)
            m = jnp.max(s_d, axis=-1, keepdims=True)
            if past:
                s_p = _dot_nt(qh, k_ref[0, :past, sl])
                m = jnp.maximum(m, jnp.max(s_p, axis=-1, keepdims=True))
            p_d = jnp.exp(s_d - m)
            den = jnp.sum(p_d, axis=-1, keepdims=True)
            acc = _dot(p_d.astype(BF16), v_ref[0, rows, :])
            if past:
                p_p = jnp.exp(s_p - m)
                den = den + jnp.sum(p_p, axis=-1, keepdims=True)
                acc = acc + _dot(p_p.astype(BF16), v_ref[0, :past, :])
            outs.append(acc / den)
        o_ref[0, rows, :] = jnp.where(lane < V_HEAD_DIM, outs[0], outs[1]).astype(BF16)


def _attention(q, k, v, tq, dep):
    bn, s_len, _ = q.shape
    n_pairs = MLA_HEADS // 2
    return pl.pallas_call(
        functools.partial(_attn_kernel, tq=tq),
        grid=(bn, n_pairs),
        in_specs=[pl.BlockSpec((1, s_len, 2 * LANES), lambda b, h: (b, 0, h)),
                  pl.BlockSpec((1, s_len, 2 * LANES), lambda b, h: (b, 0, h)),
                  pl.BlockSpec((1, s_len, 2 * V_HEAD_DIM), lambda b, h: (b, 0, h)), _DEP_SPEC],
        out_specs=pl.BlockSpec((1, s_len, 2 * V_HEAD_DIM), lambda b, h: (b, 0, h)),
        out_shape=jax.ShapeDtypeStruct((bn, s_len, MLA_HEADS * V_HEAD_DIM), BF16),
        compiler_params=pltpu.CompilerParams(
            dimension_semantics=("parallel", "parallel"), vmem_limit_bytes=VMEM_LIMIT),
        name="mla_attention",
    )(q, k, v, dep)


def _mem_kv_kernel(mem_ref, wk_ref, wv_ref, k_ref, v_ref):
    mb = mem_ref[0].astype(BF16)
    k_ref[0, 0] = _dot(mb, wk_ref[0]).astype(BF16)
    v_ref[0, 0] = _dot(mb, wv_ref[0]).astype(BF16)


def _mem_kv(mem, w_xk, w_xv):
    bn, m_len, d = mem.shape
    n_l = w_xk.shape[0]
    wspec = pl.BlockSpec((1, d, d), lambda l, b: (l, 0, 0))
    ospec = pl.BlockSpec((1, 1, m_len, d), lambda l, b: (l, b, 0, 0))
    return pl.pallas_call(
        _mem_kv_kernel,
        grid=(n_l, bn),
        in_specs=[pl.BlockSpec((1, m_len, d), lambda l, b: (b, 0, 0)), wspec, wspec],
        out_specs=[ospec, ospec],
        out_shape=[jax.ShapeDtypeStruct((n_l, bn, m_len, d), BF16)] * 2,
        compiler_params=pltpu.CompilerParams(
            dimension_semantics=("arbitrary", "arbitrary"), vmem_limit_bytes=VMEM_LIMIT),
        name="mem_kv",
    )(mem, w_xk, w_xv)


def _first_argmax(vals, idx):
    m = jnp.max(functools.reduce(jnp.maximum, vals), axis=0, keepdims=True)
    big = jnp.int32(1 << 20)
    cand = functools.reduce(jnp.minimum, [jnp.where(v == m, ix, big) for v, ix in zip(vals, idx)])
    first = jnp.min(cand, axis=0, keepdims=True)
    return [ix == first for ix in idx], m, first


def _route(logits_t, bias):
    n = logits_t.shape[1]
    scores = jax.nn.sigmoid(logits_t)
    sel = scores + bias
    sub = lax.broadcasted_iota(jnp.int32, (GROUP_SIZE, n), 0)
    neg = -jnp.inf
    sel_g = [sel[g * GROUP_SIZE:(g + 1) * GROUP_SIZE] for g in range(N_GROUPS)]
    gs = []
    for g in range(N_GROUPS):
        (hot,), m1, _ = _first_argmax([sel_g[g]], [sub])
        m2 = jnp.max(jnp.where(hot, neg, sel_g[g]), axis=0, keepdims=True)
        gs.append(m1 + m2)
    gsv = jnp.concatenate(gs, axis=0)
    gmask = jnp.zeros(gsv.shape, F32)
    for _ in range(TOPK_GROUPS):
        (hot,), _m, _ = _first_argmax([gsv], [sub])
        gmask = jnp.where(hot, 1.0, gmask)
        gsv = jnp.where(hot, neg, gsv)
    msel = [jnp.where(gmask[g:g + 1] > 0.0, sel_g[g], neg) for g in range(N_GROUPS)]
    eidx = [sub + g * GROUP_SIZE for g in range(N_GROUPS)]
    picks, ids = [], []
    for _ in range(TOP_K):
        hots, _m, first = _first_argmax(msel, eidx)
        picks.append(hots)
        ids.append(first)
        msel = [jnp.where(h, neg, v) for h, v in zip(hots, msel)]
    chosen = [functools.reduce(jnp.logical_or, [p[g] for p in picks]) for g in range(N_GROUPS)]
    sc_g = [scores[g * GROUP_SIZE:(g + 1) * GROUP_SIZE] for g in range(N_GROUPS)]
    picked = [jnp.where(c, s, 0.0) for c, s in zip(chosen, sc_g)]
    wsum = jnp.sum(functools.reduce(lambda a, b: a + b, picked), axis=0, keepdims=True)
    gates = [p / wsum * ROUTE_SCALE for p in picked]
    return picks, ids, chosen, gates


def _pick_rows(hots, vals):
    acc = functools.reduce(lambda a, b: a + b, [jnp.where(h, v, 0.0) for h, v in zip(hots, vals)])
    return jnp.sum(acc, axis=0, keepdims=True)


def _post_kernel(x_ref, yag_ref, sgb_ref, o_ref, km_ref, vm_ref, wao_ref, wmo_ref, ln1_ref,
                 wxq_ref, wxo_ref, ln2_ref, wr_ref, dep_ref,
                 x2p_ref, x2_ref, logit_ref, *, alpha):
    d = x_ref.shape[2]
    x = x_ref[0]
    y_b = _dot(o_ref[0], wao_ref[0])
    y = yag_ref[0].astype(F32) + sgb_ref[0].astype(F32) * y_b
    mix = _dot(y.astype(BF16), wmo_ref[0])
    x1 = _layer_norm(alpha * x + mix, ln1_ref[0, 0:1], ln1_ref[0, 1:2])

    hd = d // XATTN_HEADS
    xq = (_dot(x1.astype(BF16), wxq_ref[0]) * (hd ** -0.5)).astype(BF16)
    heads = []
    for h in range(XATTN_HEADS):
        sl = slice(h * hd, (h + 1) * hd)
        s = _dot_nt(xq[:, sl], km_ref[0, 0, :, sl])
        m = jnp.max(s, axis=-1, keepdims=True)
        p = jnp.exp(s - m)
        den = jnp.sum(p, axis=-1, keepdims=True)
        heads.append((_dot(p.astype(BF16), vm_ref[0, 0, :, sl]) / den).astype(BF16))
    xat = _dot(jnp.concatenate(heads, axis=-1), wxo_ref[0])
    x2 = _layer_norm(alpha * x1 + xat, ln2_ref[0, 0:1], ln2_ref[0, 1:2])

    x2b = x2.astype(BF16)
    x2l = (x2 - x2b.astype(F32)).astype(BF16)
    wr = wr_ref[0]
    wrh = wr.astype(BF16)
    wrl = (wr - wrh.astype(F32)).astype(BF16)
    logit_ref[0] = _dot_nt(wrh, x2b) + (_dot_nt(wrh, x2l) + _dot_nt(wrl, x2b))

    x2p_ref[0] = _pack_rows(x2)
    x2_ref[0] = x2


def _post(l, bn, x_b0, mem_b0, x, yag, sgb, o, kmem, vmem, wao, wmo, ln1, wxq, wxo, ln2, wr,
          ts, alpha, dep):
    _, s_len, d = x.shape
    m_len = kmem.shape[2]
    tile = lambda w: pl.BlockSpec((1, ts, w), lambda b, i: (b, i, 0))
    lay = lambda a: pl.BlockSpec((1,) + a.shape[1:], lambda b, i: (l,) + (0,) * (a.ndim - 1),
                                 pipeline_mode=pl.Buffered(1))
    memspec = pl.BlockSpec((1, 1, m_len, d), lambda b, i: (l, b + mem_b0, 0, 0))
    return pl.pallas_call(
        functools.partial(_post_kernel, alpha=alpha),
        grid=(bn, s_len // ts),
        in_specs=[pl.BlockSpec((1, ts, d), lambda b, i: (b + x_b0, i, 0)),
                  tile(d), tile(d), tile(o.shape[2]), memspec, memspec,
                  lay(wao), lay(wmo), lay(ln1), lay(wxq), lay(wxo), lay(ln2), lay(wr), _DEP_SPEC],
        out_specs=[tile(d // 2), tile(d),
                   pl.BlockSpec((1, N_EXPERTS, ts), lambda b, i: (b, 0, i))],
        out_shape=[jax.ShapeDtypeStruct((bn, s_len, d // 2), I32),
                   jax.ShapeDtypeStruct((bn, s_len, d), F32),
                   jax.ShapeDtypeStruct((bn, N_EXPERTS, s_len), F32)],
        compiler_params=pltpu.CompilerParams(
            dimension_semantics=("parallel", "arbitrary"), vmem_limit_bytes=VMEM_LIMIT),
        name="post_mixer",
    )(x, yag, sgb, o, kmem, vmem, wao, wmo, ln1, wxq, wxo, ln2, wr, dep)


RANK_BLOCK = 512


def _route_kernel(lg_ref, rb_ref, eid_ref, rank_ref, gate_ref, cnt_ref):
    @pl.when(pl.program_id(0) == 0)
    def _():
        cnt_ref[...] = jnp.zeros_like(cnt_ref)

    n = lg_ref.shape[2]
    blk = min(RANK_BLOCK, n)
    picks, ids, chosen, gates = _route(lg_ref[0], rb_ref[0])

    sel_t = jnp.concatenate([c.astype(F32) for c in chosen], axis=0)
    before = (lax.broadcasted_iota(jnp.int32, (blk, blk), 0)
              < lax.broadcasted_iota(jnp.int32, (blk, blk), 1)).astype(BF16)
    running = cnt_ref[:, 0:1]
    ranks = []
    for j in range(n // blk):
        sb = sel_t[:, j * blk:(j + 1) * blk]
        ranks.append(_dot(sb.astype(BF16), before) + running)
        running = running + jnp.sum(sb, axis=1, keepdims=True)
    cnt_ref[...] = jnp.broadcast_to(running, cnt_ref.shape)
    rank_all = jnp.concatenate(ranks, axis=1)
    rank_g = [rank_all[g * GROUP_SIZE:(g + 1) * GROUP_SIZE] for g in range(N_GROUPS)]
    eid_ref[0] = jnp.concatenate(ids, axis=0)
    rank_ref[0] = jnp.concatenate([_pick_rows(h, rank_g) for h in picks], axis=0).astype(I32)
    gate_ref[0] = jnp.concatenate([_pick_rows(h, gates) for h in picks], axis=0)


def _route_call(l, logits, rb):
    bn, n_e, s_len = logits.shape
    pick = pl.BlockSpec((1, TOP_K, s_len), lambda b: (b, 0, 0))
    return pl.pallas_call(
        _route_kernel,
        grid=(bn,),
        in_specs=[pl.BlockSpec((1, n_e, s_len), lambda b: (b, 0, 0)),
                  pl.BlockSpec((1,) + rb.shape[1:], lambda b: (l, 0, 0))],
        out_specs=[pick, pick, pick, pl.BlockSpec((n_e, LANES), lambda b: (0, 0))],
        out_shape=[jax.ShapeDtypeStruct((bn, TOP_K, s_len), I32),
                   jax.ShapeDtypeStruct((bn, TOP_K, s_len), I32),
                   jax.ShapeDtypeStruct((bn, TOP_K, s_len), F32),
                   jax.ShapeDtypeStruct((n_e, LANES), F32)],
        compiler_params=pltpu.CompilerParams(
            dimension_semantics=("arbitrary",), vmem_limit_bytes=VMEM_LIMIT),
        name="moe_route",
    )(logits, rb)


def _slot_kernel(offs_ref, eid_ref, rank_ref, slot_ref):
    eid = eid_ref[0]
    acc = rank_ref[0]
    for e in range(N_EXPERTS):
        acc = acc + jnp.where(eid == e, offs_ref[e], 0)
    slot_ref[0] = acc


def _slots(offs, eid, rank):
    bn, k, s_len = eid.shape
    spec = pl.BlockSpec((1, k, s_len), lambda b, offs_ref: (b, 0, 0))
    return pl.pallas_call(
        _slot_kernel,
        grid_spec=pltpu.PrefetchScalarGridSpec(
            num_scalar_prefetch=1, grid=(bn,), in_specs=[spec, spec], out_specs=spec),
        out_shape=jax.ShapeDtypeStruct((bn, k, s_len), I32),
        name="moe_slots",
    )(offs, eid, rank)


def _sc_mesh():
    return plsc.VectorSubcoreMesh(core_axis_name="c", subcore_axis_name="s")


def _sc_dispatch(xp, slot_chunks, n_slots, dep):
    t, w = xp.shape
    n_chunks, k, ch = slot_chunks.shape
    per_worker = n_chunks // (SC_CORES * SC_SUBCORES)

    @functools.partial(
        pl.kernel, mesh=_sc_mesh(),
        out_type=jax.ShapeDtypeStruct((n_slots, w), I32),
        scratch_types=[pltpu.VMEM((k, ch), I32), pltpu.VMEM((ch, w), I32), pltpu.SemaphoreType.DMA],
        name="moe_dispatch")
    def body(x_hbm, slot_hbm, dep_hbm, out_hbm, idx_v, rows_v, sem):
        worker = lax.axis_index("s") * SC_CORES + lax.axis_index("c")

        @pl.loop(0, per_worker)
        def _(j):
            c = worker * per_worker + j
            pltpu.sync_copy(slot_hbm.at[c], idx_v)
            pltpu.sync_copy(x_hbm.at[pl.ds(c * ch, ch)], rows_v)
            copies = [pltpu.async_copy(rows_v, out_hbm.at[idx_v.at[kk]], sem) for kk in range(k)]
            for cp in copies:
                cp.wait()

    return body(xp, slot_chunks, dep)


def _sc_combine(ys, slot_chunks, t):
    w = ys.shape[1]
    n_chunks, k, ch = slot_chunks.shape
    per_worker = n_chunks // (SC_CORES * SC_SUBCORES)

    @functools.partial(
        pl.kernel, mesh=_sc_mesh(),
        out_type=jax.ShapeDtypeStruct((k, t, w), I32),
        scratch_types=[pltpu.VMEM((k, ch), I32), pltpu.VMEM((ch, w), I32), pltpu.VMEM((ch, w), I32),
                       pltpu.SemaphoreType.DMA, pltpu.SemaphoreType.DMA,
                       pltpu.SemaphoreType.DMA, pltpu.SemaphoreType.DMA],
        name="moe_combine")
    def body(y_hbm, slot_hbm, out_hbm, idx_v, rows0, rows1, g0, g1, w0, w1):
        worker = lax.axis_index("s") * SC_CORES + lax.axis_index("c")
        bufs, gsem, wsem = (rows0, rows1), (g0, g1), (w0, w1)

        @pl.loop(0, per_worker)
        def _(j):
            c = worker * per_worker + j
            pltpu.sync_copy(slot_hbm.at[c], idx_v)
            gathers = [None] * k
            writes = [None] * k
            gathers[0] = pltpu.async_copy(y_hbm.at[idx_v.at[0]], bufs[0], gsem[0])
            for kk in range(k):
                cur = kk % 2
                if kk + 1 < k:
                    if kk >= 1:
                        writes[kk - 1].wait()
                    gathers[kk + 1] = pltpu.async_copy(
                        y_hbm.at[idx_v.at[kk + 1]], bufs[1 - cur], gsem[1 - cur])
                gathers[kk].wait()
                writes[kk] = pltpu.async_copy(
                    bufs[cur], out_hbm.at[kk, pl.ds(c * ch, ch)], wsem[cur])
            writes[k - 2].wait()
            writes[k - 1].wait()

    return body(ys, slot_chunks)


def _experts_kernel(ti_ref, te_ref, nv_ref, eo_ref, ne_ref, nu_ref, xs_ref, wg_hbm, wu_hbm, wd_hbm,
                    dep_ref, ys_ref, wg_f, wu_f, wd_f, wg_b, wu_b, wd_b, sem, *, layer):
    j = pl.program_id(0)
    expert = te_ref[j]
    slot = eo_ref[j]
    prev = te_ref[jnp.maximum(j - 1, 0)]
    new_expert = jnp.logical_or(j == 0, expert != prev)

    def weight_copies(e, s):
        return [pltpu.make_async_copy(hbm.at[layer, e], buf.at[s], sem.at[s, i])
                for i, (hbm, buf) in enumerate(((wg_hbm, wg_f), (wu_hbm, wu_f), (wd_hbm, wd_f)))]

    @pl.when(j == 0)
    def _():
        for cp in weight_copies(expert, slot):
            cp.start()

    @pl.when(new_expert)
    def _():
        nxt = ne_ref[j]

        @pl.when(nxt >= 0)
        def _():
            for cp in weight_copies(nxt, 1 - slot):
                cp.start()

        for cp in weight_copies(expert, slot):
            cp.wait()
        wg_b[...] = wg_f[slot].astype(BF16)
        wu_b[...] = wu_f[slot].astype(BF16)
        wd_b[...] = wd_f[slot].astype(BF16)

    def ffn(n_rows):
        rows = pl.ds(0, n_rows)
        xb = _unpack_rows(xs_ref[rows, :]).astype(BF16)
        hid = _silu(_dot(xb, wg_b[...])) * _dot(xb, wu_b[...])
        ys_ref[rows, :] = _pack_rows(_dot(hid.astype(BF16), wd_b[...]))

    used = j < nu_ref[0]
    n_paths = xs_ref.shape[0] // EXPERT_ROWS
    groups = jnp.minimum((nv_ref[j] + EXPERT_ROWS - 1) // EXPERT_ROWS, n_paths)
    for g in range(1, n_paths + 1):
        @pl.when(jnp.logical_and(used, groups == g))
        def _(g=g):
            ffn(g * EXPERT_ROWS)


def _experts(l, xs, sched, w_gate, w_up, w_down, dep):
    n_slots, w = xs.shape
    n_tiles = n_slots // EXPERT_TILE
    d, f = w_gate.shape[2:]
    rows = pl.BlockSpec((EXPERT_TILE, w), lambda j, ti, *_: (ti[j], 0))
    hbm = pl.BlockSpec(memory_space=pl.ANY)
    return pl.pallas_call(
        functools.partial(_experts_kernel, layer=l),
        grid_spec=pltpu.PrefetchScalarGridSpec(
            num_scalar_prefetch=len(sched), grid=(sched[-1][0],),
            in_specs=[rows, hbm, hbm, hbm, _DEP_SPEC],
            out_specs=rows,
            scratch_shapes=[pltpu.VMEM((2, d, f), F32), pltpu.VMEM((2, d, f), F32),
                            pltpu.VMEM((2, f, d), F32),
                            pltpu.VMEM((d, f), BF16), pltpu.VMEM((d, f), BF16),
                            pltpu.VMEM((f, d), BF16), pltpu.SemaphoreType.DMA((2, 3))]),
        out_shape=jax.ShapeDtypeStruct((n_slots, w), I32),
        compiler_params=pltpu.CompilerParams(
            dimension_semantics=("arbitrary",), vmem_limit_bytes=VMEM_LIMIT),
        name="moe_experts",
    )(*sched, xs, w_gate, w_up, w_down, dep)


def _final_kernel(yg_ref, gate_ref, x2_ref, ln_ref, wsgu_ref, wsd_ref, *rest, alpha):
    out_ref = rest[-1]
    x2 = x2_ref[0]
    sh = _dot(x2.astype(BF16), wsgu_ref[0])
    f = sh.shape[1] // 2
    hid = _silu(sh[:, :f]) * sh[:, f:]
    acc = alpha * x2 + _dot(hid.astype(BF16), wsd_ref[0])
    gates = gate_ref[0].T
    for k in range(yg_ref.shape[0]):
        acc = acc + gates[:, k:k + 1] * _unpack_rows(yg_ref[k, 0])
    out_ref[0] = _layer_norm(acc, ln_ref[0, 0:1], ln_ref[0, 1:2])


def _final(l, yg, gate8, x2, ln3, wsgu, wsd, tm, alpha, out_rows, out_b0, out_prev, dep):
    bn, s_len, d = x2.shape
    k = yg.shape[0]
    lay = lambda a: pl.BlockSpec((1,) + a.shape[1:], lambda b, i: (l, 0, 0),
                                 pipeline_mode=pl.Buffered(1))
    in_specs = [pl.BlockSpec((k, 1, tm, d // 2), lambda b, i: (0, b, i, 0)),
                pl.BlockSpec((1, k, tm), lambda b, i: (b, 0, i)),
                pl.BlockSpec((1, tm, d), lambda b, i: (b, i, 0)),
                lay(ln3), lay(wsgu), lay(wsd), _DEP_SPEC]
    args = [yg, gate8, x2, ln3, wsgu, wsd, dep]
    aliases = {}
    if out_prev is not None:
        in_specs.append(pl.BlockSpec(memory_space=pl.ANY))
        args.append(out_prev)
        aliases = {len(args) - 1: 0}
    return pl.pallas_call(
        functools.partial(_final_kernel, alpha=alpha),
        grid=(bn, s_len // tm),
        in_specs=in_specs,
        out_specs=pl.BlockSpec((1, tm, d), lambda b, i: (b + out_b0, i, 0)),
        out_shape=jax.ShapeDtypeStruct((out_rows, s_len, d), F32),
        input_output_aliases=aliases,
        compiler_params=pltpu.CompilerParams(
            dimension_semantics=("parallel", "parallel"), vmem_limit_bytes=VMEM_LIMIT),
        name="moe_final",
    )(*args)


def _moe_schedule(eid, rank, counts):
    bn, _, s_len = eid.shape
    t = bn * s_len
    n_tiles = (t * TOP_K) // EXPERT_TILE + N_EXPERTS

    cnt = counts[:, 0].astype(I32)
    padded = (cnt + EXPERT_TILE - 1) // EXPERT_TILE * EXPERT_TILE
    ends = jnp.cumsum(padded)
    offs = ends - padded
    n_used = ends[-1] // EXPERT_TILE
    ti = jnp.minimum(jnp.arange(n_tiles, dtype=I32), n_used - 1)
    te = jnp.sum((ends[None, :] <= (ti * EXPERT_TILE)[:, None]).astype(I32), axis=1)
    nv = jnp.take(cnt + offs, te) - ti * EXPERT_TILE
    nonempty = cnt > 0
    idx = jnp.arange(N_EXPERTS, dtype=I32)
    order = jnp.cumsum(nonempty.astype(I32)) - 1
    later = jnp.where(nonempty[None, :] & (idx[None, :] > idx[:, None]), idx[None, :], N_EXPERTS)
    succ = jnp.min(later, axis=1)
    succ = jnp.where(succ == N_EXPERTS, -1, succ).astype(I32)
    eo = jnp.take(order, te) % 2
    ne = jnp.take(succ, te)

    slots = _slots(offs, eid, rank)
    slot_chunks = slots.reshape(bn, TOP_K, s_len // SC_CHUNK, SC_CHUNK).transpose(0, 2, 1, 3)
    slot_chunks = slot_chunks.reshape(t // SC_CHUNK, TOP_K, SC_CHUNK)
    return slot_chunks, (ti, te, nv, eo, ne, n_used.reshape(1)), n_tiles * EXPERT_TILE


def _head_chunks(w, widths, n_heads, per_head):
    lead = w.shape[:-1]
    wh = w.reshape(lead + (n_heads, per_head))
    parts = [sign * wh[..., a:b] for a, b, sign in widths]
    used = sum(b - a for a, b, _ in widths)
    parts.append(jnp.zeros(lead + (n_heads, LANES - used), w.dtype))
    return jnp.concatenate(parts, axis=-1).reshape(lead + (n_heads * LANES,))


def kernel(x, mem, positions, w_in, w_conv, w_conv_out, q_norm, w_uq, kv_norm, w_uk, w_uv,
           w_attn_out, w_mix_out, ln1_g, ln1_b, w_xq, w_xk, w_xv, w_xo, ln2_g, ln2_b,
           w_router, router_bias, w_gate, w_up, w_down, ws_gate, ws_up, ws_down, ln3_g, ln3_b):
    bn, s_len, d = x.shape
    depth = w_in.shape[0]
    alpha = (2 * depth) ** 0.25
    tq = min(ATTN_Q_TILE, s_len)
    ts = min(ROW_TILE, s_len)
    half = QK_ROPE_DIM // 2
    nope, rope = QK_NOPE_DIM, QK_ROPE_DIM

    inv_freq = ROPE_BASE ** (-jnp.arange(half, dtype=F32) / half)
    ang = positions.astype(F32)[..., None] * inv_freq
    cos, sin = jnp.cos(ang), jnp.sin(ang)
    scale = (nope + rope) ** -0.5
    tail = LANES - nope - rope
    tabs = _rope_tables(jnp.concatenate([cos, sin], axis=-1))

    sizes = (CONV_DIM, CONV_DIM, CONV_DIM, Q_RANK, KV_RANK, rope, d, d)
    offs = [0]
    for sz in sizes:
        offs.append(offs[-1] + sz)
    zpad = lambda w: jnp.zeros(w_in.shape[:2] + (w,), w_in.dtype)
    wx = w_in[:, :, :offs[5]].astype(BF16)
    wkr = jnp.concatenate([zpad(nope), w_in[:, :, offs[5]:offs[6]], zpad(tail)], axis=-1).astype(BF16)
    wg = w_in[:, :, offs[6]:].astype(BF16)

    per_q = nope + rope
    wq = _head_chunks(w_uq, [(0, per_q, scale)], MLA_HEADS, per_q).astype(BF16)
    wk = _head_chunks(w_uk, [(0, nope, 1.0)], MLA_HEADS, nope)
    wkv = jnp.concatenate([wk, w_uv], axis=-1).astype(BF16)

    wconv = w_conv[:, :, 0, :]
    wco = w_conv_out.astype(BF16)
    qn = q_norm[:, None, :]
    kvn = kv_norm[:, None, :]
    wao = w_attn_out.astype(BF16)
    wmo = w_mix_out.astype(BF16)
    ln1 = jnp.stack([ln1_g, ln1_b], axis=1)
    ln2 = jnp.stack([ln2_g, ln2_b], axis=1)
    ln3 = jnp.stack([ln3_g, ln3_b], axis=1)
    wxq = w_xq.astype(BF16)
    wxo = w_xo.astype(BF16)
    wr = jnp.swapaxes(w_router, 1, 2)
    rb = router_bias[:, :, None]
    wsgu = jnp.concatenate([ws_gate, ws_up], axis=-1).astype(BF16)
    wsd = ws_down.astype(BF16)

    kmem, vmem = _mem_kv(mem, w_xk.astype(BF16), w_xv.astype(BF16))

    assert bn % 2 == 0
    hb = bn // 2
    t = hb * s_len
    tm = min(FINAL_TILE, s_len)

    def mix(l, c, xc, dep):
        x_b0 = c * hb if l == 0 else 0
        return _mixer_in(l, hb, x_b0, c * hb, xc, tabs, wx, wkr, wg, wconv, wco, qn, wq, kvn, wkv,
                         ts, dep)

    def attn(mx, dep):
        yag, sgb, q, k, v = mx
        return yag, sgb, _attention(q, k, v, tq, dep)

    def post_route(l, c, xc, ma, dep):
        x_b0 = c * hb if l == 0 else 0
        yag, sgb, o = ma
        x2p, x2, logits = _post(l, hb, x_b0, c * hb, xc, yag, sgb, o, kmem, vmem, wao, wmo, ln1,
                                wxq, wxo, ln2, wr, ts, alpha, dep)
        eid, rank, gate8, counts = _route_call(l, logits, rb)
        slot_chunks, sched, n_slots = _moe_schedule(eid, rank, counts)
        return dict(x2p=x2p.reshape(t, d // 2), x2=x2, gate8=gate8, slot_chunks=slot_chunks,
                    sched=sched, n_slots=n_slots)

    def dispatch(st, dep):
        st["xs"] = _sc_dispatch(st["x2p"], st["slot_chunks"], st["n_slots"], dep)

    def experts(l, st, dep):
        st["ys"] = _experts(l, st["xs"], st["sched"], w_gate, w_up, w_down, dep)
        st["yg"] = _sc_combine(st["ys"], st["slot_chunks"], t).reshape(TOP_K, hb, s_len, d // 2)

    def final(l, c, st, dep, out_prev):
        if l == depth - 1:
            return _final(l, st["yg"], st["gate8"], st["x2"], ln3, wsgu, wsd, tm, alpha, bn, c * hb,
                          out_prev, dep)
        return _final(l, st["yg"], st["gate8"], st["x2"], ln3, wsgu, wsd, tm, alpha, hb, 0, None, dep)

    xa = xb = x
    mx_a = mix(0, 0, xa, x)
    st_a = post_route(0, 0, xa, attn(mx_a, mx_a[0]), mx_a[0])
    dispatch(st_a, st_a["gate8"])
    for l in range(depth):
        more = l + 1 < depth
        mx_b = mix(l, 1, xb, st_a["gate8"])
        experts(l, st_a, mx_b[0])
        st_b = post_route(l, 1, xb, attn(mx_b, st_a["ys"]), st_a["ys"])
        xa = final(l, 0, st_a, st_b["gate8"], None)
        dispatch(st_b, xa if more else st_b["gate8"])
        if more:
            mx_a = mix(l + 1, 0, xa, xa)
        experts(l, st_b, mx_a[0] if more else xa)
        if more:
            st_a = post_route(l + 1, 0, xa, attn(mx_a, st_b["ys"]), st_b["ys"])
        xb = final(l, 1, st_b, st_a["gate8"] if more else st_b["ys"], xa if not more else None)
        if more:
            dispatch(st_a, xb)
    return xb
```

```python
import functools

import jax
import jax.numpy as jnp
from jax import lax
from jax.experimental import pallas as pl
from jax.experimental.pallas import tpu as pltpu
from jax.experimental.pallas import tpu_sc as plsc

CONV_DIM = 512
CONV_WIDTH = 3
MLA_HEADS = 8
QK_NOPE_DIM = 64
QK_ROPE_DIM = 32
V_HEAD_DIM = 64
Q_RANK = 384
KV_RANK = 256
ROPE_BASE = 10000.0
XATTN_HEADS = 4
N_EXPERTS = 64
TOP_K = 8
N_GROUPS = 8
TOPK_GROUPS = 4
GROUP_SIZE = N_EXPERTS // N_GROUPS
ROUTE_SCALE = 2.5
LN_EPS = 1e-5
RMS_EPS = 1e-6

LANES = 128
HALO_ROWS = 8
VMEM_LIMIT = 56 * 1024 * 1024
SC_CORES = 2
SC_SUBCORES = 16
SC_CHUNK = 64
ROW_TILE = 1024
ATTN_Q_TILE = 512
FINAL_TILE = 512
TABLE_TILE = 2048
EXPERT_TILE = 1024
EXPERT_ROWS = 256

BF16 = jnp.bfloat16
F32 = jnp.float32
I32 = jnp.int32


_DEP_SPEC = pl.BlockSpec(memory_space=pl.ANY)


def _dot(a, b):
    return jnp.dot(a, b, preferred_element_type=F32)


def _dot_nt(a, b):
    return lax.dot_general(a, b, (((1,), (1,)), ((), ())), preferred_element_type=F32)


def _layer_norm(x, g, b):
    mu = jnp.mean(x, axis=-1, keepdims=True)
    xc = x - mu
    var = jnp.mean(xc * xc, axis=-1, keepdims=True)
    return xc * lax.rsqrt(var + LN_EPS) * g + b


def _rms_norm(x, g):
    ms = jnp.mean(x * x, axis=-1, keepdims=True)
    return x * lax.rsqrt(ms + RMS_EPS) * g


def _silu(x):
    return x * jax.nn.sigmoid(x)


_HI_MASK = -65536


def _pack_rows(y):
    w = y.shape[1] // 2
    lo = lax.bitcast_convert_type(y[:, :w].astype(BF16).astype(F32), I32)
    hi = lax.bitcast_convert_type(y[:, w:].astype(BF16).astype(F32), I32)
    return lax.shift_right_logical(lo, 16) | (hi & _HI_MASK)


def _unpack_rows(p):
    lo = lax.bitcast_convert_type(lax.shift_left(p, 16), F32)
    hi = lax.bitcast_convert_type(p & _HI_MASK, F32)
    return jnp.concatenate([lo, hi], axis=-1)


_C_BCH = 0
_C_CQ = _C_BCH + 3 * CONV_DIM
_C_CKV = _C_CQ + Q_RANK


def _mixer_in_kernel(x_ref, xh_ref, tab_ref, wx_ref, wkr_ref, wg_ref, wconv_ref, wco_ref, qn_ref, wq_ref,
                     kvn_ref, wkv_ref, dep_ref,
                     yag_ref, sgb_ref, q_ref, k_ref, v_ref, *, d_model):
    i = pl.program_id(1)
    d = d_model
    ts = x_ref.shape[1]
    xb = x_ref[0].astype(BF16)

    bch = _dot(xb, wx_ref[:, _C_BCH:_C_BCH + 3 * CONV_DIM])
    b_gate = bch[:, :CONV_DIM]
    u = bch[:, CONV_DIM:2 * CONV_DIM] * bch[:, 2 * CONV_DIM:]
    xh = xh_ref[0].astype(BF16)
    hp = _dot(xh, wx_ref[:, _C_BCH + CONV_DIM:_C_BCH + 3 * CONV_DIM])
    uh = hp[:, :CONV_DIM] * hp[:, CONV_DIM:]
    uh = jnp.where(i == 0, 0.0, uh)
    row = lax.broadcasted_iota(jnp.int32, (ts, CONV_DIM), 0)
    u1 = jnp.where(row == 0, uh[HALO_ROWS - 1:HALO_ROWS], pltpu.roll(u, 1, 0))
    u2 = jnp.where(row == 0, uh[HALO_ROWS - 2:HALO_ROWS - 1],
                   jnp.where(row == 1, uh[HALO_ROWS - 1:HALO_ROWS], pltpu.roll(u, 2, 0)))
    wc = wconv_ref[0]
    z = wc[0:1] * u2 + wc[1:2] * u1 + wc[2:3] * u
    y_a = _dot((b_gate * z).astype(BF16), wco_ref[0])
    g_a = _dot(xb, wg_ref[:, :d])
    yag_ref[0] = (jax.nn.sigmoid(g_a) * y_a).astype(BF16)
    g_b = _dot(xb, wg_ref[:, d:])
    sgb_ref[0] = jax.nn.sigmoid(g_b).astype(BF16)

    tab = tab_ref[0]
    t_a, t_m, t_p = tab[:, 0:LANES], tab[:, LANES:2 * LANES], tab[:, 2 * LANES:3 * LANES]
    half = QK_ROPE_DIM // 2

    def rotate(c):
        return (c * t_a + pltpu.roll(c, LANES - half, 1) * t_m + pltpu.roll(c, half, 1) * t_p)

    c_q = _dot(xb, wx_ref[:, _C_CQ:_C_CQ + Q_RANK])
    cqn = _rms_norm(c_q, qn_ref[0]).astype(BF16)
    hw = MLA_HEADS * LANES
    q = _dot(cqn, wq_ref[0])
    for h in range(MLA_HEADS):
        sl = slice(h * LANES, (h + 1) * LANES)
        q_ref[0, :, sl] = rotate(q[:, sl]).astype(BF16)

    c_kv = _dot(xb, wx_ref[:, _C_CKV:_C_CKV + KV_RANK])
    ckvn = _rms_norm(c_kv, kvn_ref[0]).astype(BF16)
    kr_rot = rotate(_dot(xb, wkr_ref[...]))
    k_nope = _dot(ckvn, wkv_ref[0, :, :hw])
    for h in range(MLA_HEADS):
        sl = slice(h * LANES, (h + 1) * LANES)
        k_ref[0, :, sl] = (k_nope[:, sl] + kr_rot).astype(BF16)
    v_ref[0] = _dot(ckvn, wkv_ref[0, :, hw:]).astype(BF16)


def _mixer_in(l, bn, x_b0, tab_b0, x, tabs, wx, wkr, wg, wconv, wco, qn, wq, kvn, wkv, ts, dep):
    _, s_len, d = x.shape
    n_t = s_len // ts
    hw = MLA_HEADS * LANES
    vw = MLA_HEADS * V_HEAD_DIM
    tile = lambda w: pl.BlockSpec((1, ts, w), lambda b, i: (b, i, 0))
    once = pl.Buffered(1)
    lay = lambda a: pl.BlockSpec((1,) + a.shape[1:], lambda b, i: (l,) + (0,) * (a.ndim - 1),
                                 pipeline_mode=once)
    flat = lambda a: pl.BlockSpec((None,) + a.shape[1:], lambda b, i: (l, 0, 0), pipeline_mode=once)
    halo = pl.BlockSpec((1, HALO_ROWS, d),
                        lambda b, i: (b + x_b0, jnp.maximum(i * (ts // HALO_ROWS) - 1, 0), 0))
    return pl.pallas_call(
        functools.partial(_mixer_in_kernel, d_model=d),
        grid=(bn, n_t),
        in_specs=[pl.BlockSpec((1, ts, d), lambda b, i: (b + x_b0, i, 0)), halo,
                  pl.BlockSpec((1, ts, 3 * LANES), lambda b, i: (b + tab_b0, i, 0)),
                  flat(wx), flat(wkr), flat(wg),
                  lay(wconv), lay(wco), lay(qn), lay(wq), lay(kvn), lay(wkv), _DEP_SPEC],
        out_specs=[tile(d), tile(d), tile(hw), tile(hw), tile(vw)],
        out_shape=[jax.ShapeDtypeStruct((bn, s_len, d), BF16),
                   jax.ShapeDtypeStruct((bn, s_len, d), BF16),
                   jax.ShapeDtypeStruct((bn, s_len, hw), BF16),
                   jax.ShapeDtypeStruct((bn, s_len, hw), BF16),
                   jax.ShapeDtypeStruct((bn, s_len, vw), BF16)],
        compiler_params=pltpu.CompilerParams(
            dimension_semantics=("parallel", "arbitrary"), vmem_limit_bytes=VMEM_LIMIT),
        name="mixer_in",
    )(x, x, tabs, wx, wkr, wg, wconv, wco, qn, wq, kvn, wkv, dep)


def _rope_tables_kernel(cs_ref, place_ref, const_ref, tab_ref):
    c = cs_ref[0]
    hi = c.astype(BF16)
    r1 = c - hi.astype(F32)
    mid = r1.astype(BF16)
    lo = (r1 - mid.astype(F32)).astype(BF16)
    p = place_ref[...]
    tab_ref[0] = (_dot(hi, p) + _dot(mid, p)) + _dot(lo, p) + const_ref[...]


def _rope_tables(cs):
    bn, s_len, w = cs.shape
    half = w // 2
    nope = QK_NOPE_DIM
    src = jnp.arange(w)[:, None]
    dst = jnp.arange(3 * LANES)[None, :]
    lane, tab = dst % LANES, dst // LANES
    is_cos, f = src < half, src % half
    place = (jnp.where((tab == 0) & is_cos & ((lane == nope + f) | (lane == nope + half + f)), 1.0, 0.0)
             + jnp.where((tab == 1) & ~is_cos & (lane == nope + f), -1.0, 0.0)
             + jnp.where((tab == 2) & ~is_cos & (lane == nope + half + f), 1.0, 0.0)).astype(BF16)
    const = jnp.where((dst < nope), 1.0, 0.0).astype(F32)
    ts = min(TABLE_TILE, s_len)
    return pl.pallas_call(
        _rope_tables_kernel,
        grid=(bn, s_len // ts),
        in_specs=[pl.BlockSpec((1, ts, w), lambda b, i: (b, i, 0)),
                  pl.BlockSpec(place.shape, lambda b, i: (0, 0)),
                  pl.BlockSpec(const.shape, lambda b, i: (0, 0))],
        out_specs=pl.BlockSpec((1, ts, 3 * LANES), lambda b, i: (b, i, 0)),
        out_shape=jax.ShapeDtypeStruct((bn, s_len, 3 * LANES), F32),
        compiler_params=pltpu.CompilerParams(
            dimension_semantics=("parallel", "parallel"), vmem_limit_bytes=VMEM_LIMIT),
        name="rope_tables",
    )(cs, place, const)


def _attn_kernel(q_ref, k_ref, v_ref, dep_ref, o_ref, *, tq):
    s_len = q_ref.shape[1]
    n_q = s_len // tq
    lane = lax.broadcasted_iota(jnp.int32, (tq, 2 * V_HEAD_DIM), 1)
    causal = (lax.broadcasted_iota(jnp.int32, (tq, tq), 1)
              <= lax.broadcasted_iota(jnp.int32, (tq, tq), 0))
    for qi in range(n_q):
        past = qi * tq
        rows = slice(past, past + tq)
        outs = []
        for h in range(2):
            sl = slice(h * LANES, (h + 1) * LANES)
            qh = q_ref[0, rows, sl]
            s_d = jnp.where(causal, _dot_nt(qh, k_ref[0, rows, sl]), -jnp.inf)
            m = jnp.max(s_d, axis=-1, keepdims=True)
            if past:
                s_p = _dot_nt(qh, k_ref[0, :past, sl])
                m = jnp.maximum(m, jnp.max(s_p, axis=-1, keepdims=True))
            p_d = jnp.exp(s_d - m)
            den = jnp.sum(p_d, axis=-1, keepdims=True)
            acc = _dot(p_d.astype(BF16), v_ref[0, rows, :])
            if past:
                p_p = jnp.exp(s_p - m)
                den = den + jnp.sum(p_p, axis=-1, keepdims=True)
                acc = acc + _dot(p_p.astype(BF16), v_ref[0, :past, :])
            outs.append(acc / den)
        o_ref[0, rows, :] = jnp.where(lane < V_HEAD_DIM, outs[0], outs[1]).astype(BF16)


def _attention(q, k, v, tq, dep):
    bn, s_len, _ = q.shape
    n_pairs = MLA_HEADS // 2
    return pl.pallas_call(
        functools.partial(_attn_kernel, tq=tq),
        grid=(bn, n_pairs),
        in_specs=[pl.BlockSpec((1, s_len, 2 * LANES), lambda b, h: (b, 0, h)),
                  pl.BlockSpec((1, s_len, 2 * LANES), lambda b, h: (b, 0, h)),
                  pl.BlockSpec((1, s_len, 2 * V_HEAD_DIM), lambda b, h: (b, 0, h)), _DEP_SPEC],
        out_specs=pl.BlockSpec((1, s_len, 2 * V_HEAD_DIM), lambda b, h: (b, 0, h)),
        out_shape=jax.ShapeDtypeStruct((bn, s_len, MLA_HEADS * V_HEAD_DIM), BF16),
        compiler_params=pltpu.CompilerParams(
            dimension_semantics=("parallel", "parallel"), vmem_limit_bytes=VMEM_LIMIT),
        name="mla_attention",
    )(q, k, v, dep)


def _mem_kv_kernel(mem_ref, wk_ref, wv_ref, k_ref, v_ref):
    mb = mem_ref[0].astype(BF16)
    k_ref[0, 0] = _dot(mb, wk_ref[0]).astype(BF16)
    v_ref[0, 0] = _dot(mb, wv_ref[0]).astype(BF16)


def _mem_kv(mem, w_xk, w_xv):
    bn, m_len, d = mem.shape
    n_l = w_xk.shape[0]
    wspec = pl.BlockSpec((1, d, d), lambda l, b: (l, 0, 0))
    ospec = pl.BlockSpec((1, 1, m_len, d), lambda l, b: (l, b, 0, 0))
    return pl.pallas_call(
        _mem_kv_kernel,
        grid=(n_l, bn),
        in_specs=[pl.BlockSpec((1, m_len, d), lambda l, b: (b, 0, 0)), wspec, wspec],
        out_specs=[ospec, ospec],
        out_shape=[jax.ShapeDtypeStruct((n_l, bn, m_len, d), BF16)] * 2,
        compiler_params=pltpu.CompilerParams(
            dimension_semantics=("arbitrary", "arbitrary"), vmem_limit_bytes=VMEM_LIMIT),
        name="mem_kv",
    )(mem, w_xk, w_xv)


def _first_argmax(vals, idx):
    m = jnp.max(functools.reduce(jnp.maximum, vals), axis=0, keepdims=True)
    big = jnp.int32(1 << 20)
    cand = functools.reduce(jnp.minimum, [jnp.where(v == m, ix, big) for v, ix in zip(vals, idx)])
    first = jnp.min(cand, axis=0, keepdims=True)
    return [ix == first for ix in idx], m, first


def _route(logits_t, bias):
    n = logits_t.shape[1]
    scores = jax.nn.sigmoid(logits_t)
    sel = scores + bias
    sub = lax.broadcasted_iota(jnp.int32, (GROUP_SIZE, n), 0)
    neg = -jnp.inf
    sel_g = [sel[g * GROUP_SIZE:(g + 1) * GROUP_SIZE] for g in range(N_GROUPS)]
    gs = []
    for g in range(N_GROUPS):
        (hot,), m1, _ = _first_argmax([sel_g[g]], [sub])
        m2 = jnp.max(jnp.where(hot, neg, sel_g[g]), axis=0, keepdims=True)
        gs.append(m1 + m2)
    gsv = jnp.concatenate(gs, axis=0)
    gmask = jnp.zeros(gsv.shape, F32)
    for _ in range(TOPK_GROUPS):
        (hot,), _m, _ = _first_argmax([gsv], [sub])
        gmask = jnp.where(hot, 1.0, gmask)
        gsv = jnp.where(hot, neg, gsv)
    msel = [jnp.where(gmask[g:g + 1] > 0.0, sel_g[g], neg) for g in range(N_GROUPS)]
    eidx = [sub + g * GROUP_SIZE for g in range(N_GROUPS)]
    picks, ids = [], []
    for _ in range(TOP_K):
        hots, _m, first = _first_argmax(msel, eidx)
        picks.append(hots)
        ids.append(first)
        msel = [jnp.where(h, neg, v) for h, v in zip(hots, msel)]
    chosen = [functools.reduce(jnp.logical_or, [p[g] for p in picks]) for g in range(N_GROUPS)]
    sc_g = [scores[g * GROUP_SIZE:(g + 1) * GROUP_SIZE] for g in range(N_GROUPS)]
    picked = [jnp.where(c, s, 0.0) for c, s in zip(chosen, sc_g)]
    wsum = jnp.sum(functools.reduce(lambda a, b: a + b, picked), axis=0, keepdims=True)
    gates = [p / wsum * ROUTE_SCALE for p in picked]
    return picks, ids, chosen, gates


def _pick_rows(hots, vals):
    acc = functools.reduce(lambda a, b: a + b, [jnp.where(h, v, 0.0) for h, v in zip(hots, vals)])
    return jnp.sum(acc, axis=0, keepdims=True)


def _post_kernel(x_ref, yag_ref, sgb_ref, o_ref, km_ref, vm_ref, wao_ref, wmo_ref, ln1_ref,
                 wxq_ref, wxo_ref, ln2_ref, wr_ref, dep_ref,
                 x2p_ref, x2_ref, logit_ref, *, alpha):
    d = x_ref.shape[2]
    x = x_ref[0]
    y_b = _dot(o_ref[0], wao_ref[0])
    y = yag_ref[0].astype(F32) + sgb_ref[0].astype(F32) * y_b
    mix = _dot(y.astype(BF16), wmo_ref[0])
    x1 = _layer_norm(alpha * x + mix, ln1_ref[0, 0:1], ln1_ref[0, 1:2])

    hd = d // XATTN_HEADS
    xq = (_dot(x1.astype(BF16), wxq_ref[0]) * (hd ** -0.5)).astype(BF16)
    heads = []
    for h in range(XATTN_HEADS):
        sl = slice(h * hd, (h + 1) * hd)
        s = _dot_nt(xq[:, sl], km_ref[0, 0, :, sl])
        m = jnp.max(s, axis=-1, keepdims=True)
        p = jnp.exp(s - m)
        den = jnp.sum(p, axis=-1, keepdims=True)
        heads.append((_dot(p.astype(BF16), vm_ref[0, 0, :, sl]) / den).astype(BF16))
    xat = _dot(jnp.concatenate(heads, axis=-1), wxo_ref[0])
    x2 = _layer_norm(alpha * x1 + xat, ln2_ref[0, 0:1], ln2_ref[0, 1:2])

    x2b = x2.astype(BF16)
    x2l = (x2 - x2b.astype(F32)).astype(BF16)
    wr = wr_ref[0]
    wrh = wr.astype(BF16)
    wrl = (wr - wrh.astype(F32)).astype(BF16)
    logit_ref[0] = _dot_nt(wrh, x2b) + (_dot_nt(wrh, x2l) + _dot_nt(wrl, x2b))

    x2p_ref[0] = _pack_rows(x2)
    x2_ref[0] = x2


def _post(l, bn, x_b0, mem_b0, x, yag, sgb, o, kmem, vmem, wao, wmo, ln1, wxq, wxo, ln2, wr,
          ts, alpha, dep):
    _, s_len, d = x.shape
    m_len = kmem.shape[2]
    tile = lambda w: pl.BlockSpec((1, ts, w), lambda b, i: (b, i, 0))
    lay = lambda a: pl.BlockSpec((1,) + a.shape[1:], lambda b, i: (l,) + (0,) * (a.ndim - 1),
                                 pipeline_mode=pl.Buffered(1))
    memspec = pl.BlockSpec((1, 1, m_len, d), lambda b, i: (l, b + mem_b0, 0, 0))
    return pl.pallas_call(
        functools.partial(_post_kernel, alpha=alpha),
        grid=(bn, s_len // ts),
        in_specs=[pl.BlockSpec((1, ts, d), lambda b, i: (b + x_b0, i, 0)),
                  tile(d), tile(d), tile(o.shape[2]), memspec, memspec,
                  lay(wao), lay(wmo), lay(ln1), lay(wxq), lay(wxo), lay(ln2), lay(wr), _DEP_SPEC],
        out_specs=[tile(d // 2), tile(d),
                   pl.BlockSpec((1, N_EXPERTS, ts), lambda b, i: (b, 0, i))],
        out_shape=[jax.ShapeDtypeStruct((bn, s_len, d // 2), I32),
                   jax.ShapeDtypeStruct((bn, s_len, d), F32),
                   jax.ShapeDtypeStruct((bn, N_EXPERTS, s_len), F32)],
        compiler_params=pltpu.CompilerParams(
            dimension_semantics=("parallel", "arbitrary"), vmem_limit_bytes=VMEM_LIMIT),
        name="post_mixer",
    )(x, yag, sgb, o, kmem, vmem, wao, wmo, ln1, wxq, wxo, ln2, wr, dep)


RANK_BLOCK = 512


def _route_kernel(lg_ref, rb_ref, eid_ref, rank_ref, gate_ref, cnt_ref):
    @pl.when(pl.program_id(0) == 0)
    def _():
        cnt_ref[...] = jnp.zeros_like(cnt_ref)

    n = lg_ref.shape[2]
    blk = min(RANK_BLOCK, n)
    picks, ids, chosen, gates = _route(lg_ref[0], rb_ref[0])

    sel_t = jnp.concatenate([c.astype(F32) for c in chosen], axis=0)
    before = (lax.broadcasted_iota(jnp.int32, (blk, blk), 0)
              < lax.broadcasted_iota(jnp.int32, (blk, blk), 1)).astype(BF16)
    running = cnt_ref[:, 0:1]
    ranks = []
    for j in range(n // blk):
        sb = sel_t[:, j * blk:(j + 1) * blk]
        ranks.append(_dot(sb.astype(BF16), before) + running)
        running = running + jnp.sum(sb, axis=1, keepdims=True)
    cnt_ref[...] = jnp.broadcast_to(running, cnt_ref.shape)
    rank_all = jnp.concatenate(ranks, axis=1)
    rank_g = [rank_all[g * GROUP_SIZE:(g + 1) * GROUP_SIZE] for g in range(N_GROUPS)]
    eid_ref[0] = jnp.concatenate(ids, axis=0)
    rank_ref[0] = jnp.concatenate([_pick_rows(h, rank_g) for h in picks], axis=0).astype(I32)
    gate_ref[0] = jnp.concatenate([_pick_rows(h, gates) for h in picks], axis=0)


def _route_call(l, logits, rb):
    bn, n_e, s_len = logits.shape
    pick = pl.BlockSpec((1, TOP_K, s_len), lambda b: (b, 0, 0))
    return pl.pallas_call(
        _route_kernel,
        grid=(bn,),
        in_specs=[pl.BlockSpec((1, n_e, s_len), lambda b: (b, 0, 0)),
                  pl.BlockSpec((1,) + rb.shape[1:], lambda b: (l, 0, 0))],
        out_specs=[pick, pick, pick, pl.BlockSpec((n_e, LANES), lambda b: (0, 0))],
        out_shape=[jax.ShapeDtypeStruct((bn, TOP_K, s_len), I32),
                   jax.ShapeDtypeStruct((bn, TOP_K, s_len), I32),
                   jax.ShapeDtypeStruct((bn, TOP_K, s_len), F32),
                   jax.ShapeDtypeStruct((n_e, LANES), F32)],
        compiler_params=pltpu.CompilerParams(
            dimension_semantics=("arbitrary",), vmem_limit_bytes=VMEM_LIMIT),
        name="moe_route",
    )(logits, rb)


def _slot_kernel(offs_ref, eid_ref, rank_ref, slot_ref):
    eid = eid_ref[0]
    acc = rank_ref[0]
    for e in range(N_EXPERTS):
        acc = acc + jnp.where(eid == e, offs_ref[e], 0)
    slot_ref[0] = acc


def _slots(offs, eid, rank):
    bn, k, s_len = eid.shape
    spec = pl.BlockSpec((1, k, s_len), lambda b, offs_ref: (b, 0, 0))
    return pl.pallas_call(
        _slot_kernel,
        grid_spec=pltpu.PrefetchScalarGridSpec(
            num_scalar_prefetch=1, grid=(bn,), in_specs=[spec, spec], out_specs=spec),
        out_shape=jax.ShapeDtypeStruct((bn, k, s_len), I32),
        name="moe_slots",
    )(offs, eid, rank)


def _sc_mesh():
    return plsc.VectorSubcoreMesh(core_axis_name="c", subcore_axis_name="s")


def _sc_dispatch(xp, slot_chunks, n_slots, dep):
    t, w = xp.shape
    n_chunks, k, ch = slot_chunks.shape
    per_worker = n_chunks // (SC_CORES * SC_SUBCORES)

    @functools.partial(
        pl.kernel, mesh=_sc_mesh(),
        out_type=jax.ShapeDtypeStruct((n_slots, w), I32),
        scratch_types=[pltpu.VMEM((k, ch), I32), pltpu.VMEM((ch, w), I32), pltpu.SemaphoreType.DMA],
        name="moe_dispatch")
    def body(x_hbm, slot_hbm, dep_hbm, out_hbm, idx_v, rows_v, sem):
        worker = lax.axis_index("s") * SC_CORES + lax.axis_index("c")

        @pl.loop(0, per_worker)
        def _(j):
            c = worker * per_worker + j
            pltpu.sync_copy(slot_hbm.at[c], idx_v)
            pltpu.sync_copy(x_hbm.at[pl.ds(c * ch, ch)], rows_v)
            copies = [pltpu.async_copy(rows_v, out_hbm.at[idx_v.at[kk]], sem) for kk in range(k)]
            for cp in copies:
                cp.wait()

    return body(xp, slot_chunks, dep)


def _sc_combine(ys, slot_chunks, t):
    w = ys.shape[1]
    n_chunks, k, ch = slot_chunks.shape
    per_worker = n_chunks // (SC_CORES * SC_SUBCORES)

    @functools.partial(
        pl.kernel, mesh=_sc_mesh(),
        out_type=jax.ShapeDtypeStruct((k, t, w), I32),
        scratch_types=[pltpu.VMEM((k, ch), I32), pltpu.VMEM((ch, w), I32), pltpu.VMEM((ch, w), I32),
                       pltpu.SemaphoreType.DMA, pltpu.SemaphoreType.DMA,
                       pltpu.SemaphoreType.DMA, pltpu.SemaphoreType.DMA],
        name="moe_combine")
    def body(y_hbm, slot_hbm, out_hbm, idx_v, rows0, rows1, g0, g1, w0, w1):
        worker = lax.axis_index("s") * SC_CORES + lax.axis_index("c")
        bufs, gsem, wsem = (rows0, rows1), (g0, g1), (w0, w1)

        @pl.loop(0, per_worker)
        def _(j):
            c = worker * per_worker + j
            pltpu.sync_copy(slot_hbm.at[c], idx_v)
            gathers = [None] * k
            writes = [None] * k
            gathers[0] = pltpu.async_copy(y_hbm.at[idx_v.at[0]], bufs[0], gsem[0])
            for kk in range(k):
                cur = kk % 2
                if kk + 1 < k:
                    if kk >= 1:
                        writes[kk - 1].wait()
                    gathers[kk + 1] = pltpu.async_copy(
                        y_hbm.at[idx_v.at[kk + 1]], bufs[1 - cur], gsem[1 - cur])
                gathers[kk].wait()
                writes[kk] = pltpu.async_copy(
                    bufs[cur], out_hbm.at[kk, pl.ds(c * ch, ch)], wsem[cur])
            writes[k - 2].wait()
            writes[k - 1].wait()

    return body(ys, slot_chunks)


def _experts_kernel(ti_ref, te_ref, nv_ref, eo_ref, ne_ref, nu_ref, xs_ref, wg_hbm, wu_hbm, wd_hbm,
                    dep_ref, ys_ref, wg_f, wu_f, wd_f, wg_b, wu_b, wd_b, sem, *, layer):
    j = pl.program_id(0)
    expert = te_ref[j]
    slot = eo_ref[j]
    prev = te_ref[jnp.maximum(j - 1, 0)]
    new_expert = jnp.logical_or(j == 0, expert != prev)

    def weight_copies(e, s):
        return [pltpu.make_async_copy(hbm.at[layer, e], buf.at[s], sem.at[s, i])
                for i, (hbm, buf) in enumerate(((wg_hbm, wg_f), (wu_hbm, wu_f), (wd_hbm, wd_f)))]

    @pl.when(j == 0)
    def _():
        for cp in weight_copies(expert, slot):
            cp.start()

    @pl.when(new_expert)
    def _():
        nxt = ne_ref[j]

        @pl.when(nxt >= 0)
        def _():
            for cp in weight_copies(nxt, 1 - slot):
                cp.start()

        for cp in weight_copies(expert, slot):
            cp.wait()
        wg_b[...] = wg_f[slot].astype(BF16)
        wu_b[...] = wu_f[slot].astype(BF16)
        wd_b[...] = wd_f[slot].astype(BF16)

    def ffn(n_rows):
        rows = pl.ds(0, n_rows)
        xb = _unpack_rows(xs_ref[rows, :]).astype(BF16)
        hid = _silu(_dot(xb, wg_b[...])) * _dot(xb, wu_b[...])
        ys_ref[rows, :] = _pack_rows(_dot(hid.astype(BF16), wd_b[...]))

    used = j < nu_ref[0]
    n_paths = xs_ref.shape[0] // EXPERT_ROWS
    groups = jnp.minimum((nv_ref[j] + EXPERT_ROWS - 1) // EXPERT_ROWS, n_paths)
    for g in range(1, n_paths + 1):
        @pl.when(jnp.logical_and(used, groups == g))
        def _(g=g):
            ffn(g * EXPERT_ROWS)


def _experts(l, xs, sched, w_gate, w_up, w_down, dep):
    n_slots, w = xs.shape
    n_tiles = n_slots // EXPERT_TILE
    d, f = w_gate.shape[2:]
    rows = pl.BlockSpec((EXPERT_TILE, w), lambda j, ti, *_: (ti[j], 0))
    hbm = pl.BlockSpec(memory_space=pl.ANY)
    return pl.pallas_call(
        functools.partial(_experts_kernel, layer=l),
        grid_spec=pltpu.PrefetchScalarGridSpec(
            num_scalar_prefetch=len(sched), grid=(sched[-1][0],),
            in_specs=[rows, hbm, hbm, hbm, _DEP_SPEC],
            out_specs=rows,
            scratch_shapes=[pltpu.VMEM((2, d, f), F32), pltpu.VMEM((2, d, f), F32),
                            pltpu.VMEM((2, f, d), F32),
                            pltpu.VMEM((d, f), BF16), pltpu.VMEM((d, f), BF16),
                            pltpu.VMEM((f, d), BF16), pltpu.SemaphoreType.DMA((2, 3))]),
        out_shape=jax.ShapeDtypeStruct((n_slots, w), I32),
        compiler_params=pltpu.CompilerParams(
            dimension_semantics=("arbitrary",), vmem_limit_bytes=VMEM_LIMIT),
        name="moe_experts",
    )(*sched, xs, w_gate, w_up, w_down, dep)


def _final_kernel(yg_ref, gate_ref, x2_ref, ln_ref, wsgu_ref, wsd_ref, *rest, alpha):
    out_ref = rest[-1]
    x2 = x2_ref[0]
    sh = _dot(x2.astype(BF16), wsgu_ref[0])
    f = sh.shape[1] // 2
    hid = _silu(sh[:, :f]) * sh[:, f:]
    acc = alpha * x2 + _dot(hid.astype(BF16), wsd_ref[0])
    gates = gate_ref[0].T
    for k in range(yg_ref.shape[0]):
        acc = acc + gates[:, k:k + 1] * _unpack_rows(yg_ref[k, 0])
    out_ref[0] = _layer_norm(acc, ln_ref[0, 0:1], ln_ref[0, 1:2])


def _final(l, yg, gate8, x2, ln3, wsgu, wsd, tm, alpha, out_rows, out_b0, out_prev, dep):
    bn, s_len, d = x2.shape
    k = yg.shape[0]
    lay = lambda a: pl.BlockSpec((1,) + a.shape[1:], lambda b, i: (l, 0, 0),
                                 pipeline_mode=pl.Buffered(1))
    in_specs = [pl.BlockSpec((k, 1, tm, d // 2), lambda b, i: (0, b, i, 0)),
                pl.BlockSpec((1, k, tm), lambda b, i: (b, 0, i)),
                pl.BlockSpec((1, tm, d), lambda b, i: (b, i, 0)),
                lay(ln3), lay(wsgu), lay(wsd), _DEP_SPEC]
    args = [yg, gate8, x2, ln3, wsgu, wsd, dep]
    aliases = {}
    if out_prev is not None:
        in_specs.append(pl.BlockSpec(memory_space=pl.ANY))
        args.append(out_prev)
        aliases = {len(args) - 1: 0}
    return pl.pallas_call(
        functools.partial(_final_kernel, alpha=alpha),
        grid=(bn, s_len // tm),
        in_specs=in_specs,
        out_specs=pl.BlockSpec((1, tm, d), lambda b, i: (b + out_b0, i, 0)),
        out_shape=jax.ShapeDtypeStruct((out_rows, s_len, d), F32),
        input_output_aliases=aliases,
        compiler_params=pltpu.CompilerParams(
            dimension_semantics=("parallel", "parallel"), vmem_limit_bytes=VMEM_LIMIT),
        name="moe_final",
    )(*args)


def _moe_schedule(eid, rank, counts):
    bn, _, s_len = eid.shape
    t = bn * s_len
    n_tiles = (t * TOP_K) // EXPERT_TILE + N_EXPERTS

    cnt = counts[:, 0].astype(I32)
    padded = (cnt + EXPERT_TILE - 1) // EXPERT_TILE * EXPERT_TILE
    ends = jnp.cumsum(padded)
    offs = ends - padded
    n_used = ends[-1] // EXPERT_TILE
    ti = jnp.minimum(jnp.arange(n_tiles, dtype=I32), n_used - 1)
    te = jnp.sum((ends[None, :] <= (ti * EXPERT_TILE)[:, None]).astype(I32), axis=1)
    nv = jnp.take(cnt + offs, te) - ti * EXPERT_TILE
    nonempty = cnt > 0
    idx = jnp.arange(N_EXPERTS, dtype=I32)
    order = jnp.cumsum(nonempty.astype(I32)) - 1
    later = jnp.where(nonempty[None, :] & (idx[None, :] > idx[:, None]), idx[None, :], N_EXPERTS)
    succ = jnp.min(later, axis=1)
    succ = jnp.where(succ == N_EXPERTS, -1, succ).astype(I32)
    eo = jnp.take(order, te) % 2
    ne = jnp.take(succ, te)

    slots = _slots(offs, eid, rank)
    slot_chunks = slots.reshape(bn, TOP_K, s_len // SC_CHUNK, SC_CHUNK).transpose(0, 2, 1, 3)
    slot_chunks = slot_chunks.reshape(t // SC_CHUNK, TOP_K, SC_CHUNK)
    return slot_chunks, (ti, te, nv, eo, ne, n_used.reshape(1)), n_tiles * EXPERT_TILE


def _head_chunks(w, widths, n_heads, per_head):
    lead = w.shape[:-1]
    wh = w.reshape(lead + (n_heads, per_head))
    parts = [sign * wh[..., a:b] for a, b, sign in widths]
    used = sum(b - a for a, b, _ in widths)
    parts.append(jnp.zeros(lead + (n_heads, LANES - used), w.dtype))
    return jnp.concatenate(parts, axis=-1).reshape(lead + (n_heads * LANES,))


def kernel(x, mem, positions, w_in, w_conv, w_conv_out, q_norm, w_uq, kv_norm, w_uk, w_uv,
           w_attn_out, w_mix_out, ln1_g, ln1_b, w_xq, w_xk, w_xv, w_xo, ln2_g, ln2_b,
           w_router, router_bias, w_gate, w_up, w_down, ws_gate, ws_up, ws_down, ln3_g, ln3_b):
    bn, s_len, d = x.shape
    depth = w_in.shape[0]
    alpha = (2 * depth) ** 0.25
    tq = min(ATTN_Q_TILE, s_len)
    ts = min(ROW_TILE, s_len)
    half = QK_ROPE_DIM // 2
    nope, rope = QK_NOPE_DIM, QK_ROPE_DIM

    inv_freq = ROPE_BASE ** (-jnp.arange(half, dtype=F32) / half)
    ang = positions.astype(F32)[..., None] * inv_freq
    cos, sin = jnp.cos(ang), jnp.sin(ang)
    scale = (nope + rope) ** -0.5
    tail = LANES - nope - rope
    tabs = _rope_tables(jnp.concatenate([cos, sin], axis=-1))

    sizes = (CONV_DIM, CONV_DIM, CONV_DIM, Q_RANK, KV_RANK, rope, d, d)
    offs = [0]
    for sz in sizes:
        offs.append(offs[-1] + sz)
    zpad = lambda w: jnp.zeros(w_in.shape[:2] + (w,), w_in.dtype)
    wx = w_in[:, :, :offs[5]].astype(BF16)
    wkr = jnp.concatenate([zpad(nope), w_in[:, :, offs[5]:offs[6]], zpad(tail)], axis=-1).astype(BF16)
    wg = w_in[:, :, offs[6]:].astype(BF16)

    per_q = nope + rope
    wq = _head_chunks(w_uq, [(0, per_q, scale)], MLA_HEADS, per_q).astype(BF16)
    wk = _head_chunks(w_uk, [(0, nope, 1.0)], MLA_HEADS, nope)
    wkv = jnp.concatenate([wk, w_uv], axis=-1).astype(BF16)

    wconv = w_conv[:, :, 0, :]
    wco = w_conv_out.astype(BF16)
    qn = q_norm[:, None, :]
    kvn = kv_norm[:, None, :]
    wao = w_attn_out.astype(BF16)
    wmo = w_mix_out.astype(BF16)
    ln1 = jnp.stack([ln1_g, ln1_b], axis=1)
    ln2 = jnp.stack([ln2_g, ln2_b], axis=1)
    ln3 = jnp.stack([ln3_g, ln3_b], axis=1)
    wxq = w_xq.astype(BF16)
    wxo = w_xo.astype(BF16)
    wr = jnp.swapaxes(w_router, 1, 2)
    rb = router_bias[:, :, None]
    wsgu = jnp.concatenate([ws_gate, ws_up], axis=-1).astype(BF16)
    wsd = ws_down.astype(BF16)

    kmem, vmem = _mem_kv(mem, w_xk.astype(BF16), w_xv.astype(BF16))

    assert bn % 2 == 0
    hb = bn // 2
    t = hb * s_len
    tm = min(FINAL_TILE, s_len)

    def mix(l, c, xc, dep):
        x_b0 = c * hb if l == 0 else 0
        return _mixer_in(l, hb, x_b0, c * hb, xc, tabs, wx, wkr, wg, wconv, wco, qn, wq, kvn, wkv,
                         ts, dep)

    def attn(mx, dep):
        yag, sgb, q, k, v = mx
        return yag, sgb, _attention(q, k, v, tq, dep)

    def post_route(l, c, xc, ma, dep):
        x_b0 = c * hb if l == 0 else 0
        yag, sgb, o = ma
        x2p, x2, logits = _post(l, hb, x_b0, c * hb, xc, yag, sgb, o, kmem, vmem, wao, wmo, ln1,
                                wxq, wxo, ln2, wr, ts, alpha, dep)
        eid, rank, gate8, counts = _route_call(l, logits, rb)
        slot_chunks, sched, n_slots = _moe_schedule(eid, rank, counts)
        return dict(x2p=x2p.reshape(t, d // 2), x2=x2, gate8=gate8, slot_chunks=slot_chunks,
                    sched=sched, n_slots=n_slots)

    def dispatch(st, dep):
        st["xs"] = _sc_dispatch(st["x2p"], st["slot_chunks"], st["n_slots"], dep)

    def experts(l, st, dep):
        st["ys"] = _experts(l, st["xs"], st["sched"], w_gate, w_up, w_down, dep)
        st["yg"] = _sc_combine(st["ys"], st["slot_chunks"], t).reshape(TOP_K, hb, s_len, d // 2)

    def final(l, c, st, dep, out_prev):
        if l == depth - 1:
            return _final(l, st["yg"], st["gate8"], st["x2"], ln3, wsgu, wsd, tm, alpha, bn, c * hb,
                          out_prev, dep)
        return _final(l, st["yg"], st["gate8"], st["x2"], ln3, wsgu, wsd, tm, alpha, hb, 0, None, dep)

    xa = xb = x
    mx_a = mix(0, 0, xa, x)
    st_a = post_route(0, 0, xa, attn(mx_a, mx_a[0]), mx_a[0])
    dispatch(st_a, st_a["gate8"])
    for l in range(depth):
        more = l + 1 < depth
        mx_b = mix(l, 1, xb, st_a["gate8"])
        experts(l, st_a, mx_b[0])
        st_b = post_route(l, 1, xb, attn(mx_b, st_a["ys"]), st_a["ys"])
        xa = final(l, 0, st_a, st_b["gate8"], None)
        dispatch(st_b, xa if more else st_b["gate8"])
        if more:
            mx_a = mix(l + 1, 0, xa, xa)
        experts(l, st_b, mx_a[0] if more else xa)
        if more:
            st_a = post_route(l + 1, 0, xa, attn(mx_a, st_b["ys"]), st_b["ys"])
        xb = final(l, 1, st_b, st_a["gate8"] if more else st_b["ys"], xa if not more else None)
        if more:
            dispatch(st_a, xb)
    return xb
```

```python
import functools

import jax
import jax.numpy as jnp
from jax import lax
from jax.experimental import pallas as pl
from jax.experimental.pallas import tpu as pltpu
from jax.experimental.pallas import tpu_sc as plsc

CONV_DIM = 512
CONV_WIDTH = 3
MLA_HEADS = 8
QK_NOPE_DIM = 64
QK_ROPE_DIM = 32
V_HEAD_DIM = 64
Q_RANK = 384
KV_RANK = 256
ROPE_BASE = 10000.0
XATTN_HEADS = 4
N_EXPERTS = 64
TOP_K = 8
N_GROUPS = 8
TOPK_GROUPS = 4
GROUP_SIZE = N_EXPERTS // N_GROUPS
ROUTE_SCALE = 2.5
LN_EPS = 1e-5
RMS_EPS = 1e-6

LANES = 128
HALO_ROWS = 8
VMEM_LIMIT = 56 * 1024 * 1024
SC_CORES = 2
SC_SUBCORES = 16
SC_CHUNK = 64
ROW_TILE = 1024
ATTN_Q_TILE = 512
FINAL_TILE = 512
TABLE_TILE = 2048
EXPERT_TILE = 1024
EXPERT_ROWS = 512

BF16 = jnp.bfloat16
F32 = jnp.float32
I32 = jnp.int32


_DEP_SPEC = pl.BlockSpec(memory_space=pl.ANY)


def _dot(a, b):
    return jnp.dot(a, b, preferred_element_type=F32)


def _dot_nt(a, b):
    return lax.dot_general(a, b, (((1,), (1,)), ((), ())), preferred_element_type=F32)


def _layer_norm(x, g, b):
    mu = jnp.mean(x, axis=-1, keepdims=True)
    xc = x - mu
    var = jnp.mean(xc * xc, axis=-1, keepdims=True)
    return xc * lax.rsqrt(var + LN_EPS) * g + b


def _rms_norm(x, g):
    ms = jnp.mean(x * x, axis=-1, keepdims=True)
    return x * lax.rsqrt(ms + RMS_EPS) * g


def _silu(x):
    return x * jax.nn.sigmoid(x)


_HI_MASK = -65536


def _pack_rows(y):
    w = y.shape[1] // 2
    lo = lax.bitcast_convert_type(y[:, :w].astype(BF16).astype(F32), I32)
    hi = lax.bitcast_convert_type(y[:, w:].astype(BF16).astype(F32), I32)
    return lax.shift_right_logical(lo, 16) | (hi & _HI_MASK)


def _unpack_rows(p):
    lo = lax.bitcast_convert_type(lax.shift_left(p, 16), F32)
    hi = lax.bitcast_convert_type(p & _HI_MASK, F32)
    return jnp.concatenate([lo, hi], axis=-1)


_C_BCH = 0
_C_CQ = _C_BCH + 3 * CONV_DIM
_C_CKV = _C_CQ + Q_RANK


def _mixer_in_kernel(x_ref, xh_ref, tab_ref, wx_ref, wkr_ref, wg_ref, wconv_ref, wco_ref, qn_ref, wq_ref,
                     kvn_ref, wkv_ref, dep_ref,
                     yag_ref, sgb_ref, q_ref, k_ref, v_ref, *, d_model):
    i = pl.program_id(1)
    d = d_model
    ts = x_ref.shape[1]
    xb = x_ref[0].astype(BF16)

    bch = _dot(xb, wx_ref[:, _C_BCH:_C_BCH + 3 * CONV_DIM])
    b_gate = bch[:, :CONV_DIM]
    u = bch[:, CONV_DIM:2 * CONV_DIM] * bch[:, 2 * CONV_DIM:]
    xh = xh_ref[0].astype(BF16)
    hp = _dot(xh, wx_ref[:, _C_BCH + CONV_DIM:_C_BCH + 3 * CONV_DIM])
    uh = hp[:, :CONV_DIM] * hp[:, CONV_DIM:]
    uh = jnp.where(i == 0, 0.0, uh)
    row = lax.broadcasted_iota(jnp.int32, (ts, CONV_DIM), 0)
    u1 = jnp.where(row == 0, uh[HALO_ROWS - 1:HALO_ROWS], pltpu.roll(u, 1, 0))
    u2 = jnp.where(row == 0, uh[HALO_ROWS - 2:HALO_ROWS - 1],
                   jnp.where(row == 1, uh[HALO_ROWS - 1:HALO_ROWS], pltpu.roll(u, 2, 0)))
    wc = wconv_ref[0]
    z = wc[0:1] * u2 + wc[1:2] * u1 + wc[2:3] * u
    y_a = _dot((b_gate * z).astype(BF16), wco_ref[0])
    g_a = _dot(xb, wg_ref[:, :d])
    yag_ref[0] = (jax.nn.sigmoid(g_a) * y_a).astype(BF16)
    g_b = _dot(xb, wg_ref[:, d:])
    sgb_ref[0] = jax.nn.sigmoid(g_b).astype(BF16)

    tab = tab_ref[0]
    t_a, t_m, t_p = tab[:, 0:LANES], tab[:, LANES:2 * LANES], tab[:, 2 * LANES:3 * LANES]
    half = QK_ROPE_DIM // 2

    def rotate(c):
        return (c * t_a + pltpu.roll(c, LANES - half, 1) * t_m + pltpu.roll(c, half, 1) * t_p)

    c_q = _dot(xb, wx_ref[:, _C_CQ:_C_CQ + Q_RANK])
    cqn = _rms_norm(c_q, qn_ref[0]).astype(BF16)
    hw = MLA_HEADS * LANES
    q = _dot(cqn, wq_ref[0])
    for h in range(MLA_HEADS):
        sl = slice(h * LANES, (h + 1) * LANES)
        q_ref[0, :, sl] = rotate(q[:, sl]).astype(BF16)

    c_kv = _dot(xb, wx_ref[:, _C_CKV:_C_CKV + KV_RANK])
    ckvn = _rms_norm(c_kv, kvn_ref[0]).astype(BF16)
    kr_rot = rotate(_dot(xb, wkr_ref[...]))
    k_nope = _dot(ckvn, wkv_ref[0, :, :hw])
    for h in range(MLA_HEADS):
        sl = slice(h * LANES, (h + 1) * LANES)
        k_ref[0, :, sl] = (k_nope[:, sl] + kr_rot).astype(BF16)
    v_ref[0] = _dot(ckvn, wkv_ref[0, :, hw:]).astype(BF16)


def _mixer_in(l, bn, x_b0, tab_b0, x, tabs, wx, wkr, wg, wconv, wco, qn, wq, kvn, wkv, ts, dep):
    _, s_len, d = x.shape
    n_t = s_len // ts
    hw = MLA_HEADS * LANES
    vw = MLA_HEADS * V_HEAD_DIM
    tile = lambda w: pl.BlockSpec((1, ts, w), lambda b, i: (b, i, 0))
    once = pl.Buffered(1)
    lay = lambda a: pl.BlockSpec((1,) + a.shape[1:], lambda b, i: (l,) + (0,) * (a.ndim - 1),
                                 pipeline_mode=once)
    flat = lambda a: pl.BlockSpec((None,) + a.shape[1:], lambda b, i: (l, 0, 0), pipeline_mode=once)
    halo = pl.BlockSpec((1, HALO_ROWS, d),
                        lambda b, i: (b + x_b0, jnp.maximum(i * (ts // HALO_ROWS) - 1, 0), 0))
    return pl.pallas_call(
        functools.partial(_mixer_in_kernel, d_model=d),
        grid=(bn, n_t),
        in_specs=[pl.BlockSpec((1, ts, d), lambda b, i: (b + x_b0, i, 0)), halo,
                  pl.BlockSpec((1, ts, 3 * LANES), lambda b, i: (b + tab_b0, i, 0)),
                  flat(wx), flat(wkr), flat(wg),
                  lay(wconv), lay(wco), lay(qn), lay(wq), lay(kvn), lay(wkv), _DEP_SPEC],
        out_specs=[tile(d), tile(d), tile(hw), tile(hw), tile(vw)],
        out_shape=[jax.ShapeDtypeStruct((bn, s_len, d), BF16),
                   jax.ShapeDtypeStruct((bn, s_len, d), BF16),
                   jax.ShapeDtypeStruct((bn, s_len, hw), BF16),
                   jax.ShapeDtypeStruct((bn, s_len, hw), BF16),
                   jax.ShapeDtypeStruct((bn, s_len, vw), BF16)],
        compiler_params=pltpu.CompilerParams(
            dimension_semantics=("parallel", "arbitrary"), vmem_limit_bytes=VMEM_LIMIT),
        name="mixer_in",
    )(x, x, tabs, wx, wkr, wg, wconv, wco, qn, wq, kvn, wkv, dep)


def _rope_tables_kernel(cs_ref, place_ref, const_ref, tab_ref):
    c = cs_ref[0]
    hi = c.astype(BF16)
    r1 = c - hi.astype(F32)
    mid = r1.astype(BF16)
    lo = (r1 - mid.astype(F32)).astype(BF16)
    p = place_ref[...]
    tab_ref[0] = (_dot(hi, p) + _dot(mid, p)) + _dot(lo, p) + const_ref[...]


def _rope_tables(cs):
    bn, s_len, w = cs.shape
    half = w // 2
    nope = QK_NOPE_DIM
    src = jnp.arange(w)[:, None]
    dst = jnp.arange(3 * LANES)[None, :]
    lane, tab = dst % LANES, dst // LANES
    is_cos, f = src < half, src % half
    place = (jnp.where((tab == 0) & is_cos & ((lane == nope + f) | (lane == nope + half + f)), 1.0, 0.0)
             + jnp.where((tab == 1) & ~is_cos & (lane == nope + f), -1.0, 0.0)
             + jnp.where((tab == 2) & ~is_cos & (lane == nope + half + f), 1.0, 0.0)).astype(BF16)
    const = jnp.where((dst < nope), 1.0, 0.0).astype(F32)
    ts = min(TABLE_TILE, s_len)
    return pl.pallas_call(
        _rope_tables_kernel,
        grid=(bn, s_len // ts),
        in_specs=[pl.BlockSpec((1, ts, w), lambda b, i: (b, i, 0)),
                  pl.BlockSpec(place.shape, lambda b, i: (0, 0)),
                  pl.BlockSpec(const.shape, lambda b, i: (0, 0))],
        out_specs=pl.BlockSpec((1, ts, 3 * LANES), lambda b, i: (b, i, 0)),
        out_shape=jax.ShapeDtypeStruct((bn, s_len, 3 * LANES), F32),
        compiler_params=pltpu.CompilerParams(
            dimension_semantics=("parallel", "parallel"), vmem_limit_bytes=VMEM_LIMIT),
        name="rope_tables",
    )(cs, place, const)


def _attn_kernel(q_ref, k_ref, v_ref, dep_ref, o_ref, *, tq):
    s_len = q_ref.shape[1]
    n_q = s_len // tq
    lane = lax.broadcasted_iota(jnp.int32, (tq, 2 * V_HEAD_DIM), 1)
    causal = (lax.broadcasted_iota(jnp.int32, (tq, tq), 1)
              <= lax.broadcasted_iota(jnp.int32, (tq, tq), 0))
    for qi in range(n_q):
        past = qi * tq
        rows = slice(past, past + tq)
        outs = []
        for h in range(2):
            sl = slice(h * LANES, (h + 1) * LANES)
            qh = q_ref[0, rows, sl]
            s_d = jnp.where(causal, _dot_nt(qh, k_ref[0, rows, sl]), -jnp.inf)
            m = jnp.max(s_d, axis=-1, keepdims=True)
            if past:
                s_p = _dot_nt(qh, k_ref[0, :past, sl])
                m = jnp.maximum(m, jnp.max(s_p, axis=-1, keepdims=True))
            p_d = jnp.exp(s_d - m)
            den = jnp.sum(p_d, axis=-1, keepdims=True)
            acc = _dot(p_d.astype(BF16), v_ref[0, rows, :])
            if past:
                p_p = jnp.exp(s_p - m)
                den = den + jnp.sum(p_p, axis=-1, keepdims=True)
                acc = acc + _dot(p_p.astype(BF16), v_ref[0, :past, :])
            outs.append(acc / den)
        o_ref[0, rows, :] = jnp.where(lane < V_HEAD_DIM, outs[0], outs[1]).astype(BF16)


def _attention(q, k, v, tq, dep):
    bn, s_len, _ = q.shape
    n_pairs = MLA_HEADS // 2
    return pl.pallas_call(
        functools.partial(_attn_kernel, tq=tq),
        grid=(bn, n_pairs),
        in_specs=[pl.BlockSpec((1, s_len, 2 * LANES), lambda b, h: (b, 0, h)),
                  pl.BlockSpec((1, s_len, 2 * LANES), lambda b, h: (b, 0, h)),
                  pl.BlockSpec((1, s_len, 2 * V_HEAD_DIM), lambda b, h: (b, 0, h)), _DEP_SPEC],
        out_specs=pl.BlockSpec((1, s_len, 2 * V_HEAD_DIM), lambda b, h: (b, 0, h)),
        out_shape=jax.ShapeDtypeStruct((bn, s_len, MLA_HEADS * V_HEAD_DIM), BF16),
        compiler_params=pltpu.CompilerParams(
            dimension_semantics=("parallel", "parallel"), vmem_limit_bytes=VMEM_LIMIT),
        name="mla_attention",
    )(q, k, v, dep)


def _mem_kv_kernel(mem_ref, wk_ref, wv_ref, k_ref, v_ref):
    mb = mem_ref[0].astype(BF16)
    k_ref[0, 0] = _dot(mb, wk_ref[0]).astype(BF16)
    v_ref[0, 0] = _dot(mb, wv_ref[0]).astype(BF16)


def _mem_kv(mem, w_xk, w_xv):
    bn, m_len, d = mem.shape
    n_l = w_xk.shape[0]
    wspec = pl.BlockSpec((1, d, d), lambda l, b: (l, 0, 0))
    ospec = pl.BlockSpec((1, 1, m_len, d), lambda l, b: (l, b, 0, 0))
    return pl.pallas_call(
        _mem_kv_kernel,
        grid=(n_l, bn),
        in_specs=[pl.BlockSpec((1, m_len, d), lambda l, b: (b, 0, 0)), wspec, wspec],
        out_specs=[ospec, ospec],
        out_shape=[jax.ShapeDtypeStruct((n_l, bn, m_len, d), BF16)] * 2,
        compiler_params=pltpu.CompilerParams(
            dimension_semantics=("arbitrary", "arbitrary"), vmem_limit_bytes=VMEM_LIMIT),
        name="mem_kv",
    )(mem, w_xk, w_xv)


def _first_argmax(vals, idx):
    m = jnp.max(functools.reduce(jnp.maximum, vals), axis=0, keepdims=True)
    big = jnp.int32(1 << 20)
    cand = functools.reduce(jnp.minimum, [jnp.where(v == m, ix, big) for v, ix in zip(vals, idx)])
    first = jnp.min(cand, axis=0, keepdims=True)
    return [ix == first for ix in idx], m, first


def _route(logits_t, bias):
    n = logits_t.shape[1]
    scores = jax.nn.sigmoid(logits_t)
    sel = scores + bias
    sub = lax.broadcasted_iota(jnp.int32, (GROUP_SIZE, n), 0)
    neg = -jnp.inf
    sel_g = [sel[g * GROUP_SIZE:(g + 1) * GROUP_SIZE] for g in range(N_GROUPS)]
    gs = []
    for g in range(N_GROUPS):
        (hot,), m1, _ = _first_argmax([sel_g[g]], [sub])
        m2 = jnp.max(jnp.where(hot, neg, sel_g[g]), axis=0, keepdims=True)
        gs.append(m1 + m2)
    gsv = jnp.concatenate(gs, axis=0)
    gmask = jnp.zeros(gsv.shape, F32)
    for _ in range(TOPK_GROUPS):
        (hot,), _m, _ = _first_argmax([gsv], [sub])
        gmask = jnp.where(hot, 1.0, gmask)
        gsv = jnp.where(hot, neg, gsv)
    msel = [jnp.where(gmask[g:g + 1] > 0.0, sel_g[g], neg) for g in range(N_GROUPS)]
    eidx = [sub + g * GROUP_SIZE for g in range(N_GROUPS)]
    picks, ids = [], []
    for _ in range(TOP_K):
        hots, _m, first = _first_argmax(msel, eidx)
        picks.append(hots)
        ids.append(first)
        msel = [jnp.where(h, neg, v) for h, v in zip(hots, msel)]
    chosen = [functools.reduce(jnp.logical_or, [p[g] for p in picks]) for g in range(N_GROUPS)]
    sc_g = [scores[g * GROUP_SIZE:(g + 1) * GROUP_SIZE] for g in range(N_GROUPS)]
    picked = [jnp.where(c, s, 0.0) for c, s in zip(chosen, sc_g)]
    wsum = jnp.sum(functools.reduce(lambda a, b: a + b, picked), axis=0, keepdims=True)
    gates = [p / wsum * ROUTE_SCALE for p in picked]
    return picks, ids, chosen, gates


def _pick_rows(hots, vals):
    acc = functools.reduce(lambda a, b: a + b, [jnp.where(h, v, 0.0) for h, v in zip(hots, vals)])
    return jnp.sum(acc, axis=0, keepdims=True)


def _post_kernel(x_ref, yag_ref, sgb_ref, o_ref, km_ref, vm_ref, wao_ref, wmo_ref, ln1_ref,
                 wxq_ref, wxo_ref, ln2_ref, wr_ref, dep_ref,
                 x2p_ref, x2_ref, logit_ref, *, alpha):
    d = x_ref.shape[2]
    x = x_ref[0]
    y_b = _dot(o_ref[0], wao_ref[0])
    y = yag_ref[0].astype(F32) + sgb_ref[0].astype(F32) * y_b
    mix = _dot(y.astype(BF16), wmo_ref[0])
    x1 = _layer_norm(alpha * x + mix, ln1_ref[0, 0:1], ln1_ref[0, 1:2])

    hd = d // XATTN_HEADS
    xq = (_dot(x1.astype(BF16), wxq_ref[0]) * (hd ** -0.5)).astype(BF16)
    heads = []
    for h in range(XATTN_HEADS):
        sl = slice(h * hd, (h + 1) * hd)
        s = _dot_nt(xq[:, sl], km_ref[0, 0, :, sl])
        m = jnp.max(s, axis=-1, keepdims=True)
        p = jnp.exp(s - m)
        den = jnp.sum(p, axis=-1, keepdims=True)
        heads.append((_dot(p.astype(BF16), vm_ref[0, 0, :, sl]) / den).astype(BF16))
    xat = _dot(jnp.concatenate(heads, axis=-1), wxo_ref[0])
    x2 = _layer_norm(alpha * x1 + xat, ln2_ref[0, 0:1], ln2_ref[0, 1:2])

    x2b = x2.astype(BF16)
    x2l = (x2 - x2b.astype(F32)).astype(BF16)
    wr = wr_ref[0]
    wrh = wr.astype(BF16)
    wrl = (wr - wrh.astype(F32)).astype(BF16)
    logit_ref[0] = _dot_nt(wrh, x2b) + (_dot_nt(wrh, x2l) + _dot_nt(wrl, x2b))

    x2p_ref[0] = _pack_rows(x2)
    x2_ref[0] = x2


def _post(l, bn, x_b0, mem_b0, x, yag, sgb, o, kmem, vmem, wao, wmo, ln1, wxq, wxo, ln2, wr,
          ts, alpha, dep):
    _, s_len, d = x.shape
    m_len = kmem.shape[2]
    tile = lambda w: pl.BlockSpec((1, ts, w), lambda b, i: (b, i, 0))
    lay = lambda a: pl.BlockSpec((1,) + a.shape[1:], lambda b, i: (l,) + (0,) * (a.ndim - 1),
                                 pipeline_mode=pl.Buffered(1))
    memspec = pl.BlockSpec((1, 1, m_len, d), lambda b, i: (l, b + mem_b0, 0, 0))
    return pl.pallas_call(
        functools.partial(_post_kernel, alpha=alpha),
        grid=(bn, s_len // ts),
        in_specs=[pl.BlockSpec((1, ts, d), lambda b, i: (b + x_b0, i, 0)),
                  tile(d), tile(d), tile(o.shape[2]), memspec, memspec,
                  lay(wao), lay(wmo), lay(ln1), lay(wxq), lay(wxo), lay(ln2), lay(wr), _DEP_SPEC],
        out_specs=[tile(d // 2), tile(d),
                   pl.BlockSpec((1, N_EXPERTS, ts), lambda b, i: (b, 0, i))],
        out_shape=[jax.ShapeDtypeStruct((bn, s_len, d // 2), I32),
                   jax.ShapeDtypeStruct((bn, s_len, d), F32),
                   jax.ShapeDtypeStruct((bn, N_EXPERTS, s_len), F32)],
        compiler_params=pltpu.CompilerParams(
            dimension_semantics=("parallel", "arbitrary"), vmem_limit_bytes=VMEM_LIMIT),
        name="post_mixer",
    )(x, yag, sgb, o, kmem, vmem, wao, wmo, ln1, wxq, wxo, ln2, wr, dep)


RANK_BLOCK = 512


def _route_kernel(lg_ref, rb_ref, eid_ref, rank_ref, gate_ref, cnt_ref):
    @pl.when(pl.program_id(0) == 0)
    def _():
        cnt_ref[...] = jnp.zeros_like(cnt_ref)

    n = lg_ref.shape[2]
    blk = min(RANK_BLOCK, n)
    picks, ids, chosen, gates = _route(lg_ref[0], rb_ref[0])

    sel_t = jnp.concatenate([c.astype(F32) for c in chosen], axis=0)
    before = (lax.broadcasted_iota(jnp.int32, (blk, blk), 0)
              < lax.broadcasted_iota(jnp.int32, (blk, blk), 1)).astype(BF16)
    running = cnt_ref[:, 0:1]
    ranks = []
    for j in range(n // blk):
        sb = sel_t[:, j * blk:(j + 1) * blk]
        ranks.append(_dot(sb.astype(BF16), before) + running)
        running = running + jnp.sum(sb, axis=1, keepdims=True)
    cnt_ref[...] = jnp.broadcast_to(running, cnt_ref.shape)
    rank_all = jnp.concatenate(ranks, axis=1)
    rank_g = [rank_all[g * GROUP_SIZE:(g + 1) * GROUP_SIZE] for g in range(N_GROUPS)]
    eid_ref[0] = jnp.concatenate(ids, axis=0)
    rank_ref[0] = jnp.concatenate([_pick_rows(h, rank_g) for h in picks], axis=0).astype(I32)
    gate_ref[0] = jnp.concatenate([_pick_rows(h, gates) for h in picks], axis=0)


def _route_call(l, logits, rb):
    bn, n_e, s_len = logits.shape
    pick = pl.BlockSpec((1, TOP_K, s_len), lambda b: (b, 0, 0))
    return pl.pallas_call(
        _route_kernel,
        grid=(bn,),
        in_specs=[pl.BlockSpec((1, n_e, s_len), lambda b: (b, 0, 0)),
                  pl.BlockSpec((1,) + rb.shape[1:], lambda b: (l, 0, 0))],
        out_specs=[pick, pick, pick, pl.BlockSpec((n_e, LANES), lambda b: (0, 0))],
        out_shape=[jax.ShapeDtypeStruct((bn, TOP_K, s_len), I32),
                   jax.ShapeDtypeStruct((bn, TOP_K, s_len), I32),
                   jax.ShapeDtypeStruct((bn, TOP_K, s_len), F32),
                   jax.ShapeDtypeStruct((n_e, LANES), F32)],
        compiler_params=pltpu.CompilerParams(
            dimension_semantics=("arbitrary",), vmem_limit_bytes=VMEM_LIMIT),
        name="moe_route",
    )(logits, rb)


def _slot_kernel(offs_ref, eid_ref, rank_ref, slot_ref):
    eid = eid_ref[0]
    acc = rank_ref[0]
    for e in range(N_EXPERTS):
        acc = acc + jnp.where(eid == e, offs_ref[e], 0)
    slot_ref[0] = acc


def _slots(offs, eid, rank):
    bn, k, s_len = eid.shape
    spec = pl.BlockSpec((1, k, s_len), lambda b, offs_ref: (b, 0, 0))
    return pl.pallas_call(
        _slot_kernel,
        grid_spec=pltpu.PrefetchScalarGridSpec(
            num_scalar_prefetch=1, grid=(bn,), in_specs=[spec, spec], out_specs=spec),
        out_shape=jax.ShapeDtypeStruct((bn, k, s_len), I32),
        name="moe_slots",
    )(offs, eid, rank)


def _sc_mesh():
    return plsc.VectorSubcoreMesh(core_axis_name="c", subcore_axis_name="s")


def _sc_dispatch(xp, slot_chunks, n_slots, dep):
    t, w = xp.shape
    n_chunks, k, ch = slot_chunks.shape
    per_worker = n_chunks // (SC_CORES * SC_SUBCORES)

    @functools.partial(
        pl.kernel, mesh=_sc_mesh(),
        out_type=jax.ShapeDtypeStruct((n_slots, w), I32),
        scratch_types=[pltpu.VMEM((k, ch), I32), pltpu.VMEM((ch, w), I32), pltpu.SemaphoreType.DMA],
        name="moe_dispatch")
    def body(x_hbm, slot_hbm, dep_hbm, out_hbm, idx_v, rows_v, sem):
        worker = lax.axis_index("s") * SC_CORES + lax.axis_index("c")

        @pl.loop(0, per_worker)
        def _(j):
            c = worker * per_worker + j
            pltpu.sync_copy(slot_hbm.at[c], idx_v)
            pltpu.sync_copy(x_hbm.at[pl.ds(c * ch, ch)], rows_v)
            copies = [pltpu.async_copy(rows_v, out_hbm.at[idx_v.at[kk]], sem) for kk in range(k)]
            for cp in copies:
                cp.wait()

    return body(xp, slot_chunks, dep)


def _sc_combine(ys, slot_chunks, t):
    w = ys.shape[1]
    n_chunks, k, ch = slot_chunks.shape
    per_worker = n_chunks // (SC_CORES * SC_SUBCORES)

    @functools.partial(
        pl.kernel, mesh=_sc_mesh(),
        out_type=jax.ShapeDtypeStruct((k, t, w), I32),
        scratch_types=[pltpu.VMEM((k, ch), I32), pltpu.VMEM((ch, w), I32), pltpu.VMEM((ch, w), I32),
                       pltpu.SemaphoreType.DMA, pltpu.SemaphoreType.DMA,
                       pltpu.SemaphoreType.DMA, pltpu.SemaphoreType.DMA],
        name="moe_combine")
    def body(y_hbm, slot_hbm, out_hbm, idx_v, rows0, rows1, g0, g1, w0, w1):
        worker = lax.axis_index("s") * SC_CORES + lax.axis_index("c")
        bufs, gsem, wsem = (rows0, rows1), (g0, g1), (w0, w1)

        @pl.loop(0, per_worker)
        def _(j):
            c = worker * per_worker + j
            pltpu.sync_copy(slot_hbm.at[c], idx_v)
            gathers = [None] * k
            writes = [None] * k
            gathers[0] = pltpu.async_copy(y_hbm.at[idx_v.at[0]], bufs[0], gsem[0])
            for kk in range(k):
                cur = kk % 2
                if kk + 1 < k:
                    if kk >= 1:
                        writes[kk - 1].wait()
                    gathers[kk + 1] = pltpu.async_copy(
                        y_hbm.at[idx_v.at[kk + 1]], bufs[1 - cur], gsem[1 - cur])
                gathers[kk].wait()
                writes[kk] = pltpu.async_copy(
                    bufs[cur], out_hbm.at[kk, pl.ds(c * ch, ch)], wsem[cur])
            writes[k - 2].wait()
            writes[k - 1].wait()

    return body(ys, slot_chunks)


def _experts_kernel(ti_ref, te_ref, nv_ref, eo_ref, ne_ref, nu_ref, xs_ref, wg_hbm, wu_hbm, wd_hbm,
                    dep_ref, ys_ref, wg_f, wu_f, wd_f, wg_b, wu_b, wd_b, sem, *, layer):
    j = pl.program_id(0)
    expert = te_ref[j]
    slot = eo_ref[j]
    prev = te_ref[jnp.maximum(j - 1, 0)]
    new_expert = jnp.logical_or(j == 0, expert != prev)

    def weight_copies(e, s):
        return [pltpu.make_async_copy(hbm.at[layer, e], buf.at[s], sem.at[s, i])
                for i, (hbm, buf) in enumerate(((wg_hbm, wg_f), (wu_hbm, wu_f), (wd_hbm, wd_f)))]

    @pl.when(j == 0)
    def _():
        for cp in weight_copies(expert, slot):
            cp.start()

    @pl.when(new_expert)
    def _():
        nxt = ne_ref[j]

        @pl.when(nxt >= 0)
        def _():
            for cp in weight_copies(nxt, 1 - slot):
                cp.start()

        for cp in weight_copies(expert, slot):
            cp.wait()
        wg_b[...] = wg_f[slot].astype(BF16)
        wu_b[...] = wu_f[slot].astype(BF16)
        wd_b[...] = wd_f[slot].astype(BF16)

    def ffn(n_rows):
        rows = pl.ds(0, n_rows)
        xb = _unpack_rows(xs_ref[rows, :]).astype(BF16)
        hid = _silu(_dot(xb, wg_b[...])) * _dot(xb, wu_b[...])
        ys_ref[rows, :] = _pack_rows(_dot(hid.astype(BF16), wd_b[...]))

    used = j < nu_ref[0]
    n_paths = xs_ref.shape[0] // EXPERT_ROWS
    groups = jnp.minimum((nv_ref[j] + EXPERT_ROWS - 1) // EXPERT_ROWS, n_paths)
    for g in range(1, n_paths + 1):
        @pl.when(jnp.logical_and(used, groups == g))
        def _(g=g):
            ffn(g * EXPERT_ROWS)


def _experts(l, xs, sched, w_gate, w_up, w_down, dep):
    n_slots, w = xs.shape
    n_tiles = n_slots // EXPERT_TILE
    d, f = w_gate.shape[2:]
    rows = pl.BlockSpec((EXPERT_TILE, w), lambda j, ti, *_: (ti[j], 0))
    hbm = pl.BlockSpec(memory_space=pl.ANY)
    return pl.pallas_call(
        functools.partial(_experts_kernel, layer=l),
        grid_spec=pltpu.PrefetchScalarGridSpec(
            num_scalar_prefetch=len(sched), grid=(sched[-1][0],),
            in_specs=[rows, hbm, hbm, hbm, _DEP_SPEC],
            out_specs=rows,
            scratch_shapes=[pltpu.VMEM((2, d, f), F32), pltpu.VMEM((2, d, f), F32),
                            pltpu.VMEM((2, f, d), F32),
                            pltpu.VMEM((d, f), BF16), pltpu.VMEM((d, f), BF16),
                            pltpu.VMEM((f, d), BF16), pltpu.SemaphoreType.DMA((2, 3))]),
        out_shape=jax.ShapeDtypeStruct((n_slots, w), I32),
        compiler_params=pltpu.CompilerParams(
            dimension_semantics=("arbitrary",), vmem_limit_bytes=VMEM_LIMIT),
        name="moe_experts",
    )(*sched, xs, w_gate, w_up, w_down, dep)


def _final_kernel(yg_ref, gate_ref, x2_ref, ln_ref, wsgu_ref, wsd_ref, *rest, alpha):
    out_ref = rest[-1]
    x2 = x2_ref[0]
    sh = _dot(x2.astype(BF16), wsgu_ref[0])
    f = sh.shape[1] // 2
    hid = _silu(sh[:, :f]) * sh[:, f:]
    acc = alpha * x2 + _dot(hid.astype(BF16), wsd_ref[0])
    gates = gate_ref[0].T
    for k in range(yg_ref.shape[0]):
        acc = acc + gates[:, k:k + 1] * _unpack_rows(yg_ref[k, 0])
    out_ref[0] = _layer_norm(acc, ln_ref[0, 0:1], ln_ref[0, 1:2])


def _final(l, yg, gate8, x2, ln3, wsgu, wsd, tm, alpha, out_rows, out_b0, out_prev, dep):
    bn, s_len, d = x2.shape
    k = yg.shape[0]
    lay = lambda a: pl.BlockSpec((1,) + a.shape[1:], lambda b, i: (l, 0, 0),
                                 pipeline_mode=pl.Buffered(1))
    in_specs = [pl.BlockSpec((k, 1, tm, d // 2), lambda b, i: (0, b, i, 0)),
                pl.BlockSpec((1, k, tm), lambda b, i: (b, 0, i)),
                pl.BlockSpec((1, tm, d), lambda b, i: (b, i, 0)),
                lay(ln3), lay(wsgu), lay(wsd), _DEP_SPEC]
    args = [yg, gate8, x2, ln3, wsgu, wsd, dep]
    aliases = {}
    if out_prev is not None:
        in_specs.append(pl.BlockSpec(memory_space=pl.ANY))
        args.append(out_prev)
        aliases = {len(args) - 1: 0}
    return pl.pallas_call(
        functools.partial(_final_kernel, alpha=alpha),
        grid=(bn, s_len // tm),
        in_specs=in_specs,
        out_specs=pl.BlockSpec((1, tm, d), lambda b, i: (b + out_b0, i, 0)),
        out_shape=jax.ShapeDtypeStruct((out_rows, s_len, d), F32),
        input_output_aliases=aliases,
        compiler_params=pltpu.CompilerParams(
            dimension_semantics=("parallel", "parallel"), vmem_limit_bytes=VMEM_LIMIT),
        name="moe_final",
    )(*args)


def _moe_schedule(eid, rank, counts):
    bn, _, s_len = eid.shape
    t = bn * s_len
    n_tiles = (t * TOP_K) // EXPERT_TILE + N_EXPERTS

    cnt = counts[:, 0].astype(I32)
    padded = (cnt + EXPERT_TILE - 1) // EXPERT_TILE * EXPERT_TILE
    ends = jnp.cumsum(padded)
    offs = ends - padded
    n_used = ends[-1] // EXPERT_TILE
    ti = jnp.minimum(jnp.arange(n_tiles, dtype=I32), n_used - 1)
    te = jnp.sum((ends[None, :] <= (ti * EXPERT_TILE)[:, None]).astype(I32), axis=1)
    nv = jnp.take(cnt + offs, te) - ti * EXPERT_TILE
    nonempty = cnt > 0
    idx = jnp.arange(N_EXPERTS, dtype=I32)
    order = jnp.cumsum(nonempty.astype(I32)) - 1
    later = jnp.where(nonempty[None, :] & (idx[None, :] > idx[:, None]), idx[None, :], N_EXPERTS)
    succ = jnp.min(later, axis=1)
    succ = jnp.where(succ == N_EXPERTS, -1, succ).astype(I32)
    eo = jnp.take(order, te) % 2
    ne = jnp.take(succ, te)

    slots = _slots(offs, eid, rank)
    slot_chunks = slots.reshape(bn, TOP_K, s_len // SC_CHUNK, SC_CHUNK).transpose(0, 2, 1, 3)
    slot_chunks = slot_chunks.reshape(t // SC_CHUNK, TOP_K, SC_CHUNK)
    return slot_chunks, (ti, te, nv, eo, ne, n_used.reshape(1)), n_tiles * EXPERT_TILE


def _head_chunks(w, widths, n_heads, per_head):
    lead = w.shape[:-1]
    wh = w.reshape(lead + (n_heads, per_head))
    parts = [sign * wh[..., a:b] for a, b, sign in widths]
    used = sum(b - a for a, b, _ in widths)
    parts.append(jnp.zeros(lead + (n_heads, LANES - used), w.dtype))
    return jnp.concatenate(parts, axis=-1).reshape(lead + (n_heads * LANES,))


def kernel(x, mem, positions, w_in, w_conv, w_conv_out, q_norm, w_uq, kv_norm, w_uk, w_uv,
           w_attn_out, w_mix_out, ln1_g, ln1_b, w_xq, w_xk, w_xv, w_xo, ln2_g, ln2_b,
           w_router, router_bias, w_gate, w_up, w_down, ws_gate, ws_up, ws_down, ln3_g, ln3_b):
    bn, s_len, d = x.shape
    depth = w_in.shape[0]
    alpha = (2 * depth) ** 0.25
    tq = min(ATTN_Q_TILE, s_len)
    ts = min(ROW_TILE, s_len)
    half = QK_ROPE_DIM // 2
    nope, rope = QK_NOPE_DIM, QK_ROPE_DIM

    inv_freq = ROPE_BASE ** (-jnp.arange(half, dtype=F32) / half)
    ang = positions.astype(F32)[..., None] * inv_freq
    cos, sin = jnp.cos(ang), jnp.sin(ang)
    scale = (nope + rope) ** -0.5
    tail = LANES - nope - rope
    tabs = _rope_tables(jnp.concatenate([cos, sin], axis=-1))

    sizes = (CONV_DIM, CONV_DIM, CONV_DIM, Q_RANK, KV_RANK, rope, d, d)
    offs = [0]
    for sz in sizes:
        offs.append(offs[-1] + sz)
    zpad = lambda w: jnp.zeros(w_in.shape[:2] + (w,), w_in.dtype)
    wx = w_in[:, :, :offs[5]].astype(BF16)
    wkr = jnp.concatenate([zpad(nope), w_in[:, :, offs[5]:offs[6]], zpad(tail)], axis=-1).astype(BF16)
    wg = w_in[:, :, offs[6]:].astype(BF16)

    per_q = nope + rope
    wq = _head_chunks(w_uq, [(0, per_q, scale)], MLA_HEADS, per_q).astype(BF16)
    wk = _head_chunks(w_uk, [(0, nope, 1.0)], MLA_HEADS, nope)
    wkv = jnp.concatenate([wk, w_uv], axis=-1).astype(BF16)

    wconv = w_conv[:, :, 0, :]
    wco = w_conv_out.astype(BF16)
    qn = q_norm[:, None, :]
    kvn = kv_norm[:, None, :]
    wao = w_attn_out.astype(BF16)
    wmo = w_mix_out.astype(BF16)
    ln1 = jnp.stack([ln1_g, ln1_b], axis=1)
    ln2 = jnp.stack([ln2_g, ln2_b], axis=1)
    ln3 = jnp.stack([ln3_g, ln3_b], axis=1)
    wxq = w_xq.astype(BF16)
    wxo = w_xo.astype(BF16)
    wr = jnp.swapaxes(w_router, 1, 2)
    rb = router_bias[:, :, None]
    wsgu = jnp.concatenate([ws_gate, ws_up], axis=-1).astype(BF16)
    wsd = ws_down.astype(BF16)

    kmem, vmem = _mem_kv(mem, w_xk.astype(BF16), w_xv.astype(BF16))

    assert bn % 2 == 0
    hb = bn // 2
    t = hb * s_len
    tm = min(FINAL_TILE, s_len)

    def mix(l, c, xc, dep):
        x_b0 = c * hb if l == 0 else 0
        return _mixer_in(l, hb, x_b0, c * hb, xc, tabs, wx, wkr, wg, wconv, wco, qn, wq, kvn, wkv,
                         ts, dep)

    def attn(mx, dep):
        yag, sgb, q, k, v = mx
        return yag, sgb, _attention(q, k, v, tq, dep)

    def post_route(l, c, xc, ma, dep):
        x_b0 = c * hb if l == 0 else 0
        yag, sgb, o = ma
        x2p, x2, logits = _post(l, hb, x_b0, c * hb, xc, yag, sgb, o, kmem, vmem, wao, wmo, ln1,
                                wxq, wxo, ln2, wr, ts, alpha, dep)
        eid, rank, gate8, counts = _route_call(l, logits, rb)
        slot_chunks, sched, n_slots = _moe_schedule(eid, rank, counts)
        return dict(x2p=x2p.reshape(t, d // 2), x2=x2, gate8=gate8, slot_chunks=slot_chunks,
                    sched=sched, n_slots=n_slots)

    def dispatch(st, dep):
        st["xs"] = _sc_dispatch(st["x2p"], st["slot_chunks"], st["n_slots"], dep)

    def experts(l, st, dep):
        st["ys"] = _experts(l, st["xs"], st["sched"], w_gate, w_up, w_down, dep)
        st["yg"] = _sc_combine(st["ys"], st["slot_chunks"], t).reshape(TOP_K, hb, s_len, d // 2)

    def final(l, c, st, dep, out_prev):
        if l == depth - 1:
            return _final(l, st["yg"], st["gate8"], st["x2"], ln3, wsgu, wsd, tm, alpha, bn, c * hb,
                          out_prev, dep)
        return _final(l, st["yg"], st["gate8"], st["x2"], ln3, wsgu, wsd, tm, alpha, hb, 0, None, dep)

    xa = xb = x
    mx_a = mix(0, 0, xa, x)
    st_a = post_route(0, 0, xa, attn(mx_a, mx_a[0]), mx_a[0])
    dispatch(st_a, st_a["gate8"])
    for l in range(depth):
        more = l + 1 < depth
        mx_b = mix(l, 1, xb, st_a["gate8"])
        experts(l, st_a, mx_b[0])
        st_b = post_route(l, 1, xb, attn(mx_b, st_a["ys"]), st_a["ys"])
        xa = final(l, 0, st_a, st_b["gate8"], None)
        dispatch(st_b, xa if more else st_b["gate8"])
        if more:
            mx_a = mix(l + 1, 0, xa, xa)
        experts(l, st_b, mx_a[0] if more else xa)
        if more:
            st_a = post_route(l + 1, 0, xa, attn(mx_a, st_b["ys"]), st_b["ys"])
        xb = final(l, 1, st_b, st_a["gate8"] if more else st_b["ys"], xa if not more else None)
        if more:
            dispatch(st_a, xb)
    return xb
```
